```python
import jax, jax.numpy as jnp
from jax import lax
import numpy as np

D_MODEL = 2048
BATCH = 1
SEQ = 8192
DEPTH = 2
DEC_BATCH = 8
DEC_SEQ = 32
PAST_LEN = 4096

CHUNK = 64
D_CONV = 1024
CONV_W = 3
N_HEADS_ATT = 8
HEAD_DIM_ATT = 128
D_ATT = N_HEADS_ATT * HEAD_DIM_ATT
BAND_CHUNKS = 8
ATT_REACH = BAND_CHUNKS * CHUNK
REL_CLIP = 128
N_HEADS_MLSTM = 4
HEAD_DIM_MLSTM = 256
D_MLSTM = N_HEADS_MLSTM * HEAD_DIM_MLSTM
N_BRANCH = 3
IN_WIDTH = 3 * D_CONV + 3 * D_ATT + 4 * D_MLSTM + 2 * N_HEADS_MLSTM + N_BRANCH * D_MODEL
N_GROUPS = 4
EXPERTS_PER_GROUP = 4
N_EXPERTS = N_GROUPS * EXPERTS_PER_GROUP
TOP_K = 2
D_EXPERT = 512
EPS = 1e-6

kernel_name = 'hybrid_streaming_encoder_step'


def _rmsnorm(x, g):
    xf = x.astype(jnp.float32)
    xn = xf * lax.rsqrt(jnp.mean(xf * xf, axis=-1, keepdims=True) + EPS)
    return xn.astype(x.dtype) * g


def _head_layernorm(h, g):
    B, T, H, Dh = h.shape
    hf = h.astype(jnp.float32)
    mu = jnp.mean(hf, axis=-1, keepdims=True)
    var = jnp.mean(jnp.square(hf - mu), axis=-1, keepdims=True)
    hn = (hf - mu) * lax.rsqrt(var + EPS)
    return hn.reshape(B, T, H * Dh).astype(h.dtype) * g


def _split_points():
    widths = [D_CONV] * 3 + [D_ATT] * 3 + [D_MLSTM] * 4 + [2 * N_HEADS_MLSTM]
    points, acc = [], 0
    for w in widths:
        acc += w
        points.append(acc)
    return points


def _short_conv(xa, gb, gc, conv_w, prev):
    T = xa.shape[1]
    u = gc * xa
    up = jnp.concatenate([prev.astype(u.dtype), u], axis=1)
    y = conv_w[0] * up[:, 0:T]
    for j in range(1, CONV_W):
        y = y + conv_w[j] * up[:, j:j + T]
    return gb * y, up[:, -(CONV_W - 1):]


def _rel_bias(table, dist):
    return table[:, jnp.clip(dist, -REL_CLIP, REL_CLIP) + REL_CLIP]


def _band_attention_prompt(q, k, v, rel_bias):
    B, T, H, Dh = q.shape
    nc = T // CHUNK
    band = (BAND_CHUNKS + 1) * CHUNK
    qc = q.reshape(B, nc, CHUNK, H, Dh)
    pad = jnp.zeros((B, BAND_CHUNKS * CHUNK, H, Dh), k.dtype)
    kp = jnp.concatenate([pad, k], axis=1).reshape(B, nc + BAND_CHUNKS, CHUNK, H, Dh)
    vp = jnp.concatenate([pad.astype(v.dtype), v], axis=1).reshape(B, nc + BAND_CHUNKS, CHUNK, H, Dh)
    kb = jnp.concatenate([kp[:, j:j + nc] for j in range(BAND_CHUNKS + 1)], axis=2)
    vb = jnp.concatenate([vp[:, j:j + nc] for j in range(BAND_CHUNKS + 1)], axis=2)
    s = jnp.einsum('bclhd,bcmhd->bchlm', qc, kb).astype(jnp.float32) * (Dh ** -0.5)
    t = jnp.arange(CHUNK)
    u = jnp.arange(band)
    dist = BAND_CHUNKS * CHUNK + t[:, None] - u[None, :]
    s = s + _rel_bias(rel_bias, dist).astype(jnp.float32)[None, None]
    key_chunk = jnp.arange(nc)[:, None] - BAND_CHUNKS + u[None, :] // CHUNK
    s = jnp.where((key_chunk >= 0)[None, :, None, None, :], s, -jnp.inf)
    p = jax.nn.softmax(s, axis=-1).astype(v.dtype)
    o = jnp.einsum('bchlm,bcmhd->bclhd', p, vb)
    return o.reshape(B, T, H * Dh)


def _band_attention_step(q, k, v, k_past, v_past, rel_bias):
    B, S, H, Dh = q.shape
    P = k_past.shape[1]
    kk = jnp.concatenate([k_past.astype(k.dtype), k], axis=1)
    vv = jnp.concatenate([v_past.astype(v.dtype), v], axis=1)
    s = jnp.einsum('bshd,bmhd->bhsm', q, kk).astype(jnp.float32) * (Dh ** -0.5)
    key_pos = jnp.concatenate([jnp.arange(P) - P, jnp.arange(S)])
    dist = jnp.arange(S)[:, None] - key_pos[None, :]
    s = s + _rel_bias(rel_bias, dist).astype(jnp.float32)[None]
    p = jax.nn.softmax(s, axis=-1).astype(v.dtype)
    o = jnp.einsum('bhsm,bmhd->bshd', p, vv)
    return o.reshape(B, S, H * Dh)


def _mlstm_chunk(state, xs):
    C0, n0, m0 = state
    q, k, v, logf, logi = xs
    L = q.shape[2]
    b = jnp.cumsum(logf.astype(jnp.float32), axis=-1)
    logi = logi.astype(jnp.float32)
    m0f = m0.astype(jnp.float32)
    D = b[..., :, None] - b[..., None, :] + logi[..., None, :]
    causal = jnp.tril(jnp.ones((L, L), dtype=bool))
    D = jnp.where(causal, D, -jnp.inf)
    inter = b + m0f[..., None]
    m = jnp.maximum(inter, jnp.max(D, axis=-1))
    w = jnp.exp(D - m[..., None])
    sc = jnp.exp(inter - m)
    qk = jnp.einsum('bhtd,bhsd->bhts', q, k).astype(jnp.float32) * w
    num = sc[..., None] * jnp.einsum('bhvk,bhtk->bhtv', C0, q).astype(jnp.float32) + jnp.einsum('bhts,bhsv->bhtv', qk, v.astype(jnp.float32))
    den = sc * jnp.einsum('bhk,bhtk->bht', n0, q).astype(jnp.float32) + jnp.sum(qk, axis=-1)
    h = num / jnp.maximum(jnp.abs(den), jnp.exp(-m))[..., None]
    m_last = m[..., -1]
    b_last = b[..., -1]
    decay = jnp.exp(b_last + m0f - m_last)
    ws = jnp.exp(b_last[..., None] - b + logi - m_last[..., None])
    C_new = decay[..., None, None] * C0.astype(jnp.float32) + jnp.einsum('bhs,bhsv,bhsk->bhvk', ws, v.astype(jnp.float32), k.astype(jnp.float32))
    n_new = decay[..., None] * n0.astype(jnp.float32) + jnp.einsum('bhs,bhsk->bhk', ws, k.astype(jnp.float32))
    return (C_new.astype(C0.dtype), n_new.astype(n0.dtype), m_last.astype(m0.dtype)), h.astype(q.dtype)


def _mlstm_prompt(q, k, v, logf, logi, C0, n0, m0):
    B, T, H, Dh = q.shape
    nc = T // CHUNK

    def to_blocks(a):
        a = a.reshape((B, nc, CHUNK, H) + a.shape[3:])
        return jnp.moveaxis(a, (1, 3), (0, 2))

    state, h = lax.scan(_mlstm_chunk, (C0, n0, m0), (to_blocks(q), to_blocks(k), to_blocks(v), to_blocks(logf), to_blocks(logi)))
    h = jnp.transpose(h, (1, 0, 3, 2, 4)).reshape(B, T, H, Dh)
    return h, state


def _token_mixer(h, w_in, conv_w, rel_bias, gate_bias, mlstm_norm, w_proj_conv, w_proj_att, w_proj_mlstm, w_out, st):
    B, T, _ = h.shape
    z = jnp.einsum('btd,de->bte', h, w_in)
    xa, gb, gc, q, k, v, qm, km, vm, om, gif, glog = jnp.split(z, _split_points(), axis=-1)
    if st is None:
        conv_prev = jnp.zeros((B, CONV_W - 1, D_CONV), h.dtype)
        C0 = jnp.zeros((B, N_HEADS_MLSTM, HEAD_DIM_MLSTM, HEAD_DIM_MLSTM), jnp.float32)
        n0 = jnp.zeros((B, N_HEADS_MLSTM, HEAD_DIM_MLSTM), jnp.float32)
        m0 = jnp.zeros((B, N_HEADS_MLSTM), jnp.float32)
        k_past = v_past = None
    else:
        conv_prev, k_past, v_past, C0, n0, m0 = st
    ya, conv_new = _short_conv(xa, gb, gc, conv_w, conv_prev)
    q = q.reshape(B, T, N_HEADS_ATT, HEAD_DIM_ATT)
    k = k.reshape(B, T, N_HEADS_ATT, HEAD_DIM_ATT)
    v = v.reshape(B, T, N_HEADS_ATT, HEAD_DIM_ATT)
    if st is None:
        yb = _band_attention_prompt(q, k, v, rel_bias)
        keep = min(ATT_REACH, T)
        k_rows, v_rows = k[:, T - keep:], v[:, T - keep:]
    else:
        yb = _band_attention_step(q, k, v, k_past, v_past, rel_bias)
        k_rows, v_rows = k, v
    qm = qm.reshape(B, T, N_HEADS_MLSTM, HEAD_DIM_MLSTM)
    km = km.reshape(B, T, N_HEADS_MLSTM, HEAD_DIM_MLSTM) * (HEAD_DIM_MLSTM ** -0.5)
    vm = vm.reshape(B, T, N_HEADS_MLSTM, HEAD_DIM_MLSTM)
    gif = gif.astype(jnp.float32) + gate_bias.astype(jnp.float32)
    logi = gif[..., :N_HEADS_MLSTM]
    logf = jax.nn.log_sigmoid(gif[..., N_HEADS_MLSTM:])
    if st is None:
        hm, (C1, n1, m1) = _mlstm_prompt(qm, km, vm, logf, logi, C0, n0, m0)
    else:
        (C1, n1, m1), hm = _mlstm_chunk((C0, n0, m0), (qm.transpose(0, 2, 1, 3), km.transpose(0, 2, 1, 3), vm.transpose(0, 2, 1, 3), logf.transpose(0, 2, 1), logi.transpose(0, 2, 1)))
        hm = hm.transpose(0, 2, 1, 3)
    yc = jax.nn.sigmoid(om) * _head_layernorm(hm, mlstm_norm)
    g = jax.nn.sigmoid(glog.reshape(B, T, N_BRANCH, D_MODEL))
    merged = (g[:, :, 0] * jnp.einsum('btc,cd->btd', ya, w_proj_conv)
              + g[:, :, 1] * jnp.einsum('btc,cd->btd', yb, w_proj_att)
              + g[:, :, 2] * jnp.einsum('btc,cd->btd', yc, w_proj_mlstm))
    out = jnp.einsum('btd,de->bte', merged, w_out)
    return out, (conv_new, k_rows, v_rows, C1, n1, m1)


def _hier_moe(h, router_group, router_group_bias, router_expert, router_expert_bias, w_gate, w_up, w_down):
    B, T, D = h.shape
    x = h.reshape(B * T, D)
    glog = jnp.einsum('nd,dg->ng', x, router_group).astype(jnp.float32) + router_group_bias.astype(jnp.float32)
    gprob = jax.nn.softmax(glog, axis=-1)
    g_sel = jnp.argmax(glog, axis=-1)
    p_g = jnp.take_along_axis(gprob, g_sel[:, None], axis=-1)
    elog = (jnp.einsum('nd,de->ne', x, router_expert).astype(jnp.float32) + router_expert_bias.astype(jnp.float32)).reshape(-1, N_GROUPS, EXPERTS_PER_GROUP)
    elog_g = jnp.take_along_axis(elog, g_sel[:, None, None], axis=1)[:, 0]
    top_v, top_i = lax.top_k(elog_g, TOP_K)
    wts = jax.nn.softmax(top_v, axis=-1) * p_g
    eidx = g_sel[:, None] * EXPERTS_PER_GROUP + top_i
    combine = jnp.sum(jax.nn.one_hot(eidx, N_EXPERTS, dtype=jnp.float32) * wts[..., None], axis=1)
    a = jax.nn.silu(jnp.einsum('nd,edf->enf', x, w_gate)) * jnp.einsum('nd,edf->enf', x, w_up)
    a = a * combine.T[:, :, None].astype(a.dtype)
    y = jnp.einsum('enf,efd->nd', a, w_down)
    return y.reshape(B, T, D)


def _layer(x, lw, st):
    (norm_mix, norm_ffn, w_in, conv_w, rel_bias, gate_bias, mlstm_norm, w_proj_conv, w_proj_att, w_proj_mlstm,
     w_out, router_group, router_group_bias, router_expert, router_expert_bias, w_gate, w_up, w_down) = lw
    mix, new_st = _token_mixer(_rmsnorm(x, norm_mix), w_in, conv_w, rel_bias, gate_bias, mlstm_norm,
                               w_proj_conv, w_proj_att, w_proj_mlstm, w_out, st)
    x = x + mix
    x = x + _hier_moe(_rmsnorm(x, norm_ffn), router_group, router_group_bias, router_expert, router_expert_bias, w_gate, w_up, w_down)
    return x, new_st


def setup_inputs(seed: int = 0) -> dict:
    key = jax.random.key(seed)
    ks = jax.random.split(key, 40)
    f32 = jnp.float32

    def nrm(k, shape, scale):
        return jax.random.normal(k, shape, f32) * scale

    att_past = min(ATT_REACH, PAST_LEN)
    gate_bias = jnp.concatenate([nrm(ks[11], (DEPTH, N_HEADS_MLSTM), 0.1),
                                 jax.random.uniform(ks[12], (DEPTH, N_HEADS_MLSTM), f32, 3.0, 6.0)], axis=-1)
    return {
        'x_prompt': nrm(ks[0], (BATCH, SEQ, D_MODEL), 1.0),
        'x_sample': nrm(ks[1], (DEC_BATCH, DEC_SEQ, D_MODEL), 1.0),
        'state_conv': nrm(ks[2], (DEPTH, DEC_BATCH, CONV_W - 1, D_CONV), 1.0),
        'cache_k': nrm(ks[3], (DEPTH, DEC_BATCH, att_past, N_HEADS_ATT, HEAD_DIM_ATT), 1.0),
        'cache_v': nrm(ks[4], (DEPTH, DEC_BATCH, att_past, N_HEADS_ATT, HEAD_DIM_ATT), 1.0),
        'state_C': nrm(ks[5], (DEPTH, DEC_BATCH, N_HEADS_MLSTM, HEAD_DIM_MLSTM, HEAD_DIM_MLSTM), 0.05),
        'state_n': nrm(ks[6], (DEPTH, DEC_BATCH, N_HEADS_MLSTM, HEAD_DIM_MLSTM), 0.05),
        'state_m': nrm(ks[7], (DEPTH, DEC_BATCH, N_HEADS_MLSTM), 1.0),
        'norm_mix': 1.0 + nrm(ks[8], (DEPTH, D_MODEL), 0.02),
        'norm_ffn': 1.0 + nrm(ks[9], (DEPTH, D_MODEL), 0.02),
        'w_in': nrm(ks[10], (DEPTH, D_MODEL, IN_WIDTH), D_MODEL ** -0.5),
        'conv_w': nrm(ks[13], (DEPTH, CONV_W, D_CONV), CONV_W ** -0.5),
        'rel_bias': nrm(ks[14], (DEPTH, N_HEADS_ATT, 2 * REL_CLIP + 1), 0.1),
        'gate_bias': gate_bias,
        'mlstm_norm': 1.0 + nrm(ks[15], (DEPTH, D_MLSTM), 0.02),
        'w_proj_conv': nrm(ks[16], (DEPTH, D_CONV, D_MODEL), D_CONV ** -0.5),
        'w_proj_att': nrm(ks[17], (DEPTH, D_ATT, D_MODEL), D_ATT ** -0.5),
        'w_proj_mlstm': nrm(ks[18], (DEPTH, D_MLSTM, D_MODEL), D_MLSTM ** -0.5),
        'w_out': nrm(ks[19], (DEPTH, D_MODEL, D_MODEL), D_MODEL ** -0.5),
        'router_group': nrm(ks[20], (DEPTH, D_MODEL, N_GROUPS), D_MODEL ** -0.5),
        'router_group_bias': nrm(ks[21], (DEPTH, N_GROUPS), 0.01),
        'router_expert': nrm(ks[22], (DEPTH, D_MODEL, N_EXPERTS), D_MODEL ** -0.5),
        'router_expert_bias': nrm(ks[23], (DEPTH, N_EXPERTS), 0.01),
        'w_gate': nrm(ks[24], (DEPTH, N_EXPERTS, D_MODEL, D_EXPERT), D_MODEL ** -0.5),
        'w_up': nrm(ks[25], (DEPTH, N_EXPERTS, D_MODEL, D_EXPERT), D_MODEL ** -0.5),
        'w_down': nrm(ks[26], (DEPTH, N_EXPERTS, D_EXPERT, D_MODEL), D_EXPERT ** -0.5),
        'norm_final': 1.0 + nrm(ks[27], (D_MODEL,), 0.02),
    }


def reference(x_prompt, x_sample, state_conv, cache_k, cache_v, state_C, state_n, state_m,
              norm_mix, norm_ffn, w_in, conv_w, rel_bias, gate_bias, mlstm_norm,
              w_proj_conv, w_proj_att, w_proj_mlstm, w_out,
              router_group, router_group_bias, router_expert, router_expert_bias,
              w_gate, w_up, w_down, norm_final):
    y_prompt, y_sample = x_prompt, x_sample
    p_states, s_states = [], []
    for l in range(DEPTH):
        lw = (norm_mix[l], norm_ffn[l], w_in[l], conv_w[l], rel_bias[l], gate_bias[l], mlstm_norm[l],
              w_proj_conv[l], w_proj_att[l], w_proj_mlstm[l], w_out[l],
              router_group[l], router_group_bias[l], router_expert[l], router_expert_bias[l],
              w_gate[l], w_up[l], w_down[l])
        y_prompt, ps = _layer(y_prompt, lw, None)
        y_sample, ss = _layer(y_sample, lw, (state_conv[l], cache_k[l], cache_v[l], state_C[l], state_n[l], state_m[l]))
        p_states.append(ps)
        s_states.append(ss)
    y_prompt = _rmsnorm(y_prompt, norm_final)
    y_sample = _rmsnorm(y_sample, norm_final)
    p_conv, p_k, p_v, p_C, p_n, p_m = (jnp.stack(a) for a in zip(*p_states))
    s_conv, s_k, s_v, s_C, s_n, s_m = (jnp.stack(a) for a in zip(*s_states))
    return (y_prompt, y_sample, p_conv, p_k, p_v, p_C, p_n, p_m, s_conv, s_k, s_v, s_C, s_n, s_m)
```

```python
import functools

import jax
import jax.numpy as jnp
from jax import lax
from jax.experimental import pallas as pl
from jax.experimental.pallas import tpu as pltpu

D_MODEL = 2048
SEQ = 8192
DEPTH = 2
DEC_BATCH = 8
DEC_SEQ = 32
N_SAMPLE = DEC_BATCH * DEC_SEQ
N_ROWS = SEQ + N_SAMPLE

CHUNK = 64
D_CONV = 1024
CONV_W = 3
N_HEADS_ATT = 8
HEAD_DIM_ATT = 128
D_ATT = N_HEADS_ATT * HEAD_DIM_ATT
BAND_CHUNKS = 8
ATT_REACH = BAND_CHUNKS * CHUNK
REL_CLIP = 128
N_HEADS_MLSTM = 4
HEAD_DIM_MLSTM = 256
D_MLSTM = N_HEADS_MLSTM * HEAD_DIM_MLSTM
N_BRANCH = 3
MAIN_WIDTH = 3 * D_CONV + 3 * D_ATT + 4 * D_MLSTM
GIF_WIDTH = 2 * N_HEADS_MLSTM
N_GROUPS = 4
EXPERTS_PER_GROUP = 4
N_EXPERTS = N_GROUPS * EXPERTS_PER_GROUP
D_EXPERT = 512
EPS = 1e-6

LANES = 128
ROW_TILE = 256
N_TILES = N_ROWS // ROW_TILE
N_PROMPT_TILES = SEQ // ROW_TILE
MM_ROWS = 1056
MM_COLS = 1024
ATT_Q = 256
ATT_K = ATT_Q + ATT_REACH
MLSTM_L = 256
NEG = -1e30
VMEM_LIMIT = 56 * 1024 * 1024

COL_XA, COL_GB, COL_GC, COL_Q, COL_K, COL_V, COL_QM, COL_KM, COL_VM, COL_OM = range(10)

F32 = jnp.float32
BF16 = jnp.bfloat16


def _params(sem):
    return pltpu.CompilerParams(dimension_semantics=sem, vmem_limit_bytes=VMEM_LIMIT)


def _dot(a, b):
    return jnp.dot(a, b, preferred_element_type=F32)


def _dot_nt(a, b):
    return lax.dot_general(a, b, (((1,), (1,)), ((), ())), preferred_element_type=F32)


def _dot_tn(a, b):
    return lax.dot_general(a, b, (((0,), (0,)), ((), ())), preferred_element_type=F32)


def _split_hi_lo(w):
    hi = w.astype(BF16)
    lo = (w - hi.astype(F32)).astype(BF16)
    return hi, lo


def _sigmoid(x):
    return 1.0 / (1.0 + jnp.exp(-x))


def _log_sigmoid(x):
    return jnp.minimum(x, 0.0) - jnp.log1p(jnp.exp(-jnp.abs(x)))


def _rms(x, g):
    ms = jnp.mean(x * x, axis=-1, keepdims=True)
    return x * lax.rsqrt(ms + EPS) * g


def _small_proj(h, whi_ref, wlo_ref, b_ref):
    h_hi = h.astype(BF16)
    h_lo = (h - h_hi.astype(F32)).astype(BF16)
    whi = whi_ref[...]
    return _dot(h_hi, whi) + _dot(h_lo, whi) + _dot(h_hi, wlo_ref[...]) + b_ref[...]


def _norm_first_kernel(xp_ref, xs_ref, g_ref, whi_ref, wlo_ref, b_ref, x_out, h_out, s_out):
    i = pl.program_id(0)
    x = jnp.where(i < N_PROMPT_TILES, xp_ref[...], xs_ref[...])
    x_out[...] = x
    h = _rms(x, g_ref[...])
    h_out[...] = h.astype(BF16)
    s_out[...] = _small_proj(h, whi_ref, wlo_ref, b_ref)


def _norm_add_kernel(x_ref, y_ref, g_ref, whi_ref, wlo_ref, b_ref, x_out, h_out, s_out):
    x = x_ref[...] + y_ref[...]
    x_out[...] = x
    h = _rms(x, g_ref[...])
    h_out[...] = h.astype(BF16)
    s_out[...] = _small_proj(h, whi_ref, wlo_ref, b_ref)


def _route(logits):
    lane = lax.broadcasted_iota(jnp.int32, logits.shape, 1)
    is_g = lane < N_GROUPS
    gl = jnp.where(is_g, logits, NEG)
    gmax = jnp.max(gl, axis=1, keepdims=True)
    g_sel = jnp.min(jnp.where(is_g & (gl == gmax), lane, LANES), axis=1, keepdims=True)
    p_g = 1.0 / jnp.sum(jnp.where(is_g, jnp.exp(gl - gmax), 0.0), axis=1, keepdims=True)
    e_lane = lane - N_GROUPS
    in_g = (e_lane >= 0) & (e_lane < N_EXPERTS) & ((e_lane // EXPERTS_PER_GROUP) == g_sel)
    e1 = jnp.max(jnp.where(in_g, logits, NEG), axis=1, keepdims=True)
    i1 = jnp.min(jnp.where(in_g & (logits == e1), lane, LANES), axis=1, keepdims=True)
    rest = in_g & (lane != i1)
    e2 = jnp.max(jnp.where(rest, logits, NEG), axis=1, keepdims=True)
    i2 = jnp.min(jnp.where(rest & (logits == e2), lane, LANES), axis=1, keepdims=True)
    r = jnp.exp(e2 - e1)
    w1 = p_g / (1.0 + r)
    w2 = w1 * r
    return jnp.where(lane == i1, w1, jnp.where(lane == i2, w2, 0.0))


def _norm_route_kernel(x_ref, g_ref, whi_ref, wlo_ref, b_ref, h_out, c_out):
    h = _rms(x_ref[...], g_ref[...])
    h_out[...] = h.astype(BF16)
    c_out[...] = _route(_small_proj(h, whi_ref, wlo_ref, b_ref))


def _final_norm_kernel(x_ref, y_ref, g_ref, o_ref):
    o_ref[...] = _rms(x_ref[...] + y_ref[...], g_ref[...])


def _row_spec(width):
    return pl.BlockSpec((ROW_TILE, width), lambda i: (i, 0))


def _const_spec(shape):
    return pl.BlockSpec(shape, lambda i: (0,) * len(shape))


def _norm_first(xp, xs, g, whi, wlo, b):
    return pl.pallas_call(
        _norm_first_kernel,
        grid=(N_TILES,),
        in_specs=[
            pl.BlockSpec((ROW_TILE, D_MODEL), lambda i: (jnp.minimum(i, N_PROMPT_TILES - 1), 0)),
            _const_spec((ROW_TILE, D_MODEL)),
            _const_spec((1, D_MODEL)),
            _const_spec((D_MODEL, LANES)),
            _const_spec((D_MODEL, LANES)),
            _const_spec((1, LANES)),
        ],
        out_specs=[_row_spec(D_MODEL), _row_spec(D_MODEL), _row_spec(LANES)],
        out_shape=[
            jax.ShapeDtypeStruct((N_ROWS, D_MODEL), F32),
            jax.ShapeDtypeStruct((N_ROWS, D_MODEL), BF16),
            jax.ShapeDtypeStruct((N_ROWS, LANES), F32),
        ],
        compiler_params=_params(("parallel",)),
        name="norm_first",
    )(xp, xs, g, whi, wlo, b)


def _norm_add(x, y, g, whi, wlo, b):
    return pl.pallas_call(
        _norm_add_kernel,
        grid=(N_TILES,),
        in_specs=[
            _row_spec(D_MODEL),
            _row_spec(D_MODEL),
            _const_spec((1, D_MODEL)),
            _const_spec((D_MODEL, LANES)),
            _const_spec((D_MODEL, LANES)),
            _const_spec((1, LANES)),
        ],
        out_specs=[_row_spec(D_MODEL), _row_spec(D_MODEL), _row_spec(LANES)],
        out_shape=[
            jax.ShapeDtypeStruct((N_ROWS, D_MODEL), F32),
            jax.ShapeDtypeStruct((N_ROWS, D_MODEL), BF16),
            jax.ShapeDtypeStruct((N_ROWS, LANES), F32),
        ],
        compiler_params=_params(("parallel",)),
        name="norm_add",
    )(x, y, g, whi, wlo, b)


def _norm_route(x, g, whi, wlo, b):
    return pl.pallas_call(
        _norm_route_kernel,
        grid=(N_TILES,),
        in_specs=[
            _row_spec(D_MODEL),
            _const_spec((1, D_MODEL)),
            _const_spec((D_MODEL, LANES)),
            _const_spec((D_MODEL, LANES)),
            _const_spec((1, LANES)),
        ],
        out_specs=[_row_spec(D_MODEL), _row_spec(LANES)],
        out_shape=[
            jax.ShapeDtypeStruct((N_ROWS, D_MODEL), BF16),
            jax.ShapeDtypeStruct((N_ROWS, LANES), F32),
        ],
        compiler_params=_params(("parallel",)),
        name="norm_route",
    )(x, g, whi, wlo, b)


def _final_norm(x, y, g, first_tile, n_tiles):
    return pl.pallas_call(
        _final_norm_kernel,
        grid=(n_tiles,),
        in_specs=[
            pl.BlockSpec((ROW_TILE, D_MODEL), lambda i: (i + first_tile, 0)),
            pl.BlockSpec((ROW_TILE, D_MODEL), lambda i: (i + first_tile, 0)),
            _const_spec((1, D_MODEL)),
        ],
        out_specs=pl.BlockSpec((ROW_TILE, D_MODEL), lambda i: (i, 0)),
        out_shape=jax.ShapeDtypeStruct((n_tiles * ROW_TILE, D_MODEL), F32),
        compiler_params=_params(("parallel",)),
        name="final_norm",
    )(x, y, g)


def _mm_kernel(h_ref, w_ref, o_ref, wb_ref):
    @pl.when(pl.program_id(1) == 0)
    def _():
        wb_ref[...] = w_ref[...].astype(BF16)

    o_ref[...] = _dot(h_ref[...], wb_ref[...]).astype(o_ref.dtype)


def _project(h, w, layer, n_cols):
    return pl.pallas_call(
        _mm_kernel,
        grid=(n_cols // MM_COLS, N_ROWS // MM_ROWS),
        in_specs=[
            pl.BlockSpec((MM_ROWS, D_MODEL), lambda j, i: (i, 0)),
            pl.BlockSpec((None, D_MODEL, MM_COLS), lambda j, i: (layer, 0, j)),
        ],
        out_specs=pl.BlockSpec((MM_ROWS, MM_COLS), lambda j, i: (i, j)),
        out_shape=jax.ShapeDtypeStruct((N_ROWS, n_cols), BF16),
        scratch_shapes=[pltpu.VMEM((D_MODEL, MM_COLS), BF16)],
        compiler_params=_params(("parallel", "arbitrary")),
        name="project",
    )(h, w)


def _attn_prompt_kernel(q_ref, k0_ref, k1_ref, k2_ref, v0_ref, v1_ref, v2_ref, bias_ref, o_ref):
    j = pl.program_id(0)
    key_row = lax.broadcasted_iota(jnp.int32, (ATT_Q, ATT_K), 1) + (j * ATT_Q - ATT_REACH)
    valid = key_row >= 0
    scale = HEAD_DIM_ATT ** -0.5
    for h in range(N_HEADS_ATT):
        sl = slice(h * HEAD_DIM_ATT, (h + 1) * HEAD_DIM_ATT)
        q = q_ref[:, sl]
        kk = jnp.concatenate([k0_ref[:, sl], k1_ref[:, sl], k2_ref[:, sl]], axis=0)
        vv = jnp.concatenate([v0_ref[:, sl], v1_ref[:, sl], v2_ref[:, sl]], axis=0)
        s = _dot_nt(q, kk) * scale + bias_ref[h]
        s = jnp.where(valid, s, NEG)
        mx = jnp.max(s, axis=1, keepdims=True)
        p = jnp.exp(s - mx)
        den = jnp.sum(p, axis=1, keepdims=True)
        o = _dot(p.astype(BF16), vv) / den
        o_ref[:, sl] = o.astype(o_ref.dtype)


def _attn_prompt(z, bias):
    def kv_spec(col, back):
        return pl.BlockSpec((ATT_Q, D_ATT), lambda j: (jnp.maximum(j - back, 0), col))

    return pl.pallas_call(
        _attn_prompt_kernel,
        grid=(SEQ // ATT_Q,),
        in_specs=[
            pl.BlockSpec((ATT_Q, D_ATT), lambda j: (j, COL_Q)),
            kv_spec(COL_K, 2), kv_spec(COL_K, 1), kv_spec(COL_K, 0),
            kv_spec(COL_V, 2), kv_spec(COL_V, 1), kv_spec(COL_V, 0),
            _const_spec((N_HEADS_ATT, ATT_Q, ATT_K)),
        ],
        out_specs=pl.BlockSpec((ATT_Q, D_ATT), lambda j: (j, 0)),
        out_shape=jax.ShapeDtypeStruct((SEQ, D_ATT), BF16),
        compiler_params=_params(("parallel",)),
        name="attn_prompt",
    )(z, z, z, z, z, z, z, bias)


def _attn_step_kernel(q_ref, k_ref, v_ref, ck_ref, cv_ref, bp_ref, bn_ref, o_ref):
    scale = HEAD_DIM_ATT ** -0.5
    for h in range(N_HEADS_ATT):
        sl = slice(h * HEAD_DIM_ATT, (h + 1) * HEAD_DIM_ATT)
        q = q_ref[:, sl]
        s_past = _dot_nt(q, ck_ref[:, sl].astype(BF16)) * scale + bp_ref[h]
        s_new = _dot_nt(q, k_ref[:, sl]) * scale + bn_ref[h]
        mx = jnp.maximum(jnp.max(s_past, axis=1, keepdims=True), jnp.max(s_new, axis=1, keepdims=True))
        p_past = jnp.exp(s_past - mx)
        p_new = jnp.exp(s_new - mx)
        den = jnp.sum(p_past, axis=1, keepdims=True) + jnp.sum(p_new, axis=1, keepdims=True)
        o = _dot(p_past.astype(BF16), cv_ref[:, sl].astype(BF16)) + _dot(p_new.astype(BF16), v_ref[:, sl])
        o_ref[:, sl] = (o / den).astype(o_ref.dtype)


def _attn_step(z, cache_k, cache_v, bias_past, bias_new, layer):
    first = SEQ // DEC_SEQ
    past = cache_k.shape[2]

    def z_spec(col):
        return pl.BlockSpec((DEC_SEQ, D_ATT), lambda b: (first + b, col))

    cache_spec = pl.BlockSpec((None, None, past, D_ATT), lambda b: (layer, b, 0, 0))
    return pl.pallas_call(
        _attn_step_kernel,
        grid=(DEC_BATCH,),
        in_specs=[
            z_spec(COL_Q), z_spec(COL_K), z_spec(COL_V), cache_spec, cache_spec,
            _const_spec((N_HEADS_ATT, DEC_SEQ, past)),
            _const_spec((N_HEADS_ATT, DEC_SEQ, DEC_SEQ)),
        ],
        out_specs=pl.BlockSpec((DEC_SEQ, D_ATT), lambda b: (b, 0)),
        out_shape=jax.ShapeDtypeStruct((N_SAMPLE, D_ATT), BF16),
        compiler_params=_params(("parallel",)),
        name="attn_step",
    )(z, z, z, cache_k, cache_v, bias_past, bias_new)


def _mlstm_block(q, k, v, om, gates, gnorm, c0, n0, m0):
    li_col, lf_col, li_row, lf_row = gates
    L = q.shape[0]
    kscale = HEAD_DIM_MLSTM ** -0.5
    t_idx = lax.broadcasted_iota(jnp.int32, (L, L), 0)
    s_idx = lax.broadcasted_iota(jnp.int32, (L, L), 1)
    causal = s_idx <= t_idx
    b_col = jnp.sum(jnp.where(causal, lf_row, 0.0), axis=1, keepdims=True)
    b_row = jnp.sum(jnp.where(t_idx <= s_idx, lf_col, 0.0), axis=0, keepdims=True)
    d = jnp.where(causal, b_col - b_row + li_row, NEG)
    inter = b_col + m0
    m_col = jnp.maximum(inter, jnp.max(d, axis=1, keepdims=True))
    w = jnp.exp(d - m_col)
    sc = jnp.exp(inter - m_col)
    qk = _dot_nt(q, k) * kscale * w
    num = sc * _dot_nt(q, c0.astype(BF16)) + _dot(qk.astype(BF16), v)
    qf = q.astype(F32)
    den = sc * jnp.sum(qf * n0, axis=1, keepdims=True) + jnp.sum(qk, axis=1, keepdims=True)
    hh = num / jnp.maximum(jnp.abs(den), jnp.exp(-m_col))
    mu = jnp.mean(hh, axis=1, keepdims=True)
    cen = hh - mu
    var = jnp.mean(cen * cen, axis=1, keepdims=True)
    y = _sigmoid(om.astype(F32)) * (cen * lax.rsqrt(var + EPS) * gnorm)
    m_last = m_col[L - 1:L, :]
    b_last = b_col[L - 1:L, :]
    decay = jnp.exp(b_last + m0 - m_last)
    ws = jnp.exp(b_last - b_col + li_col - m_last) * kscale
    vs = (v.astype(F32) * ws).astype(BF16)
    c1 = decay * c0 + _dot_tn(vs, k)
    n1 = decay * n0 + jnp.sum(k.astype(F32) * ws, axis=0, keepdims=True)
    return y, c1, n1, m_last


def _gate_views(gif, head):
    gt = gif.T
    li_col = gif[:, head:head + 1]
    lf_col = _log_sigmoid(gif[:, N_HEADS_MLSTM + head:N_HEADS_MLSTM + head + 1])
    li_row = gt[head:head + 1, :]
    lf_row = _log_sigmoid(gt[N_HEADS_MLSTM + head:N_HEADS_MLSTM + head + 1, :])
    return li_col, lf_col, li_row, lf_row


def _mlstm_prompt_kernel(q_ref, k_ref, v_ref, om_ref, gif_ref, gn_ref, y_ref, c_out, n_out, m_out,
                         c_scr, n_scr, m_scr):
    step = pl.program_id(0)

    @pl.when(step == 0)
    def _():
        c_scr[...] = jnp.zeros_like(c_scr)
        n_scr[...] = jnp.zeros_like(n_scr)
        m_scr[...] = jnp.zeros_like(m_scr)

    gif = gif_ref[...]
    for h in range(N_HEADS_MLSTM):
        sl = slice(h * HEAD_DIM_MLSTM, (h + 1) * HEAD_DIM_MLSTM)
        y, c1, n1, m1 = _mlstm_block(q_ref[:, sl], k_ref[:, sl], v_ref[:, sl], om_ref[:, sl],
                                     _gate_views(gif, h), gn_ref[:, sl],
                                     c_scr[h], n_scr[h], m_scr[h][:, :1])
        y_ref[:, sl] = y.astype(y_ref.dtype)
        c_scr[h] = c1
        n_scr[h] = n1
        m_scr[h] = jnp.broadcast_to(m1, (1, LANES))

    @pl.when(step == pl.num_programs(0) - 1)
    def _():
        c_out[...] = c_scr[...]
        n_out[...] = n_scr[...]
        m_out[...] = m_scr[...]


def _mlstm_prompt(z, gif, gnorm):
    def z_spec(col):
        return pl.BlockSpec((MLSTM_L, D_MLSTM), lambda c: (c, col))

    state_shapes = [
        jax.ShapeDtypeStruct((N_HEADS_MLSTM, HEAD_DIM_MLSTM, HEAD_DIM_MLSTM), F32),
        jax.ShapeDtypeStruct((N_HEADS_MLSTM, 1, HEAD_DIM_MLSTM), F32),
        jax.ShapeDtypeStruct((N_HEADS_MLSTM, 1, LANES), F32),
    ]
    return pl.pallas_call(
        _mlstm_prompt_kernel,
        grid=(SEQ // MLSTM_L,),
        in_specs=[
            z_spec(COL_QM), z_spec(COL_KM), z_spec(COL_VM), z_spec(COL_OM),
            pl.BlockSpec((MLSTM_L, LANES), lambda c: (c, 0)),
            _const_spec((1, D_MLSTM)),
        ],
        out_specs=[pl.BlockSpec((MLSTM_L, D_MLSTM), lambda c: (c, 0))]
        + [_const_spec(s.shape) for s in state_shapes],
        out_shape=[jax.ShapeDtypeStruct((SEQ, D_MLSTM), BF16)] + state_shapes,
        scratch_shapes=[pltpu.VMEM(s.shape, F32) for s in state_shapes],
        compiler_params=_params(("arbitrary",)),
        name="mlstm_prompt",
    )(z, z, z, z, gif, gnorm)


def _mlstm_step_kernel(q_ref, k_ref, v_ref, om_ref, gif_ref, gn_ref, c_ref, n_ref, m_ref,
                       y_ref, c_out, n_out, m_out):
    gif = gif_ref[...]
    for h in range(N_HEADS_MLSTM):
        sl = slice(h * HEAD_DIM_MLSTM, (h + 1) * HEAD_DIM_MLSTM)
        y, c1, n1, m1 = _mlstm_block(q_ref[:, sl], k_ref[:, sl], v_ref[:, sl], om_ref[:, sl],
                                     _gate_views(gif, h), gn_ref[:, sl],
                                     c_ref[h], n_ref[h], m_ref[h][:, :1])
        y_ref[:, sl] = y.astype(y_ref.dtype)
        c_out[h] = c1
        n_out[h] = n1
        m_out[h] = jnp.broadcast_to(m1, (1, LANES))


def _mlstm_step(z, gif, gnorm, state_c, state_n, state_m, layer):
    first = SEQ // DEC_SEQ

    def z_spec(col):
        return pl.BlockSpec((DEC_SEQ, D_MLSTM), lambda b: (first + b, col))

    def st_in(shape):
        return pl.BlockSpec((None, None) + shape, lambda b: (layer, b) + (0,) * len(shape))

    def st_out(shape):
        return pl.BlockSpec((None,) + shape, lambda b: (b,) + (0,) * len(shape))

    shapes = [(N_HEADS_MLSTM, HEAD_DIM_MLSTM, HEAD_DIM_MLSTM), (N_HEADS_MLSTM, 1, HEAD_DIM_MLSTM),
              (N_HEADS_MLSTM, 1, LANES)]
    return pl.pallas_call(
        _mlstm_step_kernel,
        grid=(DEC_BATCH,),
        in_specs=[
            z_spec(COL_QM), z_spec(COL_KM), z_spec(COL_VM), z_spec(COL_OM),
            pl.BlockSpec((DEC_SEQ, LANES), lambda b: (first + b, 0)),
            _const_spec((1, D_MLSTM)),
        ] + [st_in(s) for s in shapes],
        out_specs=[pl.BlockSpec((DEC_SEQ, D_MLSTM), lambda b: (b, 0))] + [st_out(s) for s in shapes],
        out_shape=[jax.ShapeDtypeStruct((N_SAMPLE, D_MLSTM), BF16)]
        + [jax.ShapeDtypeStruct((DEC_BATCH,) + s, F32) for s in shapes],
        compiler_params=_params(("parallel",)),
        name="mlstm_step",
    )(z, z, z, z, gif, gnorm, state_c, state_n, state_m)


HALO = 16
GROUPS_PER_TILE = ROW_TILE // DEC_SEQ


def _merge_kernel(xa_ref, gb_ref, gc_ref, xah_ref, gch_ref, s1_ref, s2_ref, cw_ref,
                  ybp_ref, ybs_ref, ycp_ref, ycs_ref, zg_ref, x_ref,
                  wpc_ref, wpa_ref, wpm_ref, wout_ref, x_out, tail_out, u_scr):
    i = pl.program_id(0)
    is_s = i >= N_PROMPT_TILES
    row = lax.broadcasted_iota(jnp.int32, (ROW_TILE, 1), 0)
    pos = jnp.where(is_s, row % DEC_SEQ, row)
    u = gc_ref[...].astype(F32) * xa_ref[...].astype(F32)
    u_halo = gch_ref[...].astype(F32) * xah_ref[...].astype(F32)
    keep = jnp.logical_and(i > 0, jnp.logical_not(is_s))
    h1 = jnp.where(keep, u_halo[HALO - 1:HALO, :], 0.0)
    h2 = jnp.where(keep, u_halo[HALO - 2:HALO - 1, :], 0.0)
    f1 = jnp.where(is_s, s1_ref[...], h1)
    f2 = jnp.where(is_s, s2_ref[...], jnp.where(row == 0, h2, h1))
    u_m1 = jnp.where(pos >= 1, pltpu.roll(u, 1, 0), f1)
    u_m2 = jnp.where(pos >= 2, pltpu.roll(u, 2, 0), f2)
    cw = cw_ref[...]
    y = cw[0:1, :] * u_m2 + cw[1:2, :] * u_m1 + cw[2:3, :] * u
    ya = gb_ref[...].astype(F32) * y

    u_scr[...] = u
    for g in range(GROUPS_PER_TILE):
        for j in range(CONV_W - 1):
            src = (g + 1) * DEC_SEQ - (CONV_W - 1) + j
            tail_out[j, g:g + 1, :] = u_scr[src:src + 1, :]

    yb = jnp.where(is_s, ybs_ref[...], ybp_ref[...])
    yc = jnp.where(is_s, ycs_ref[...], ycp_ref[...])
    g = _sigmoid(zg_ref[...].astype(F32))
    merged = (g[:, 0:D_MODEL] * _dot(ya.astype(BF16), wpc_ref[...])
              + g[:, D_MODEL:2 * D_MODEL] * _dot(yb, wpa_ref[...])
              + g[:, 2 * D_MODEL:3 * D_MODEL] * _dot(yc, wpm_ref[...]))
    x_out[...] = x_ref[...] + _dot(merged.astype(BF16), wout_ref[...])


def _merge(z, zg, s1, s2, conv_w, yb_p, yb_s, yc_p, yc_s, x, wpc, wpa, wpm, wout):
    halo_blocks = ROW_TILE // HALO

    def z_spec(col):
        return pl.BlockSpec((ROW_TILE, D_CONV), lambda i: (i, col))

    def halo_spec(col):
        return pl.BlockSpec((HALO, D_CONV), lambda i: (jnp.maximum(i * halo_blocks - 1, 0), col))

    def prompt_spec(width):
        return pl.BlockSpec((ROW_TILE, width), lambda i: (jnp.minimum(i, N_PROMPT_TILES - 1), 0))

    def weight_spec(shape):
        return pl.BlockSpec(shape, lambda i: (0, 0), pipeline_mode=pl.Buffered(1))

    return pl.pallas_call(
        _merge_kernel,
        grid=(N_TILES,),
        in_specs=[
            z_spec(COL_XA), z_spec(COL_GB), z_spec(COL_GC), halo_spec(COL_XA), halo_spec(COL_GC),
            _const_spec((ROW_TILE, D_CONV)), _const_spec((ROW_TILE, D_CONV)), _const_spec((CONV_W, D_CONV)),
            prompt_spec(D_ATT), _const_spec((ROW_TILE, D_ATT)),
            prompt_spec(D_MLSTM), _const_spec((ROW_TILE, D_MLSTM)),
            _row_spec(N_BRANCH * D_MODEL), _row_spec(D_MODEL),
            weight_spec((D_CONV, D_MODEL)), weight_spec((D_ATT, D_MODEL)), weight_spec((D_MLSTM, D_MODEL)),
            weight_spec((D_MODEL, D_MODEL)),
        ],
        out_specs=[
            _row_spec(D_MODEL),
            pl.BlockSpec((None, CONV_W - 1, GROUPS_PER_TILE, D_CONV), lambda i: (i, 0, 0, 0)),
        ],
        out_shape=[
            jax.ShapeDtypeStruct((N_ROWS, D_MODEL), F32),
            jax.ShapeDtypeStruct((N_TILES, CONV_W - 1, GROUPS_PER_TILE, D_CONV), F32),
        ],
        scratch_shapes=[pltpu.VMEM((ROW_TILE, D_CONV), F32)],
        compiler_params=_params(("parallel",)),
        name="merge",
    )(z, z, z, z, z, s1, s2, conv_w, yb_p, yb_s, yc_p, yc_s, zg, x, wpc, wpa, wpm, wout)


def _moe_kernel(h_ref, c_ref, wg_ref, wu_ref, wd_ref, y_ref):
    e = pl.program_id(1)
    h = h_ref[...]
    comb = c_ref[...]
    lane = lax.broadcasted_iota(jnp.int32, comb.shape, 1)
    ce = jnp.sum(jnp.where(lane == e + N_GROUPS, comb, 0.0), axis=1, keepdims=True)
    gate = _dot(h, wg_ref[...])
    a = gate * _sigmoid(gate) * _dot(h, wu_ref[...]) * ce
    y = _dot(a.astype(BF16), wd_ref[...])

    @pl.when(e == 0)
    def _():
        y_ref[...] = y

    @pl.when(e > 0)
    def _():
        y_ref[...] += y


def _moe(h, comb, wg, wu, wd):
    return pl.pallas_call(
        _moe_kernel,
        grid=(N_ROWS // MM_ROWS, N_EXPERTS),
        in_specs=[
            pl.BlockSpec((MM_ROWS, D_MODEL), lambda i, e: (i, 0)),
            pl.BlockSpec((MM_ROWS, LANES), lambda i, e: (i, 0)),
            pl.BlockSpec((None, D_MODEL, D_EXPERT), lambda i, e: (e, 0, 0)),
            pl.BlockSpec((None, D_MODEL, D_EXPERT), lambda i, e: (e, 0, 0)),
            pl.BlockSpec((None, D_EXPERT, D_MODEL), lambda i, e: (e, 0, 0)),
        ],
        out_specs=pl.BlockSpec((MM_ROWS, D_MODEL), lambda i, e: (i, 0)),
        out_shape=jax.ShapeDtypeStruct((N_ROWS, D_MODEL), F32),
        compiler_params=_params(("parallel", "arbitrary")),
        name="moe",
    )(h, comb, wg, wu, wd)


def _pad_lanes(w):
    return jnp.pad(w, [(0, 0)] * (w.ndim - 1) + [(0, LANES - w.shape[-1])])


def _prompt_bias(table):
    l = jnp.arange(ATT_Q)[:, None]
    m = jnp.arange(ATT_K)[None, :]
    dist = ATT_REACH + l - m
    bias = table[:, jnp.clip(dist, -REL_CLIP, REL_CLIP) + REL_CLIP]
    qc = l // CHUNK
    kc = m // CHUNK
    band = (kc >= qc) & (kc <= qc + BAND_CHUNKS)
    return jnp.where(band[None], bias, NEG)


def _step_bias(table, past):
    s = jnp.arange(DEC_SEQ)[:, None]
    key_pos = jnp.concatenate([jnp.arange(past) - past, jnp.arange(DEC_SEQ)])[None, :]
    bias = table[:, jnp.clip(s - key_pos, -REL_CLIP, REL_CLIP) + REL_CLIP]
    return bias[:, :, :past], bias[:, :, past:]


def kernel(x_prompt, x_sample, state_conv, cache_k, cache_v, state_C, state_n, state_m, norm_mix, norm_ffn, w_in, conv_w, rel_bias, gate_bias, mlstm_norm, w_proj_conv, w_proj_att, w_proj_mlstm, w_out, router_group, router_group_bias, router_expert, router_expert_bias, w_gate, w_up, w_down, norm_final):
    past = cache_k.shape[2]
    xp = x_prompt.reshape(SEQ, D_MODEL)
    xs = x_sample.reshape(N_SAMPLE, D_MODEL)
    cache_k2 = cache_k.reshape(DEPTH, DEC_BATCH, past, D_ATT)
    cache_v2 = cache_v.reshape(DEPTH, DEC_BATCH, past, D_ATT)
    state_n5 = state_n.reshape(DEPTH, DEC_BATCH, N_HEADS_MLSTM, 1, HEAD_DIM_MLSTM)
    state_m5 = jnp.broadcast_to(state_m[..., None, None], (DEPTH, DEC_BATCH, N_HEADS_MLSTM, 1, LANES))
    w_glog = w_in[:, :, MAIN_WIDTH + GIF_WIDTH:]

    x_mid = y_moe = None
    outs = {k: [] for k in ("p_conv", "p_k", "p_v", "p_C", "p_n", "p_m", "s_conv", "s_k", "s_v", "s_C", "s_n", "s_m")}
    for l in range(DEPTH):
        gif_hi, gif_lo = _split_hi_lo(_pad_lanes(w_in[l, :, MAIN_WIDTH:MAIN_WIDTH + GIF_WIDTH]))
        gif_b = _pad_lanes(gate_bias[l][None, :])
        g_mix = norm_mix[l][None, :]
        if l == 0:
            x, h, gif = _norm_first(xp, xs, g_mix, gif_hi, gif_lo, gif_b)
        else:
            x, h, gif = _norm_add(x_mid, y_moe, g_mix, gif_hi, gif_lo, gif_b)
        z = _project(h, w_in, l, MAIN_WIDTH)
        zg = _project(h, w_glog, l, N_BRANCH * D_MODEL)

        yb_p = _attn_prompt(z, _prompt_bias(rel_bias[l]))
        bias_past, bias_new = _step_bias(rel_bias[l], past)
        yb_s = _attn_step(z, cache_k2, cache_v2, bias_past, bias_new, l)

        gnorm = mlstm_norm[l][None, :]
        yc_p, p_c, p_n, p_m = _mlstm_prompt(z, gif, gnorm)
        yc_s, s_c, s_n, s_m = _mlstm_step(z, gif, gnorm, state_C, state_n5, state_m5, l)

        prev = state_conv[l]
        s1 = jnp.zeros((DEC_BATCH, DEC_SEQ, D_CONV), F32).at[:, 0].set(prev[:, 1]).reshape(N_SAMPLE, D_CONV)
        s2 = (jnp.zeros((DEC_BATCH, DEC_SEQ, D_CONV), F32).at[:, 0].set(prev[:, 0]).at[:, 1].set(prev[:, 1])
              .reshape(N_SAMPLE, D_CONV))
        x_mid, tails = _merge(z, zg, s1, s2, conv_w[l], yb_p, yb_s, yc_p, yc_s, x,
                              w_proj_conv[l].astype(BF16), w_proj_att[l].astype(BF16),
                              w_proj_mlstm[l].astype(BF16), w_out[l].astype(BF16))

        r_w = _pad_lanes(jnp.concatenate([router_group[l], router_expert[l]], axis=1))
        r_hi, r_lo = _split_hi_lo(r_w)
        r_b = _pad_lanes(jnp.concatenate([router_group_bias[l], router_expert_bias[l]])[None, :])
        h2, comb = _norm_route(x_mid, norm_ffn[l][None, :], r_hi, r_lo, r_b)
        y_moe = _moe(h2, comb, w_gate[l].astype(BF16), w_up[l].astype(BF16), w_down[l].astype(BF16))

        keep = min(ATT_REACH, SEQ)
        k_rows = z[SEQ - keep:, COL_K * D_ATT:(COL_K + 1) * D_ATT].astype(F32)
        v_rows = z[SEQ - keep:, COL_V * D_ATT:(COL_V + 1) * D_ATT].astype(F32)
        outs["p_conv"].append(tails[N_PROMPT_TILES - 1, :, GROUPS_PER_TILE - 1][None])
        outs["p_k"].append(k_rows[:keep].reshape(1, keep, N_HEADS_ATT, HEAD_DIM_ATT))
        outs["p_v"].append(v_rows[:keep].reshape(1, keep, N_HEADS_ATT, HEAD_DIM_ATT))
        outs["p_C"].append(p_c[None])
        outs["p_n"].append(p_n[:, 0][None])
        outs["p_m"].append(p_m[:, 0, 0][None])
        outs["s_conv"].append(jnp.swapaxes(tails[N_PROMPT_TILES], 0, 1))
        outs["s_k"].append(k_rows[keep:].reshape(DEC_BATCH, DEC_SEQ, N_HEADS_ATT, HEAD_DIM_ATT))
        outs["s_v"].append(v_rows[keep:].reshape(DEC_BATCH, DEC_SEQ, N_HEADS_ATT, HEAD_DIM_ATT))
        outs["s_C"].append(s_c)
        outs["s_n"].append(s_n[:, :, 0])
        outs["s_m"].append(s_m[:, :, 0, 0])

    g_fin = norm_final[None, :]
    y_prompt = _final_norm(x_mid, y_moe, g_fin, 0, N_PROMPT_TILES).reshape(x_prompt.shape)
    y_sample = _final_norm(x_mid, y_moe, g_fin, N_PROMPT_TILES, 1).reshape(x_sample.shape)
    st = {k: jnp.stack(v) for k, v in outs.items()}
    return (y_prompt, y_sample, st["p_conv"], st["p_k"], st["p_v"], st["p_C"], st["p_n"], st["p_m"],
            st["s_conv"], st["s_k"], st["s_v"], st["s_C"], st["s_n"], st["s_m"])
```

```python
import functools

import jax
import jax.numpy as jnp
from jax import lax
from jax.experimental import pallas as pl
from jax.experimental.pallas import tpu as pltpu

D_MODEL = 2048
SEQ = 8192
DEPTH = 2
DEC_BATCH = 8
DEC_SEQ = 32
N_SAMPLE = DEC_BATCH * DEC_SEQ
N_ROWS = SEQ + N_SAMPLE

CHUNK = 64
D_CONV = 1024
CONV_W = 3
N_HEADS_ATT = 8
HEAD_DIM_ATT = 128
D_ATT = N_HEADS_ATT * HEAD_DIM_ATT
BAND_CHUNKS = 8
ATT_REACH = BAND_CHUNKS * CHUNK
REL_CLIP = 128
N_HEADS_MLSTM = 4
HEAD_DIM_MLSTM = 256
D_MLSTM = N_HEADS_MLSTM * HEAD_DIM_MLSTM
N_BRANCH = 3
MAIN_WIDTH = 3 * D_CONV + 3 * D_ATT + 4 * D_MLSTM
GIF_WIDTH = 2 * N_HEADS_MLSTM
N_GROUPS = 4
EXPERTS_PER_GROUP = 4
N_EXPERTS = N_GROUPS * EXPERTS_PER_GROUP
D_EXPERT = 512
EPS = 1e-6

LANES = 128
ROW_TILE = 256
N_TILES = N_ROWS // ROW_TILE
N_PROMPT_TILES = SEQ // ROW_TILE
MM_ROWS = 1056
MM_COLS = 1024
ATT_Q = 256
ATT_K = ATT_Q + ATT_REACH
MLSTM_L = 256
NEG = -1e30
VMEM_LIMIT = 56 * 1024 * 1024

COL_XA, COL_GB, COL_GC, COL_Q, COL_K, COL_V, COL_QM, COL_KM, COL_VM, COL_OM = range(10)

F32 = jnp.float32
BF16 = jnp.bfloat16


def _params(sem):
    return pltpu.CompilerParams(dimension_semantics=sem, vmem_limit_bytes=VMEM_LIMIT)


def _dot(a, b):
    return jnp.dot(a, b, preferred_element_type=F32)


def _dot_nt(a, b):
    return lax.dot_general(a, b, (((1,), (1,)), ((), ())), preferred_element_type=F32)


def _dot_tn(a, b):
    return lax.dot_general(a, b, (((0,), (0,)), ((), ())), preferred_element_type=F32)


def _split_hi_lo(w):
    hi = w.astype(BF16)
    lo = (w - hi.astype(F32)).astype(BF16)
    return hi, lo


def _sigmoid(x):
    return 1.0 / (1.0 + jnp.exp(-x))


def _log_sigmoid(x):
    return jnp.minimum(x, 0.0) - jnp.log1p(jnp.exp(-jnp.abs(x)))


def _rms(x, g):
    ms = jnp.mean(x * x, axis=-1, keepdims=True)
    return x * lax.rsqrt(ms + EPS) * g


def _small_proj(h, whi_ref, wlo_ref, b_ref):
    h_hi = h.astype(BF16)
    h_lo = (h - h_hi.astype(F32)).astype(BF16)
    whi = whi_ref[...]
    return _dot(h_hi, whi) + _dot(h_lo, whi) + _dot(h_hi, wlo_ref[...]) + b_ref[...]


def _norm_first_kernel(xp_ref, xs_ref, g_ref, whi_ref, wlo_ref, b_ref, x_out, h_out, s_out):
    i = pl.program_id(0)
    x = jnp.where(i < N_PROMPT_TILES, xp_ref[...], xs_ref[...])
    x_out[...] = x
    h = _rms(x, g_ref[...])
    h_out[...] = h.astype(BF16)
    s_out[...] = _small_proj(h, whi_ref, wlo_ref, b_ref)


def _norm_add_kernel(x_ref, y_ref, g_ref, whi_ref, wlo_ref, b_ref, x_out, h_out, s_out):
    x = x_ref[...] + y_ref[...]
    x_out[...] = x
    h = _rms(x, g_ref[...])
    h_out[...] = h.astype(BF16)
    s_out[...] = _small_proj(h, whi_ref, wlo_ref, b_ref)


def _route(logits):
    lane = lax.broadcasted_iota(jnp.int32, logits.shape, 1)
    is_g = lane < N_GROUPS
    gl = jnp.where(is_g, logits, NEG)
    gmax = jnp.max(gl, axis=1, keepdims=True)
    g_sel = jnp.min(jnp.where(is_g & (gl == gmax), lane, LANES), axis=1, keepdims=True)
    p_g = 1.0 / jnp.sum(jnp.where(is_g, jnp.exp(gl - gmax), 0.0), axis=1, keepdims=True)
    e_lane = lane - N_GROUPS
    in_g = (e_lane >= 0) & (e_lane < N_EXPERTS) & ((e_lane // EXPERTS_PER_GROUP) == g_sel)
    e1 = jnp.max(jnp.where(in_g, logits, NEG), axis=1, keepdims=True)
    i1 = jnp.min(jnp.where(in_g & (logits == e1), lane, LANES), axis=1, keepdims=True)
    rest = in_g & (lane != i1)
    e2 = jnp.max(jnp.where(rest, logits, NEG), axis=1, keepdims=True)
    i2 = jnp.min(jnp.where(rest & (logits == e2), lane, LANES), axis=1, keepdims=True)
    r = jnp.exp(e2 - e1)
    w1 = p_g / (1.0 + r)
    w2 = w1 * r
    return jnp.where(lane == i1, w1, jnp.where(lane == i2, w2, 0.0))


def _norm_route_kernel(x_ref, g_ref, whi_ref, wlo_ref, b_ref, h_out, c_out):
    h = _rms(x_ref[...], g_ref[...])
    h_out[...] = h.astype(BF16)
    c_out[...] = _route(_small_proj(h, whi_ref, wlo_ref, b_ref))


def _final_norm_kernel(x_ref, y_ref, g_ref, o_ref):
    o_ref[...] = _rms(x_ref[...] + y_ref[...], g_ref[...])


def _row_spec(width):
    return pl.BlockSpec((ROW_TILE, width), lambda i: (i, 0))


def _const_spec(shape):
    return pl.BlockSpec(shape, lambda i: (0,) * len(shape))


def _norm_first(xp, xs, g, whi, wlo, b):
    return pl.pallas_call(
        _norm_first_kernel,
        grid=(N_TILES,),
        in_specs=[
            pl.BlockSpec((ROW_TILE, D_MODEL), lambda i: (jnp.minimum(i, N_PROMPT_TILES - 1), 0)),
            _const_spec((ROW_TILE, D_MODEL)),
            _const_spec((1, D_MODEL)),
            _const_spec((D_MODEL, LANES)),
            _const_spec((D_MODEL, LANES)),
            _const_spec((1, LANES)),
        ],
        out_specs=[_row_spec(D_MODEL), _row_spec(D_MODEL), _row_spec(LANES)],
        out_shape=[
            jax.ShapeDtypeStruct((N_ROWS, D_MODEL), F32),
            jax.ShapeDtypeStruct((N_ROWS, D_MODEL), BF16),
            jax.ShapeDtypeStruct((N_ROWS, LANES), F32),
        ],
        compiler_params=_params(("parallel",)),
        name="norm_first",
    )(xp, xs, g, whi, wlo, b)


def _norm_add(x, y, g, whi, wlo, b):
    return pl.pallas_call(
        _norm_add_kernel,
        grid=(N_TILES,),
        in_specs=[
            _row_spec(D_MODEL),
            _row_spec(D_MODEL),
            _const_spec((1, D_MODEL)),
            _const_spec((D_MODEL, LANES)),
            _const_spec((D_MODEL, LANES)),
            _const_spec((1, LANES)),
        ],
        out_specs=[_row_spec(D_MODEL), _row_spec(D_MODEL), _row_spec(LANES)],
        out_shape=[
            jax.ShapeDtypeStruct((N_ROWS, D_MODEL), F32),
            jax.ShapeDtypeStruct((N_ROWS, D_MODEL), BF16),
            jax.ShapeDtypeStruct((N_ROWS, LANES), F32),
        ],
        compiler_params=_params(("parallel",)),
        name="norm_add",
    )(x, y, g, whi, wlo, b)


def _norm_route(x, g, whi, wlo, b):
    return pl.pallas_call(
        _norm_route_kernel,
        grid=(N_TILES,),
        in_specs=[
            _row_spec(D_MODEL),
            _const_spec((1, D_MODEL)),
            _const_spec((D_MODEL, LANES)),
            _const_spec((D_MODEL, LANES)),
            _const_spec((1, LANES)),
        ],
        out_specs=[_row_spec(D_MODEL), _row_spec(LANES)],
        out_shape=[
            jax.ShapeDtypeStruct((N_ROWS, D_MODEL), BF16),
            jax.ShapeDtypeStruct((N_ROWS, LANES), F32),
        ],
        compiler_params=_params(("parallel",)),
        name="norm_route",
    )(x, g, whi, wlo, b)


def _final_norm(x, y, g, first_tile, n_tiles):
    return pl.pallas_call(
        _final_norm_kernel,
        grid=(n_tiles,),
        in_specs=[
            pl.BlockSpec((ROW_TILE, D_MODEL), lambda i: (i + first_tile, 0)),
            pl.BlockSpec((ROW_TILE, D_MODEL), lambda i: (i + first_tile, 0)),
            _const_spec((1, D_MODEL)),
        ],
        out_specs=pl.BlockSpec((ROW_TILE, D_MODEL), lambda i: (i, 0)),
        out_shape=jax.ShapeDtypeStruct((n_tiles * ROW_TILE, D_MODEL), F32),
        compiler_params=_params(("parallel",)),
        name="final_norm",
    )(x, y, g)


def _mm_kernel(h_ref, w_ref, o_ref, wb_ref):
    @pl.when(pl.program_id(1) == 0)
    def _():
        wb_ref[...] = w_ref[...].astype(BF16)

    o_ref[...] = _dot(h_ref[...], wb_ref[...]).astype(o_ref.dtype)


def _project(h, w, layer, n_cols):
    return pl.pallas_call(
        _mm_kernel,
        grid=(n_cols // MM_COLS, N_ROWS // MM_ROWS),
        in_specs=[
            pl.BlockSpec((MM_ROWS, D_MODEL), lambda j, i: (i, 0)),
            pl.BlockSpec((None, D_MODEL, MM_COLS), lambda j, i: (layer, 0, j)),
        ],
        out_specs=pl.BlockSpec((MM_ROWS, MM_COLS), lambda j, i: (i, j)),
        out_shape=jax.ShapeDtypeStruct((N_ROWS, n_cols), BF16),
        scratch_shapes=[pltpu.VMEM((D_MODEL, MM_COLS), BF16)],
        compiler_params=_params(("parallel", "arbitrary")),
        name="project",
    )(h, w)


GATE_COLS = 512
GATE_COL0 = MAIN_WIDTH // GATE_COLS


def _mm_shift_kernel(h_ref, wa_ref, wb_ref, o_ref, wbf_ref):
    @pl.when(pl.program_id(1) == 0)
    def _():
        keep = GATE_COLS - GIF_WIDTH
        lane = lax.broadcasted_iota(jnp.int32, (D_MODEL, GATE_COLS), 1)
        w = jnp.where(lane < keep, pltpu.roll(wa_ref[...], keep, 1), pltpu.roll(wb_ref[...], keep, 1))
        wbf_ref[...] = w.astype(BF16)

    o_ref[...] = _dot(h_ref[...], wbf_ref[...]).astype(o_ref.dtype)


def _project_gates(h, w, layer):
    n_cols = N_BRANCH * D_MODEL
    return pl.pallas_call(
        _mm_shift_kernel,
        grid=(n_cols // GATE_COLS, N_ROWS // MM_ROWS),
        in_specs=[
            pl.BlockSpec((MM_ROWS, D_MODEL), lambda j, i: (i, 0)),
            pl.BlockSpec((None, D_MODEL, GATE_COLS), lambda j, i: (layer, 0, GATE_COL0 + j)),
            pl.BlockSpec((None, D_MODEL, GATE_COLS), lambda j, i: (layer, 0, GATE_COL0 + j + 1)),
        ],
        out_specs=pl.BlockSpec((MM_ROWS, GATE_COLS), lambda j, i: (i, j)),
        out_shape=jax.ShapeDtypeStruct((N_ROWS, n_cols), BF16),
        scratch_shapes=[pltpu.VMEM((D_MODEL, GATE_COLS), BF16)],
        compiler_params=_params(("parallel", "arbitrary")),
        name="project_gates",
    )(h, w, w)


def _attn_prompt_kernel(q_ref, k0_ref, k1_ref, k2_ref, v0_ref, v1_ref, v2_ref, bias_ref, o_ref):
    j = pl.program_id(0)
    key_row = lax.broadcasted_iota(jnp.int32, (ATT_Q, ATT_K), 1) + (j * ATT_Q - ATT_REACH)
    valid = key_row >= 0
    scale = HEAD_DIM_ATT ** -0.5
    for h in range(N_HEADS_ATT):
        sl = slice(h * HEAD_DIM_ATT, (h + 1) * HEAD_DIM_ATT)
        q = q_ref[:, sl]
        kk = jnp.concatenate([k0_ref[:, sl], k1_ref[:, sl], k2_ref[:, sl]], axis=0)
        vv = jnp.concatenate([v0_ref[:, sl], v1_ref[:, sl], v2_ref[:, sl]], axis=0)
        s = _dot_nt(q, kk) * scale + bias_ref[h]
        s = jnp.where(valid, s, NEG)
        mx = jnp.max(s, axis=1, keepdims=True)
        p = jnp.exp(s - mx)
        den = jnp.sum(p, axis=1, keepdims=True)
        o = _dot(p.astype(BF16), vv) / den
        o_ref[:, sl] = o.astype(o_ref.dtype)


def _attn_prompt(z, bias):
    def kv_spec(col, back):
        return pl.BlockSpec((ATT_Q, D_ATT), lambda j: (jnp.maximum(j - back, 0), col))

    return pl.pallas_call(
        _attn_prompt_kernel,
        grid=(SEQ // ATT_Q,),
        in_specs=[
            pl.BlockSpec((ATT_Q, D_ATT), lambda j: (j, COL_Q)),
            kv_spec(COL_K, 2), kv_spec(COL_K, 1), kv_spec(COL_K, 0),
            kv_spec(COL_V, 2), kv_spec(COL_V, 1), kv_spec(COL_V, 0),
            _const_spec((N_HEADS_ATT, ATT_Q, ATT_K)),
        ],
        out_specs=pl.BlockSpec((ATT_Q, D_ATT), lambda j: (j, 0)),
        out_shape=jax.ShapeDtypeStruct((SEQ, D_ATT), BF16),
        compiler_params=_params(("parallel",)),
        name="attn_prompt",
    )(z, z, z, z, z, z, z, bias)


def _attn_step_kernel(q_ref, k_ref, v_ref, ck_ref, cv_ref, bp_ref, bn_ref, o_ref):
    scale = HEAD_DIM_ATT ** -0.5
    for h in range(N_HEADS_ATT):
        sl = slice(h * HEAD_DIM_ATT, (h + 1) * HEAD_DIM_ATT)
        q = q_ref[:, sl]
        s_past = _dot_nt(q, ck_ref[:, h, :].astype(BF16)) * scale + bp_ref[h]
        s_new = _dot_nt(q, k_ref[:, sl]) * scale + bn_ref[h]
        mx = jnp.maximum(jnp.max(s_past, axis=1, keepdims=True), jnp.max(s_new, axis=1, keepdims=True))
        p_past = jnp.exp(s_past - mx)
        p_new = jnp.exp(s_new - mx)
        den = jnp.sum(p_past, axis=1, keepdims=True) + jnp.sum(p_new, axis=1, keepdims=True)
        o = _dot(p_past.astype(BF16), cv_ref[:, h, :].astype(BF16)) + _dot(p_new.astype(BF16), v_ref[:, sl])
        o_ref[:, sl] = (o / den).astype(o_ref.dtype)


def _attn_step(z, cache_k, cache_v, bias_past, bias_new, layer):
    first = SEQ // DEC_SEQ
    past = cache_k.shape[2]

    def z_spec(col):
        return pl.BlockSpec((DEC_SEQ, D_ATT), lambda b: (first + b, col))

    cache_spec = pl.BlockSpec((None, None, past, N_HEADS_ATT, HEAD_DIM_ATT), lambda b: (layer, b, 0, 0, 0))
    return pl.pallas_call(
        _attn_step_kernel,
        grid=(DEC_BATCH,),
        in_specs=[
            z_spec(COL_Q), z_spec(COL_K), z_spec(COL_V), cache_spec, cache_spec,
            _const_spec((N_HEADS_ATT, DEC_SEQ, past)),
            _const_spec((N_HEADS_ATT, DEC_SEQ, DEC_SEQ)),
        ],
        out_specs=pl.BlockSpec((DEC_SEQ, D_ATT), lambda b: (b, 0)),
        out_shape=jax.ShapeDtypeStruct((N_SAMPLE, D_ATT), BF16),
        compiler_params=_params(("parallel",)),
        name="attn_step",
    )(z, z, z, cache_k, cache_v, bias_past, bias_new)


def _mlstm_block(q, k, v, om, gates, gnorm, c0, n0, m0):
    li_col, lf_col, li_row, lf_row = gates
    L = q.shape[0]
    kscale = HEAD_DIM_MLSTM ** -0.5
    t_idx = lax.broadcasted_iota(jnp.int32, (L, L), 0)
    s_idx = lax.broadcasted_iota(jnp.int32, (L, L), 1)
    causal = s_idx <= t_idx
    b_col = jnp.sum(jnp.where(causal, lf_row, 0.0), axis=1, keepdims=True)
    b_row = jnp.sum(jnp.where(t_idx <= s_idx, lf_col, 0.0), axis=0, keepdims=True)
    d = jnp.where(causal, b_col - b_row + li_row, NEG)
    inter = b_col + m0
    m_col = jnp.maximum(inter, jnp.max(d, axis=1, keepdims=True))
    w = jnp.exp(d - m_col)
    sc = jnp.exp(inter - m_col)
    qk = _dot_nt(q, k) * kscale * w
    num = sc * _dot_nt(q, c0.astype(BF16)) + _dot(qk.astype(BF16), v)
    qf = q.astype(F32)
    den = sc * jnp.sum(qf * n0, axis=1, keepdims=True) + jnp.sum(qk, axis=1, keepdims=True)
    hh = num / jnp.maximum(jnp.abs(den), jnp.exp(-m_col))
    mu = jnp.mean(hh, axis=1, keepdims=True)
    cen = hh - mu
    var = jnp.mean(cen * cen, axis=1, keepdims=True)
    y = _sigmoid(om.astype(F32)) * (cen * lax.rsqrt(var + EPS) * gnorm)
    m_last = m_col[L - 1:L, :]
    b_last = b_col[L - 1:L, :]
    decay = jnp.exp(b_last + m0 - m_last)
    ws = jnp.exp(b_last - b_col + li_col - m_last) * kscale
    vs = (v.astype(F32) * ws).astype(BF16)
    c1 = decay * c0 + _dot_tn(vs, k)
    n1 = decay * n0 + jnp.sum(k.astype(F32) * ws, axis=0, keepdims=True)
    return y, c1, n1, m_last


def _gate_views(gif, head):
    gt = gif.T
    li_col = gif[:, head:head + 1]
    lf_col = _log_sigmoid(gif[:, N_HEADS_MLSTM + head:N_HEADS_MLSTM + head + 1])
    li_row = gt[head:head + 1, :]
    lf_row = _log_sigmoid(gt[N_HEADS_MLSTM + head:N_HEADS_MLSTM + head + 1, :])
    return li_col, lf_col, li_row, lf_row


def _mlstm_prompt_kernel(q_ref, k_ref, v_ref, om_ref, gif_ref, gn_ref, y_ref, c_out, n_out, m_out,
                         c_scr, n_scr, m_scr):
    step = pl.program_id(0)

    @pl.when(step == 0)
    def _():
        c_scr[...] = jnp.zeros_like(c_scr)
        n_scr[...] = jnp.zeros_like(n_scr)
        m_scr[...] = jnp.zeros_like(m_scr)

    gif = gif_ref[...]
    for h in range(N_HEADS_MLSTM):
        sl = slice(h * HEAD_DIM_MLSTM, (h + 1) * HEAD_DIM_MLSTM)
        y, c1, n1, m1 = _mlstm_block(q_ref[:, sl], k_ref[:, sl], v_ref[:, sl], om_ref[:, sl],
                                     _gate_views(gif, h), gn_ref[:, sl],
                                     c_scr[h], n_scr[h], m_scr[h][:, :1])
        y_ref[:, sl] = y.astype(y_ref.dtype)
        c_scr[h] = c1
        n_scr[h] = n1
        m_scr[h] = jnp.broadcast_to(m1, (1, LANES))

    @pl.when(step == pl.num_programs(0) - 1)
    def _():
        c_out[...] = c_scr[...]
        n_out[...] = n_scr[...]
        m_out[...] = m_scr[...]


def _mlstm_prompt(z, gif, gnorm):
    def z_spec(col):
        return pl.BlockSpec((MLSTM_L, D_MLSTM), lambda c: (c, col))

    state_shapes = [
        jax.ShapeDtypeStruct((N_HEADS_MLSTM, HEAD_DIM_MLSTM, HEAD_DIM_MLSTM), F32),
        jax.ShapeDtypeStruct((N_HEADS_MLSTM, 1, HEAD_DIM_MLSTM), F32),
        jax.ShapeDtypeStruct((N_HEADS_MLSTM, 1, LANES), F32),
    ]
    return pl.pallas_call(
        _mlstm_prompt_kernel,
        grid=(SEQ // MLSTM_L,),
        in_specs=[
            z_spec(COL_QM), z_spec(COL_KM), z_spec(COL_VM), z_spec(COL_OM),
            pl.BlockSpec((MLSTM_L, LANES), lambda c: (c, 0)),
            _const_spec((1, D_MLSTM)),
        ],
        out_specs=[pl.BlockSpec((MLSTM_L, D_MLSTM), lambda c: (c, 0))]
        + [_const_spec(s.shape) for s in state_shapes],
        out_shape=[jax.ShapeDtypeStruct((SEQ, D_MLSTM), BF16)] + state_shapes,
        scratch_shapes=[pltpu.VMEM(s.shape, F32) for s in state_shapes],
        compiler_params=_params(("arbitrary",)),
        name="mlstm_prompt",
    )(z, z, z, z, gif, gnorm)


def _mlstm_step_kernel(q_ref, k_ref, v_ref, om_ref, gif_ref, gn_ref, c_ref, n_ref, m_ref,
                       y_ref, c_out, n_out, m_out):
    gif = gif_ref[...]
    for h in range(N_HEADS_MLSTM):
        sl = slice(h * HEAD_DIM_MLSTM, (h + 1) * HEAD_DIM_MLSTM)
        y, c1, n1, m1 = _mlstm_block(q_ref[:, sl], k_ref[:, sl], v_ref[:, sl], om_ref[:, sl],
                                     _gate_views(gif, h), gn_ref[:, sl],
                                     c_ref[h], n_ref[h], m_ref[h][:, :1])
        y_ref[:, sl] = y.astype(y_ref.dtype)
        c_out[h] = c1
        n_out[h] = n1
        m_out[h] = jnp.broadcast_to(m1, (1, LANES))


def _mlstm_step(z, gif, gnorm, state_c, state_n, state_m, layer):
    first = SEQ // DEC_SEQ

    def z_spec(col):
        return pl.BlockSpec((DEC_SEQ, D_MLSTM), lambda b: (first + b, col))

    def st_in(shape):
        return pl.BlockSpec((None, None) + shape, lambda b: (layer, b) + (0,) * len(shape))

    def st_out(shape):
        return pl.BlockSpec((None,) + shape, lambda b: (b,) + (0,) * len(shape))

    shapes = [(N_HEADS_MLSTM, HEAD_DIM_MLSTM, HEAD_DIM_MLSTM), (N_HEADS_MLSTM, 1, HEAD_DIM_MLSTM),
              (N_HEADS_MLSTM, 1, LANES)]
    return pl.pallas_call(
        _mlstm_step_kernel,
        grid=(DEC_BATCH,),
        in_specs=[
            z_spec(COL_QM), z_spec(COL_KM), z_spec(COL_VM), z_spec(COL_OM),
            pl.BlockSpec((DEC_SEQ, LANES), lambda b: (first + b, 0)),
            _const_spec((1, D_MLSTM)),
        ] + [st_in(s) for s in shapes],
        out_specs=[pl.BlockSpec((DEC_SEQ, D_MLSTM), lambda b: (b, 0))] + [st_out(s) for s in shapes],
        out_shape=[jax.ShapeDtypeStruct((N_SAMPLE, D_MLSTM), BF16)]
        + [jax.ShapeDtypeStruct((DEC_BATCH,) + s, F32) for s in shapes],
        compiler_params=_params(("parallel",)),
        name="mlstm_step",
    )(z, z, z, z, gif, gnorm, state_c, state_n, state_m)


HALO = 16
GROUPS_PER_TILE = ROW_TILE // DEC_SEQ


def _merge_kernel(xa_ref, gb_ref, gc_ref, xah_ref, gch_ref, s1_ref, s2_ref, cw_ref,
                  ybp_ref, ybs_ref, ycp_ref, ycs_ref, zg_ref, x_ref,
                  wpc_ref, wpa_ref, wpm_ref, wout_ref, x_out, tail_out, u_scr):
    i = pl.program_id(0)
    is_s = i >= N_PROMPT_TILES
    row = lax.broadcasted_iota(jnp.int32, (ROW_TILE, 1), 0)
    pos = jnp.where(is_s, row % DEC_SEQ, row)
    u = gc_ref[...].astype(F32) * xa_ref[...].astype(F32)
    u_halo = gch_ref[...].astype(F32) * xah_ref[...].astype(F32)
    keep = jnp.logical_and(i > 0, jnp.logical_not(is_s))
    h1 = jnp.where(keep, u_halo[HALO - 1:HALO, :], 0.0)
    h2 = jnp.where(keep, u_halo[HALO - 2:HALO - 1, :], 0.0)
    f1 = jnp.where(is_s, s1_ref[...], h1)
    f2 = jnp.where(is_s, s2_ref[...], jnp.where(row == 0, h2, h1))
    u_m1 = jnp.where(pos >= 1, pltpu.roll(u, 1, 0), f1)
    u_m2 = jnp.where(pos >= 2, pltpu.roll(u, 2, 0), f2)
    cw = cw_ref[...]
    y = cw[0:1, :] * u_m2 + cw[1:2, :] * u_m1 + cw[2:3, :] * u
    ya = gb_ref[...].astype(F32) * y

    u_scr[...] = u
    for g in range(GROUPS_PER_TILE):
        for j in range(CONV_W - 1):
            src = (g + 1) * DEC_SEQ - (CONV_W - 1) + j
            tail_out[j, g:g + 1, :] = u_scr[src:src + 1, :]

    yb = jnp.where(is_s, ybs_ref[...], ybp_ref[...])
    yc = jnp.where(is_s, ycs_ref[...], ycp_ref[...])
    g = _sigmoid(zg_ref[...].astype(F32))
    merged = (g[:, 0:D_MODEL] * _dot(ya.astype(BF16), wpc_ref[...])
              + g[:, D_MODEL:2 * D_MODEL] * _dot(yb, wpa_ref[...])
              + g[:, 2 * D_MODEL:3 * D_MODEL] * _dot(yc, wpm_ref[...]))
    x_out[...] = x_ref[...] + _dot(merged.astype(BF16), wout_ref[...])


def _merge(z, zg, s1, s2, conv_w, yb_p, yb_s, yc_p, yc_s, x, wpc, wpa, wpm, wout):
    halo_blocks = ROW_TILE // HALO

    def z_spec(col):
        return pl.BlockSpec((ROW_TILE, D_CONV), lambda i: (i, col))

    def halo_spec(col):
        return pl.BlockSpec((HALO, D_CONV), lambda i: (jnp.maximum(i * halo_blocks - 1, 0), col))

    def prompt_spec(width):
        return pl.BlockSpec((ROW_TILE, width), lambda i: (jnp.minimum(i, N_PROMPT_TILES - 1), 0))

    def weight_spec(shape):
        return pl.BlockSpec(shape, lambda i: (0, 0), pipeline_mode=pl.Buffered(1))

    return pl.pallas_call(
        _merge_kernel,
        grid=(N_TILES,),
        in_specs=[
            z_spec(COL_XA), z_spec(COL_GB), z_spec(COL_GC), halo_spec(COL_XA), halo_spec(COL_GC),
            _const_spec((ROW_TILE, D_CONV)), _const_spec((ROW_TILE, D_CONV)), _const_spec((CONV_W, D_CONV)),
            prompt_spec(D_ATT), _const_spec((ROW_TILE, D_ATT)),
            prompt_spec(D_MLSTM), _const_spec((ROW_TILE, D_MLSTM)),
            _row_spec(N_BRANCH * D_MODEL), _row_spec(D_MODEL),
            weight_spec((D_CONV, D_MODEL)), weight_spec((D_ATT, D_MODEL)), weight_spec((D_MLSTM, D_MODEL)),
            weight_spec((D_MODEL, D_MODEL)),
        ],
        out_specs=[
            _row_spec(D_MODEL),
            pl.BlockSpec((None, CONV_W - 1, GROUPS_PER_TILE, D_CONV), lambda i: (i, 0, 0, 0)),
        ],
        out_shape=[
            jax.ShapeDtypeStruct((N_ROWS, D_MODEL), F32),
            jax.ShapeDtypeStruct((N_TILES, CONV_W - 1, GROUPS_PER_TILE, D_CONV), F32),
        ],
        scratch_shapes=[pltpu.VMEM((ROW_TILE, D_CONV), F32)],
        compiler_params=_params(("parallel",)),
        name="merge",
    )(z, z, z, z, z, s1, s2, conv_w, yb_p, yb_s, yc_p, yc_s, zg, x, wpc, wpa, wpm, wout)


def _moe_kernel(h_ref, c_ref, wg_ref, wu_ref, wd_ref, y_ref):
    e = pl.program_id(1)
    h = h_ref[...]
    comb = c_ref[...]
    lane = lax.broadcasted_iota(jnp.int32, comb.shape, 1)
    ce = jnp.sum(jnp.where(lane == e + N_GROUPS, comb, 0.0), axis=1, keepdims=True)
    gate = _dot(h, wg_ref[...])
    a = gate * _sigmoid(gate) * _dot(h, wu_ref[...]) * ce
    y = _dot(a.astype(BF16), wd_ref[...])

    @pl.when(e == 0)
    def _():
        y_ref[...] = y

    @pl.when(e > 0)
    def _():
        y_ref[...] += y


def _moe(h, comb, wg, wu, wd):
    return pl.pallas_call(
        _moe_kernel,
        grid=(N_ROWS // MM_ROWS, N_EXPERTS),
        in_specs=[
            pl.BlockSpec((MM_ROWS, D_MODEL), lambda i, e: (i, 0)),
            pl.BlockSpec((MM_ROWS, LANES), lambda i, e: (i, 0)),
            pl.BlockSpec((None, D_MODEL, D_EXPERT), lambda i, e: (e, 0, 0)),
            pl.BlockSpec((None, D_MODEL, D_EXPERT), lambda i, e: (e, 0, 0)),
            pl.BlockSpec((None, D_EXPERT, D_MODEL), lambda i, e: (e, 0, 0)),
        ],
        out_specs=pl.BlockSpec((MM_ROWS, D_MODEL), lambda i, e: (i, 0)),
        out_shape=jax.ShapeDtypeStruct((N_ROWS, D_MODEL), F32),
        compiler_params=_params(("parallel", "arbitrary")),
        name="moe",
    )(h, comb, wg, wu, wd)


def _pad_lanes(w):
    return jnp.pad(w, [(0, 0)] * (w.ndim - 1) + [(0, LANES - w.shape[-1])])


def _toeplitz_bias(table, n_q, n_k, reach):
    period = 1
    while period < n_q + n_k:
        period *= 2
    j = jnp.arange(period)
    d = jnp.where(j < n_k, j, j - period)
    u = table[:, jnp.clip(reach - d, -REL_CLIP, REL_CLIP) + REL_CLIP]
    flat = jnp.tile(u, (1, n_q))[:, :n_q * (period - 1)]
    return flat.reshape(table.shape[0], n_q, period - 1)[:, :, :n_k]


def _prompt_bias(table):
    bias = _toeplitz_bias(table, ATT_Q, ATT_K, ATT_REACH)
    qc = jnp.arange(ATT_Q)[:, None] // CHUNK
    kc = jnp.arange(ATT_K)[None, :] // CHUNK
    band = (kc >= qc) & (kc <= qc + BAND_CHUNKS)
    return jnp.where(band[None], bias, NEG)


def _step_bias(table, past):
    bias = _toeplitz_bias(table, DEC_SEQ, past + DEC_SEQ, past)
    return bias[:, :, :past], bias[:, :, past:]


def kernel(x_prompt, x_sample, state_conv, cache_k, cache_v, state_C, state_n, state_m, norm_mix, norm_ffn, w_in, conv_w, rel_bias, gate_bias, mlstm_norm, w_proj_conv, w_proj_att, w_proj_mlstm, w_out, router_group, router_group_bias, router_expert, router_expert_bias, w_gate, w_up, w_down, norm_final):
    past = cache_k.shape[2]
    xp = x_prompt.reshape(SEQ, D_MODEL)
    xs = x_sample.reshape(N_SAMPLE, D_MODEL)
    state_n5 = state_n.reshape(DEPTH, DEC_BATCH, N_HEADS_MLSTM, 1, HEAD_DIM_MLSTM)
    state_m5 = jnp.broadcast_to(state_m[..., None, None], (DEPTH, DEC_BATCH, N_HEADS_MLSTM, 1, LANES))

    x_mid = y_moe = None
    outs = {k: [] for k in ("p_conv", "p_k", "p_v", "p_C", "p_n", "p_m", "s_conv", "s_k", "s_v", "s_C", "s_n", "s_m")}
    for l in range(DEPTH):
        gif_hi, gif_lo = _split_hi_lo(_pad_lanes(w_in[l, :, MAIN_WIDTH:MAIN_WIDTH + GIF_WIDTH]))
        gif_b = _pad_lanes(gate_bias[l][None, :])
        g_mix = norm_mix[l][None, :]
        if l == 0:
            x, h, gif = _norm_first(xp, xs, g_mix, gif_hi, gif_lo, gif_b)
        else:
            x, h, gif = _norm_add(x_mid, y_moe, g_mix, gif_hi, gif_lo, gif_b)
        z = _project(h, w_in, l, MAIN_WIDTH)
        zg = _project_gates(h, w_in, l)

        yb_p = _attn_prompt(z, _prompt_bias(rel_bias[l]))
        bias_past, bias_new = _step_bias(rel_bias[l], past)
        yb_s = _attn_step(z, cache_k, cache_v, bias_past, bias_new, l)

        gnorm = mlstm_norm[l][None, :]
        yc_p, p_c, p_n, p_m = _mlstm_prompt(z, gif, gnorm)
        yc_s, s_c, s_n, s_m = _mlstm_step(z, gif, gnorm, state_C, state_n5, state_m5, l)

        prev = state_conv[l]
        s1 = jnp.zeros((DEC_BATCH, DEC_SEQ, D_CONV), F32).at[:, 0].set(prev[:, 1]).reshape(N_SAMPLE, D_CONV)
        s2 = (jnp.zeros((DEC_BATCH, DEC_SEQ, D_CONV), F32).at[:, 0].set(prev[:, 0]).at[:, 1].set(prev[:, 1])
              .reshape(N_SAMPLE, D_CONV))
        x_mid, tails = _merge(z, zg, s1, s2, conv_w[l], yb_p, yb_s, yc_p, yc_s, x,
                              w_proj_conv[l].astype(BF16), w_proj_att[l].astype(BF16),
                              w_proj_mlstm[l].astype(BF16), w_out[l].astype(BF16))

        r_w = _pad_lanes(jnp.concatenate([router_group[l], router_expert[l]], axis=1))
        r_hi, r_lo = _split_hi_lo(r_w)
        r_b = _pad_lanes(jnp.concatenate([router_group_bias[l], router_expert_bias[l]])[None, :])
        h2, comb = _norm_route(x_mid, norm_ffn[l][None, :], r_hi, r_lo, r_b)
        y_moe = _moe(h2, comb, w_gate[l].astype(BF16), w_up[l].astype(BF16), w_down[l].astype(BF16))

        keep = min(ATT_REACH, SEQ)
        k_rows = z[SEQ - keep:, COL_K * D_ATT:(COL_K + 1) * D_ATT].astype(F32)
        v_rows = z[SEQ - keep:, COL_V * D_ATT:(COL_V + 1) * D_ATT].astype(F32)
        outs["p_conv"].append(tails[N_PROMPT_TILES - 1, :, GROUPS_PER_TILE - 1][None])
        outs["p_k"].append(k_rows[:keep].reshape(1, keep, N_HEADS_ATT, HEAD_DIM_ATT))
        outs["p_v"].append(v_rows[:keep].reshape(1, keep, N_HEADS_ATT, HEAD_DIM_ATT))
        outs["p_C"].append(p_c[None])
        outs["p_n"].append(p_n[:, 0][None])
        outs["p_m"].append(p_m[:, 0, 0][None])
        outs["s_conv"].append(jnp.swapaxes(tails[N_PROMPT_TILES], 0, 1))
        outs["s_k"].append(k_rows[keep:].reshape(DEC_BATCH, DEC_SEQ, N_HEADS_ATT, HEAD_DIM_ATT))
        outs["s_v"].append(v_rows[keep:].reshape(DEC_BATCH, DEC_SEQ, N_HEADS_ATT, HEAD_DIM_ATT))
        outs["s_C"].append(s_c)
        outs["s_n"].append(s_n[:, :, 0])
        outs["s_m"].append(s_m[:, :, 0, 0])

    g_fin = norm_final[None, :]
    y_prompt = _final_norm(x_mid, y_moe, g_fin, 0, N_PROMPT_TILES).reshape(x_prompt.shape)
    y_sample = _final_norm(x_mid, y_moe, g_fin, N_PROMPT_TILES, 1).reshape(x_sample.shape)
    st = {k: jnp.stack(v) for k, v in outs.items()}
    return (y_prompt, y_sample, st["p_conv"], st["p_k"], st["p_v"], st["p_C"], st["p_n"], st["p_m"],
            st["s_conv"], st["s_k"], st["s_v"], st["s_C"], st["s_n"], st["s_m"])
```

```python
import functools

import jax
import jax.numpy as jnp
from jax import lax
from jax.experimental import pallas as pl
from jax.experimental.pallas import tpu as pltpu

D_MODEL = 2048
SEQ = 8192
DEPTH = 2
DEC_BATCH = 8
DEC_SEQ = 32
N_SAMPLE = DEC_BATCH * DEC_SEQ
N_ROWS = SEQ + N_SAMPLE

CHUNK = 64
D_CONV = 1024
CONV_W = 3
N_HEADS_ATT = 8
HEAD_DIM_ATT = 128
D_ATT = N_HEADS_ATT * HEAD_DIM_ATT
BAND_CHUNKS = 8
ATT_REACH = BAND_CHUNKS * CHUNK
REL_CLIP = 128
N_HEADS_MLSTM = 4
HEAD_DIM_MLSTM = 256
D_MLSTM = N_HEADS_MLSTM * HEAD_DIM_MLSTM
N_BRANCH = 3
MAIN_WIDTH = 3 * D_CONV + 3 * D_ATT + 4 * D_MLSTM
GIF_WIDTH = 2 * N_HEADS_MLSTM
N_GROUPS = 4
EXPERTS_PER_GROUP = 4
N_EXPERTS = N_GROUPS * EXPERTS_PER_GROUP
D_EXPERT = 512
EPS = 1e-6

LANES = 128
ROW_TILE = 256
N_TILES = N_ROWS // ROW_TILE
N_PROMPT_TILES = SEQ // ROW_TILE
MM_ROWS = 1056
MM_COLS = 1024
ATT_Q = 256
ATT_K = ATT_Q + ATT_REACH
MLSTM_L = 256
NEG = -1e30
VMEM_LIMIT = 56 * 1024 * 1024

GRAN = 16
XS_WIDTH = D_MODEL + LANES
MAX_RUN_PAD = N_EXPERTS * (GRAN - 1)
TILE_GRANS = (2 * ROW_TILE + MAX_RUN_PAD) // GRAN + 1
LOCAL_ROWS = TILE_GRANS * GRAN
CHUNK_GRANS = 16
CHUNK_ROWS = CHUNK_GRANS * GRAN
MAX_CHUNKS = (N_TILES * (TILE_GRANS - 1)) // CHUNK_GRANS + N_EXPERTS

COL_XA, COL_GB, COL_GC, COL_Q, COL_K, COL_V, COL_QM, COL_KM, COL_VM, COL_OM = range(10)

F32 = jnp.float32
BF16 = jnp.bfloat16


def _params(sem):
    return pltpu.CompilerParams(dimension_semantics=sem, vmem_limit_bytes=VMEM_LIMIT)


def _dot(a, b):
    return jnp.dot(a, b, preferred_element_type=F32)


def _dot_nt(a, b):
    return lax.dot_general(a, b, (((1,), (1,)), ((), ())), preferred_element_type=F32)


def _dot_tn(a, b):
    return lax.dot_general(a, b, (((0,), (0,)), ((), ())), preferred_element_type=F32)


def _split_hi_lo(w):
    hi = w.astype(BF16)
    lo = (w - hi.astype(F32)).astype(BF16)
    return hi, lo


def _sigmoid(x):
    return 1.0 / (1.0 + jnp.exp(-x))


def _log_sigmoid(x):
    return jnp.minimum(x, 0.0) - jnp.log1p(jnp.exp(-jnp.abs(x)))


def _rms(x, g):
    ms = jnp.mean(x * x, axis=-1, keepdims=True)
    return x * lax.rsqrt(ms + EPS) * g


def _small_proj(h, whi_ref, wlo_ref, b_ref):
    h_hi = h.astype(BF16)
    h_lo = (h - h_hi.astype(F32)).astype(BF16)
    whi = whi_ref[...]
    return _dot(h_hi, whi) + _dot(h_lo, whi) + _dot(h_hi, wlo_ref[...]) + b_ref[...]


def _norm_first_kernel(xp_ref, xs_ref, g_ref, whi_ref, wlo_ref, b_ref, x_out, h_out, s_out):
    i = pl.program_id(0)
    x = jnp.where(i < N_PROMPT_TILES, xp_ref[...], xs_ref[...])
    x_out[...] = x
    h = _rms(x, g_ref[...])
    h_out[...] = h.astype(BF16)
    s_out[...] = _small_proj(h, whi_ref, wlo_ref, b_ref)


def _route(logits):
    lane = lax.broadcasted_iota(jnp.int32, logits.shape, 1)
    is_g = lane < N_GROUPS
    gl = jnp.where(is_g, logits, NEG)
    gmax = jnp.max(gl, axis=1, keepdims=True)
    g_sel = jnp.min(jnp.where(is_g & (gl == gmax), lane, LANES), axis=1, keepdims=True)
    p_g = 1.0 / jnp.sum(jnp.where(is_g, jnp.exp(gl - gmax), 0.0), axis=1, keepdims=True)
    e_lane = lane - N_GROUPS
    in_g = (e_lane >= 0) & (e_lane < N_EXPERTS) & ((e_lane // EXPERTS_PER_GROUP) == g_sel)
    e1 = jnp.max(jnp.where(in_g, logits, NEG), axis=1, keepdims=True)
    i1 = jnp.min(jnp.where(in_g & (logits == e1), lane, LANES), axis=1, keepdims=True)
    rest = in_g & (lane != i1)
    e2 = jnp.max(jnp.where(rest, logits, NEG), axis=1, keepdims=True)
    i2 = jnp.min(jnp.where(rest & (logits == e2), lane, LANES), axis=1, keepdims=True)
    r = jnp.exp(e2 - e1)
    w1 = p_g / (1.0 + r)
    w2 = w1 * r
    return i1, i2, w1, w2


def _bf16_pieces(w):
    a = w.astype(BF16).astype(F32)
    b = (w - a).astype(BF16).astype(F32)
    return a, b, w - a - b


def _route_dispatch_kernel(x_ref, g_ref, whi_ref, wlo_ref, b_ref, xs_out, lp_out, cnt_out):
    h = _rms(x_ref[...], g_ref[...])
    i1, i2, w1, w2 = _route(_small_proj(h, whi_ref, wlo_ref, b_ref))
    lane = lax.broadcasted_iota(jnp.int32, (ROW_TILE, LANES), 1)
    o1 = lane == i1
    o2 = lane == i2
    onehot = jnp.where(o1 | o2, 1.0, 0.0)
    t_idx = lax.broadcasted_iota(jnp.int32, (ROW_TILE, ROW_TILE), 0)
    s_idx = lax.broadcasted_iota(jnp.int32, (ROW_TILE, ROW_TILE), 1)
    earlier = jnp.where(s_idx < t_idx, 1.0, 0.0).astype(BF16)
    rank = _dot(earlier, onehot.astype(BF16))
    gcnt = jnp.floor((jnp.sum(onehot, axis=0, keepdims=True) + (GRAN - 1)) * (1.0 / GRAN))
    a_idx = lax.broadcasted_iota(jnp.int32, (LANES, LANES), 0)
    b_idx = lax.broadcasted_iota(jnp.int32, (LANES, LANES), 1)
    before = jnp.where(a_idx < b_idx, 1.0, 0.0).astype(BF16)
    gcnt8 = jnp.broadcast_to(gcnt, (8, LANES))
    run_start = _dot(gcnt8.astype(BF16), before)[0:1, :] * GRAN
    pos = run_start + rank
    lpos1 = jnp.sum(jnp.where(o1, pos, 0.0), axis=1, keepdims=True)
    lpos2 = jnp.sum(jnp.where(o2, pos, 0.0), axis=1, keepdims=True)
    lp = jnp.where(lane == 0, lpos1, jnp.where(lane == 1, lpos2, -1.0))
    lp_out[...] = lp
    cnt_out[...] = gcnt8

    meta = jnp.zeros((ROW_TILE, LANES), F32)
    fields = ((i1 - N_GROUPS).astype(F32),) + _bf16_pieces(w1) + ((i2 - N_GROUPS).astype(F32),) + _bf16_pieces(w2)
    for k, val in enumerate(fields):
        meta = jnp.where(lane == k, val, meta)
    rows = jnp.concatenate([h.astype(BF16), meta.astype(BF16)], axis=1)
    lp_t = lp.T
    p_idx = lax.broadcasted_iota(jnp.int32, (LOCAL_ROWS, ROW_TILE), 0).astype(F32)
    perm = jnp.where((p_idx == lp_t[0:1, :]) | (p_idx == lp_t[1:2, :]), 1.0, 0.0).astype(BF16)
    xs_out[...] = _dot(perm, rows).astype(BF16)


def _row_spec(width):
    return pl.BlockSpec((ROW_TILE, width), lambda i: (i, 0))


def _const_spec(shape):
    return pl.BlockSpec(shape, lambda i: (0,) * len(shape))


def _norm_first(xp, xs, g, whi, wlo, b):
    return pl.pallas_call(
        _norm_first_kernel,
        grid=(N_TILES,),
        in_specs=[
            pl.BlockSpec((ROW_TILE, D_MODEL), lambda i: (jnp.minimum(i, N_PROMPT_TILES - 1), 0)),
            _const_spec((ROW_TILE, D_MODEL)),
            _const_spec((1, D_MODEL)),
            _const_spec((D_MODEL, LANES)),
            _const_spec((D_MODEL, LANES)),
            _const_spec((1, LANES)),
        ],
        out_specs=[_row_spec(D_MODEL), _row_spec(D_MODEL), _row_spec(LANES)],
        out_shape=[
            jax.ShapeDtypeStruct((N_ROWS, D_MODEL), F32),
            jax.ShapeDtypeStruct((N_ROWS, D_MODEL), BF16),
            jax.ShapeDtypeStruct((N_ROWS, LANES), F32),
        ],
        compiler_params=_params(("parallel",)),
        name="norm_first",
    )(xp, xs, g, whi, wlo, b)


def _route_dispatch(x, g, whi, wlo, b):
    return pl.pallas_call(
        _route_dispatch_kernel,
        grid=(N_TILES,),
        in_specs=[
            _row_spec(D_MODEL),
            _const_spec((1, D_MODEL)),
            _const_spec((D_MODEL, LANES)),
            _const_spec((D_MODEL, LANES)),
            _const_spec((1, LANES)),
        ],
        out_specs=[
            pl.BlockSpec((LOCAL_ROWS, XS_WIDTH), lambda i: (i, 0)),
            _row_spec(LANES),
            pl.BlockSpec((None, 8, LANES), lambda i: (i, 0, 0)),
        ],
        out_shape=[
            jax.ShapeDtypeStruct((N_TILES * LOCAL_ROWS, XS_WIDTH), BF16),
            jax.ShapeDtypeStruct((N_ROWS, LANES), F32),
            jax.ShapeDtypeStruct((N_TILES, 8, LANES), F32),
        ],
        compiler_params=_params(("parallel",)),
        name="route_dispatch",
    )(x, g, whi, wlo, b)


def _mm_kernel(h_ref, wt_ref, o_ref, wb_ref):
    @pl.when(pl.program_id(1) == 0)
    def _():
        wb_ref[...] = wt_ref[0].T.astype(BF16)

    o_ref[...] = _dot(h_ref[...], wb_ref[...]).astype(o_ref.dtype)


def _project(h, wt, layer, row0, n_cols):
    return pl.pallas_call(
        _mm_kernel,
        grid=(n_cols // MM_COLS, N_ROWS // MM_ROWS),
        in_specs=[
            pl.BlockSpec((MM_ROWS, D_MODEL), lambda j, i: (i, 0)),
            pl.BlockSpec((pl.Element(1), pl.Element(MM_COLS), pl.Element(D_MODEL)),
                         lambda j, i: (layer, pl.multiple_of(row0 + j * MM_COLS, 8), 0)),
        ],
        out_specs=pl.BlockSpec((MM_ROWS, MM_COLS), lambda j, i: (i, j)),
        out_shape=jax.ShapeDtypeStruct((N_ROWS, n_cols), BF16),
        scratch_shapes=[pltpu.VMEM((D_MODEL, MM_COLS), BF16)],
        compiler_params=_params(("parallel", "arbitrary")),
        name="project",
    )(h, wt)


def _attn_prompt_kernel(q_ref, k0_ref, k1_ref, k2_ref, v0_ref, v1_ref, v2_ref, bias_ref, o_ref):
    j = pl.program_id(0)
    key_row = lax.broadcasted_iota(jnp.int32, (ATT_Q, ATT_K), 1) + (j * ATT_Q - ATT_REACH)
    valid = key_row >= 0
    scale = HEAD_DIM_ATT ** -0.5
    for h in range(N_HEADS_ATT):
        sl = slice(h * HEAD_DIM_ATT, (h + 1) * HEAD_DIM_ATT)
        q = q_ref[:, sl]
        kk = jnp.concatenate([k0_ref[:, sl], k1_ref[:, sl], k2_ref[:, sl]], axis=0)
        vv = jnp.concatenate([v0_ref[:, sl], v1_ref[:, sl], v2_ref[:, sl]], axis=0)
        s = _dot_nt(q, kk) * scale + bias_ref[h]
        s = jnp.where(valid, s, NEG)
        mx = jnp.max(s, axis=1, keepdims=True)
        p = jnp.exp(s - mx)
        den = jnp.sum(p, axis=1, keepdims=True)
        o = _dot(p.astype(BF16), vv) / den
        o_ref[:, sl] = o.astype(o_ref.dtype)


def _attn_prompt(z, bias):
    def kv_spec(col, back):
        return pl.BlockSpec((ATT_Q, D_ATT), lambda j: (jnp.maximum(j - back, 0), col))

    return pl.pallas_call(
        _attn_prompt_kernel,
        grid=(SEQ // ATT_Q,),
        in_specs=[
            pl.BlockSpec((ATT_Q, D_ATT), lambda j: (j, COL_Q)),
            kv_spec(COL_K, 2), kv_spec(COL_K, 1), kv_spec(COL_K, 0),
            kv_spec(COL_V, 2), kv_spec(COL_V, 1), kv_spec(COL_V, 0),
            _const_spec((N_HEADS_ATT, ATT_Q, ATT_K)),
        ],
        out_specs=pl.BlockSpec((ATT_Q, D_ATT), lambda j: (j, 0)),
        out_shape=jax.ShapeDtypeStruct((SEQ, D_ATT), BF16),
        compiler_params=_params(("parallel",)),
        name="attn_prompt",
    )(z, z, z, z, z, z, z, bias)


def _attn_step_kernel(q_ref, k_ref, v_ref, ck_ref, cv_ref, bp_ref, bn_ref, o_ref):
    scale = HEAD_DIM_ATT ** -0.5
    for h in range(N_HEADS_ATT):
        sl = slice(h * HEAD_DIM_ATT, (h + 1) * HEAD_DIM_ATT)
        q = q_ref[:, sl]
        s_past = _dot_nt(q, ck_ref[:, h, :].astype(BF16)) * scale + bp_ref[h]
        s_new = _dot_nt(q, k_ref[:, sl]) * scale + bn_ref[h]
        mx = jnp.maximum(jnp.max(s_past, axis=1, keepdims=True), jnp.max(s_new, axis=1, keepdims=True))
        p_past = jnp.exp(s_past - mx)
        p_new = jnp.exp(s_new - mx)
        den = jnp.sum(p_past, axis=1, keepdims=True) + jnp.sum(p_new, axis=1, keepdims=True)
        o = _dot(p_past.astype(BF16), cv_ref[:, h, :].astype(BF16)) + _dot(p_new.astype(BF16), v_ref[:, sl])
        o_ref[:, sl] = (o / den).astype(o_ref.dtype)


def _attn_step(z, cache_k, cache_v, bias_past, bias_new, layer):
    first = SEQ // DEC_SEQ
    past = cache_k.shape[2]

    def z_spec(col):
        return pl.BlockSpec((DEC_SEQ, D_ATT), lambda b: (first + b, col))

    cache_spec = pl.BlockSpec((None, None, past, N_HEADS_ATT, HEAD_DIM_ATT), lambda b: (layer, b, 0, 0, 0))
    return pl.pallas_call(
        _attn_step_kernel,
        grid=(DEC_BATCH,),
        in_specs=[
            z_spec(COL_Q), z_spec(COL_K), z_spec(COL_V), cache_spec, cache_spec,
            _const_spec((N_HEADS_ATT, DEC_SEQ, past)),
            _const_spec((N_HEADS_ATT, DEC_SEQ, DEC_SEQ)),
        ],
        out_specs=pl.BlockSpec((DEC_SEQ, D_ATT), lambda b: (b, 0)),
        out_shape=jax.ShapeDtypeStruct((N_SAMPLE, D_ATT), BF16),
        compiler_params=_params(("parallel",)),
        name="attn_step",
    )(z, z, z, cache_k, cache_v, bias_past, bias_new)


def _mlstm_block(q, k, v, om, gates, gnorm, c0, n0, m0):
    li_col, lf_col, li_row, lf_row = gates
    L = q.shape[0]
    kscale = HEAD_DIM_MLSTM ** -0.5
    t_idx = lax.broadcasted_iota(jnp.int32, (L, L), 0)
    s_idx = lax.broadcasted_iota(jnp.int32, (L, L), 1)
    causal = s_idx <= t_idx
    b_col = jnp.sum(jnp.where(causal, lf_row, 0.0), axis=1, keepdims=True)
    b_row = jnp.sum(jnp.where(t_idx <= s_idx, lf_col, 0.0), axis=0, keepdims=True)
    d = jnp.where(causal, b_col - b_row + li_row, NEG)
    inter = b_col + m0
    m_col = jnp.maximum(inter, jnp.max(d, axis=1, keepdims=True))
    w = jnp.exp(d - m_col)
    sc = jnp.exp(inter - m_col)
    qk = _dot_nt(q, k) * kscale * w
    num = sc * _dot_nt(q, c0.astype(BF16)) + _dot(qk.astype(BF16), v)
    qf = q.astype(F32)
    den = sc * jnp.sum(qf * n0, axis=1, keepdims=True) + jnp.sum(qk, axis=1, keepdims=True)
    hh = num / jnp.maximum(jnp.abs(den), jnp.exp(-m_col))
    mu = jnp.mean(hh, axis=1, keepdims=True)
    cen = hh - mu
    var = jnp.mean(cen * cen, axis=1, keepdims=True)
    y = _sigmoid(om.astype(F32)) * (cen * lax.rsqrt(var + EPS) * gnorm)
    m_last = m_col[L - 1:L, :]
    b_last = b_col[L - 1:L, :]
    decay = jnp.exp(b_last + m0 - m_last)
    ws = jnp.exp(b_last - b_col + li_col - m_last) * kscale
    vs = (v.astype(F32) * ws).astype(BF16)
    c1 = decay * c0 + _dot_tn(vs, k)
    n1 = decay * n0 + jnp.sum(k.astype(F32) * ws, axis=0, keepdims=True)
    return y, c1, n1, m_last


def _gate_views(gif, head):
    gt = gif.T
    li_col = gif[:, head:head + 1]
    lf_col = _log_sigmoid(gif[:, N_HEADS_MLSTM + head:N_HEADS_MLSTM + head + 1])
    li_row = gt[head:head + 1, :]
    lf_row = _log_sigmoid(gt[N_HEADS_MLSTM + head:N_HEADS_MLSTM + head + 1, :])
    return li_col, lf_col, li_row, lf_row


def _mlstm_prompt_kernel(q_ref, k_ref, v_ref, om_ref, gif_ref, gn_ref, y_ref, c_out, n_out, m_out,
                         c_scr, n_scr, m_scr):
    step = pl.program_id(0)

    @pl.when(step == 0)
    def _():
        c_scr[...] = jnp.zeros_like(c_scr)
        n_scr[...] = jnp.zeros_like(n_scr)
        m_scr[...] = jnp.zeros_like(m_scr)

    gif = gif_ref[...]
    for h in range(N_HEADS_MLSTM):
        sl = slice(h * HEAD_DIM_MLSTM, (h + 1) * HEAD_DIM_MLSTM)
        y, c1, n1, m1 = _mlstm_block(q_ref[:, sl], k_ref[:, sl], v_ref[:, sl], om_ref[:, sl],
                                     _gate_views(gif, h), gn_ref[:, sl],
                                     c_scr[h], n_scr[h], m_scr[h][:, :1])
        y_ref[:, sl] = y.astype(y_ref.dtype)
        c_scr[h] = c1
        n_scr[h] = n1
        m_scr[h] = jnp.broadcast_to(m1, (1, LANES))

    @pl.when(step == pl.num_programs(0) - 1)
    def _():
        c_out[...] = c_scr[...]
        n_out[...] = n_scr[...]
        m_out[...] = m_scr[...]


def _mlstm_prompt(z, gif, gnorm):
    def z_spec(col):
        return pl.BlockSpec((MLSTM_L, D_MLSTM), lambda c: (c, col))

    state_shapes = [
        jax.ShapeDtypeStruct((N_HEADS_MLSTM, HEAD_DIM_MLSTM, HEAD_DIM_MLSTM), F32),
        jax.ShapeDtypeStruct((N_HEADS_MLSTM, 1, HEAD_DIM_MLSTM), F32),
        jax.ShapeDtypeStruct((N_HEADS_MLSTM, 1, LANES), F32),
    ]
    return pl.pallas_call(
        _mlstm_prompt_kernel,
        grid=(SEQ // MLSTM_L,),
        in_specs=[
            z_spec(COL_QM), z_spec(COL_KM), z_spec(COL_VM), z_spec(COL_OM),
            pl.BlockSpec((MLSTM_L, LANES), lambda c: (c, 0)),
            _const_spec((1, D_MLSTM)),
        ],
        out_specs=[pl.BlockSpec((MLSTM_L, D_MLSTM), lambda c: (c, 0))]
        + [_const_spec(s.shape) for s in state_shapes],
        out_shape=[jax.ShapeDtypeStruct((SEQ, D_MLSTM), BF16)] + state_shapes,
        scratch_shapes=[pltpu.VMEM(s.shape, F32) for s in state_shapes],
        compiler_params=_params(("arbitrary",)),
        name="mlstm_prompt",
    )(z, z, z, z, gif, gnorm)


def _mlstm_step_kernel(q_ref, k_ref, v_ref, om_ref, gif_ref, gn_ref, c_ref, n_ref, m_ref,
                       y_ref, c_out, n_out, m_out):
    gif = gif_ref[...]
    for h in range(N_HEADS_MLSTM):
        sl = slice(h * HEAD_DIM_MLSTM, (h + 1) * HEAD_DIM_MLSTM)
        y, c1, n1, m1 = _mlstm_block(q_ref[:, sl], k_ref[:, sl], v_ref[:, sl], om_ref[:, sl],
                                     _gate_views(gif, h), gn_ref[:, sl],
                                     c_ref[h], n_ref[h], m_ref[h][:, :1])
        y_ref[:, sl] = y.astype(y_ref.dtype)
        c_out[h] = c1
        n_out[h] = n1
        m_out[h] = jnp.broadcast_to(m1, (1, LANES))


def _mlstm_step(z, gif, gnorm, state_c, state_n, state_m, layer):
    first = SEQ // DEC_SEQ

    def z_spec(col):
        return pl.BlockSpec((DEC_SEQ, D_MLSTM), lambda b: (first + b, col))

    def st_in(shape):
        return pl.BlockSpec((None, None) + shape, lambda b: (layer, b) + (0,) * len(shape))

    def st_out(shape):
        return pl.BlockSpec((None,) + shape, lambda b: (b,) + (0,) * len(shape))

    shapes = [(N_HEADS_MLSTM, HEAD_DIM_MLSTM, HEAD_DIM_MLSTM), (N_HEADS_MLSTM, 1, HEAD_DIM_MLSTM),
              (N_HEADS_MLSTM, 1, LANES)]
    return pl.pallas_call(
        _mlstm_step_kernel,
        grid=(DEC_BATCH,),
        in_specs=[
            z_spec(COL_QM), z_spec(COL_KM), z_spec(COL_VM), z_spec(COL_OM),
            pl.BlockSpec((DEC_SEQ, LANES), lambda b: (first + b, 0)),
            _const_spec((1, D_MLSTM)),
        ] + [st_in(s) for s in shapes],
        out_specs=[pl.BlockSpec((DEC_SEQ, D_MLSTM), lambda b: (b, 0))] + [st_out(s) for s in shapes],
        out_shape=[jax.ShapeDtypeStruct((N_SAMPLE, D_MLSTM), BF16)]
        + [jax.ShapeDtypeStruct((DEC_BATCH,) + s, F32) for s in shapes],
        compiler_params=_params(("parallel",)),
        name="mlstm_step",
    )(z, z, z, z, gif, gnorm, state_c, state_n, state_m)


HALO = 16
GROUPS_PER_TILE = ROW_TILE // DEC_SEQ


def _merge_kernel(xa_ref, gb_ref, gc_ref, xah_ref, gch_ref, s1_ref, s2_ref, cw_ref,
                  ybp_ref, ybs_ref, ycp_ref, ycs_ref, zg_ref, x_ref,
                  wpc_ref, wpa_ref, wpm_ref, wout_ref, x_out, tail_out, u_scr):
    i = pl.program_id(0)
    is_s = i >= N_PROMPT_TILES
    row = lax.broadcasted_iota(jnp.int32, (ROW_TILE, 1), 0)
    pos = jnp.where(is_s, row % DEC_SEQ, row)
    u = gc_ref[...].astype(F32) * xa_ref[...].astype(F32)
    u_halo = gch_ref[...].astype(F32) * xah_ref[...].astype(F32)
    keep = jnp.logical_and(i > 0, jnp.logical_not(is_s))
    h1 = jnp.where(keep, u_halo[HALO - 1:HALO, :], 0.0)
    h2 = jnp.where(keep, u_halo[HALO - 2:HALO - 1, :], 0.0)
    f1 = jnp.where(is_s, s1_ref[...], h1)
    f2 = jnp.where(is_s, s2_ref[...], jnp.where(row == 0, h2, h1))
    u_m1 = jnp.where(pos >= 1, pltpu.roll(u, 1, 0), f1)
    u_m2 = jnp.where(pos >= 2, pltpu.roll(u, 2, 0), f2)
    cw = cw_ref[...]
    y = cw[0:1, :] * u_m2 + cw[1:2, :] * u_m1 + cw[2:3, :] * u
    ya = gb_ref[...].astype(F32) * y

    u_scr[...] = u
    for g in range(GROUPS_PER_TILE):
        for j in range(CONV_W - 1):
            src = (g + 1) * DEC_SEQ - (CONV_W - 1) + j
            tail_out[j, g:g + 1, :] = u_scr[src:src + 1, :]

    yb = jnp.where(is_s, ybs_ref[...], ybp_ref[...])
    yc = jnp.where(is_s, ycs_ref[...], ycp_ref[...])
    g = _sigmoid(zg_ref[...].astype(F32))
    merged = (g[:, 0:D_MODEL] * _dot(ya.astype(BF16), wpc_ref[...])
              + g[:, D_MODEL:2 * D_MODEL] * _dot(yb, wpa_ref[...])
              + g[:, 2 * D_MODEL:3 * D_MODEL] * _dot(yc, wpm_ref[...]))
    x_out[...] = x_ref[...] + _dot(merged.astype(BF16), wout_ref[...])


def _merge(z, zg, s1, s2, conv_w, yb_p, yb_s, yc_p, yc_s, x, wpc, wpa, wpm, wout):
    halo_blocks = ROW_TILE // HALO

    def z_spec(col):
        return pl.BlockSpec((ROW_TILE, D_CONV), lambda i: (i, col))

    def halo_spec(col):
        return pl.BlockSpec((HALO, D_CONV), lambda i: (jnp.maximum(i * halo_blocks - 1, 0), col))

    def prompt_spec(width):
        return pl.BlockSpec((ROW_TILE, width), lambda i: (jnp.minimum(i, N_PROMPT_TILES - 1), 0))

    def weight_spec(shape):
        return pl.BlockSpec(shape, lambda i: (0, 0), pipeline_mode=pl.Buffered(1))

    return pl.pallas_call(
        _merge_kernel,
        grid=(N_TILES,),
        in_specs=[
            z_spec(COL_XA), z_spec(COL_GB), z_spec(COL_GC), halo_spec(COL_XA), halo_spec(COL_GC),
            _const_spec((ROW_TILE, D_CONV)), _const_spec((ROW_TILE, D_CONV)), _const_spec((CONV_W, D_CONV)),
            prompt_spec(D_ATT), _const_spec((ROW_TILE, D_ATT)),
            prompt_spec(D_MLSTM), _const_spec((ROW_TILE, D_MLSTM)),
            _row_spec(N_BRANCH * D_MODEL), _row_spec(D_MODEL),
            weight_spec((D_CONV, D_MODEL)), weight_spec((D_ATT, D_MODEL)), weight_spec((D_MLSTM, D_MODEL)),
            weight_spec((D_MODEL, D_MODEL)),
        ],
        out_specs=[
            _row_spec(D_MODEL),
            pl.BlockSpec((None, CONV_W - 1, GROUPS_PER_TILE, D_CONV), lambda i: (i, 0, 0, 0)),
        ],
        out_shape=[
            jax.ShapeDtypeStruct((N_ROWS, D_MODEL), F32),
            jax.ShapeDtypeStruct((N_TILES, CONV_W - 1, GROUPS_PER_TILE, D_CONV), F32),
        ],
        scratch_shapes=[pltpu.VMEM((ROW_TILE, D_CONV), F32)],
        compiler_params=_params(("parallel",)),
        name="merge",
    )(z, z, z, z, z, s1, s2, conv_w, yb_p, yb_s, yc_p, yc_s, zg, x, wpc, wpa, wpm, wout)


def _granule_copies(idx_ref, first, n, src_hbm, buf, sem, slot):
    return [
        pltpu.make_async_copy(
            src_hbm.at[pl.ds(pl.multiple_of(idx_ref[first + k] * GRAN, GRAN), GRAN), :],
            buf.at[slot, pl.ds(k * GRAN, GRAN), :],
            sem.at[slot])
        for k in range(n)
    ]


def _prefetched_gather(idx_ref, n, src_hbm, buf, sem, first_group=0):
    step = pl.program_id(0)

    @pl.when(step == 0)
    def _():
        for cp in _granule_copies(idx_ref, first_group * n, n, src_hbm, buf, sem, 0):
            cp.start()

    @pl.when(step + 1 < pl.num_programs(0))
    def _():
        for cp in _granule_copies(idx_ref, (first_group + step + 1) * n, n, src_hbm, buf, sem, (step + 1) % 2):
            cp.start()

    slot = step % 2
    for cp in _granule_copies(idx_ref, (first_group + step) * n, n, src_hbm, buf, sem, slot):
        cp.wait()
    return slot


def _expert_ffn_kernel(src_ref, expert_ref, first_ref, valid_ref, xs_hbm, wg_ref, wu_ref, wd_ref, o_ref,
                       buf, sem, wg_bf, wu_bf, wd_bf):
    c = pl.program_id(0)
    slot = _prefetched_gather(src_ref, CHUNK_GRANS, xs_hbm, buf, sem)

    @pl.when(first_ref[c] == 1)
    def _():
        wg_bf[...] = wg_ref[...].astype(BF16)
        wu_bf[...] = wu_ref[...].astype(BF16)
        wd_bf[...] = wd_ref[...].astype(BF16)

    @pl.when(valid_ref[c] == 1)
    def _():
        rows = buf[slot]
        x = rows[:, :D_MODEL]
        meta = rows[:, D_MODEL:].astype(F32)
        w_first = meta[:, 1:2] + meta[:, 2:3] + meta[:, 3:4]
        w_second = meta[:, 5:6] + meta[:, 6:7] + meta[:, 7:8]
        w = jnp.where(meta[:, 0:1] == expert_ref[c].astype(F32), w_first, w_second)
        gate = _dot(x, wg_bf[...])
        a = gate * _sigmoid(gate) * _dot(x, wu_bf[...]) * w
        o_ref[...] = _dot(a.astype(BF16), wd_bf[...]).astype(o_ref.dtype)

    @pl.when(valid_ref[c] == 0)
    def _():
        o_ref[...] = jnp.zeros_like(o_ref)


def _expert_ffn(src, chunk_expert, chunk_first, chunk_valid, xs, w_gate, w_up, w_down, layer):
    def w_spec(rows, cols):
        return pl.BlockSpec((None, None, rows, cols), lambda c, src, ce, cf, cv: (layer, ce[c], 0, 0))

    grid_spec = pltpu.PrefetchScalarGridSpec(
        num_scalar_prefetch=4,
        grid=(MAX_CHUNKS,),
        in_specs=[
            pl.BlockSpec(memory_space=pl.ANY),
            w_spec(D_MODEL, D_EXPERT), w_spec(D_MODEL, D_EXPERT), w_spec(D_EXPERT, D_MODEL),
        ],
        out_specs=pl.BlockSpec((CHUNK_ROWS, D_MODEL), lambda c, src, ce, cf, cv: (c, 0)),
        scratch_shapes=[
            pltpu.VMEM((2, CHUNK_ROWS, XS_WIDTH), BF16),
            pltpu.SemaphoreType.DMA((2,)),
            pltpu.VMEM((D_MODEL, D_EXPERT), BF16),
            pltpu.VMEM((D_MODEL, D_EXPERT), BF16),
            pltpu.VMEM((D_EXPERT, D_MODEL), BF16),
        ],
    )
    return pl.pallas_call(
        _expert_ffn_kernel,
        grid_spec=grid_spec,
        out_shape=jax.ShapeDtypeStruct((MAX_CHUNKS * CHUNK_ROWS, D_MODEL), BF16),
        compiler_params=_params(("arbitrary",)),
        name="expert_ffn",
    )(src, chunk_expert, chunk_first, chunk_valid, xs, w_gate, w_up, w_down)


def _combine(dst_ref, ys_hbm, lp_ref, x_ref, buf, sem, first_tile=0):
    slot = _prefetched_gather(dst_ref, TILE_GRANS, ys_hbm, buf, sem, first_tile)
    lp = lp_ref[...]
    p_idx = lax.broadcasted_iota(jnp.int32, (ROW_TILE, LOCAL_ROWS), 1).astype(F32)
    pick = jnp.where((p_idx == lp[:, 0:1]) | (p_idx == lp[:, 1:2]), 1.0, 0.0).astype(BF16)
    return x_ref[...] + _dot(pick, buf[slot])


def _combine_norm_kernel(dst_ref, ys_hbm, lp_ref, x_ref, g_ref, whi_ref, wlo_ref, b_ref,
                         x_out, h_out, s_out, buf, sem):
    x = _combine(dst_ref, ys_hbm, lp_ref, x_ref, buf, sem)
    x_out[...] = x
    h = _rms(x, g_ref[...])
    h_out[...] = h.astype(BF16)
    s_out[...] = _small_proj(h, whi_ref, wlo_ref, b_ref)


def _combine_final_kernel(dst_ref, ys_hbm, lp_ref, x_ref, g_ref, o_ref, buf, sem, *, first_tile):
    o_ref[...] = _rms(_combine(dst_ref, ys_hbm, lp_ref, x_ref, buf, sem, first_tile), g_ref[...])


_COMBINE_SCRATCH = [pltpu.VMEM((2, LOCAL_ROWS, D_MODEL), BF16), pltpu.SemaphoreType.DMA((2,))]


def _combine_norm(dst, ys, lp, x, g, whi, wlo, b):
    def row(width):
        return pl.BlockSpec((ROW_TILE, width), lambda i, dst: (i, 0))

    def const(shape):
        return pl.BlockSpec(shape, lambda i, dst: (0,) * len(shape))

    grid_spec = pltpu.PrefetchScalarGridSpec(
        num_scalar_prefetch=1,
        grid=(N_TILES,),
        in_specs=[pl.BlockSpec(memory_space=pl.ANY), row(LANES), row(D_MODEL), const((1, D_MODEL)),
                  const((D_MODEL, LANES)), const((D_MODEL, LANES)), const((1, LANES))],
        out_specs=[row(D_MODEL), row(D_MODEL), row(LANES)],
        scratch_shapes=_COMBINE_SCRATCH,
    )
    return pl.pallas_call(
        _combine_norm_kernel,
        grid_spec=grid_spec,
        out_shape=[
            jax.ShapeDtypeStruct((N_ROWS, D_MODEL), F32),
            jax.ShapeDtypeStruct((N_ROWS, D_MODEL), BF16),
            jax.ShapeDtypeStruct((N_ROWS, LANES), F32),
        ],
        compiler_params=_params(("arbitrary",)),
        name="combine_norm",
    )(dst, ys, lp, x, g, whi, wlo, b)


def _combine_final(dst, ys, lp, x, g, first_tile, n_tiles):
    def row(width):
        return pl.BlockSpec((ROW_TILE, width), lambda i, dst: (i + first_tile, 0))

    grid_spec = pltpu.PrefetchScalarGridSpec(
        num_scalar_prefetch=1,
        grid=(n_tiles,),
        in_specs=[pl.BlockSpec(memory_space=pl.ANY), row(LANES), row(D_MODEL),
                  pl.BlockSpec((1, D_MODEL), lambda i, dst: (0, 0))],
        out_specs=pl.BlockSpec((ROW_TILE, D_MODEL), lambda i, dst: (i, 0)),
        scratch_shapes=_COMBINE_SCRATCH,
    )
    return pl.pallas_call(
        functools.partial(_combine_final_kernel, first_tile=first_tile),
        grid_spec=grid_spec,
        out_shape=jax.ShapeDtypeStruct((n_tiles * ROW_TILE, D_MODEL), F32),
        compiler_params=_params(("arbitrary",)),
        name="combine_final",
    )(dst, ys, lp, x, g)


def _granule_lists(gcnt):
    i32 = jnp.int32
    per_expert = jnp.sum(gcnt, axis=0)
    chunks = (per_expert + CHUNK_GRANS - 1) // CHUNK_GRANS
    chunk_end = jnp.cumsum(chunks)
    chunk_start = chunk_end - chunks
    tile_end = jnp.cumsum(gcnt, axis=0)
    tile_start = tile_end - gcnt
    local_end = jnp.cumsum(gcnt, axis=1)
    local_start = local_end - gcnt
    sorted_start = chunk_start[None, :] * CHUNK_GRANS + tile_start

    c = jnp.arange(MAX_CHUNKS, dtype=i32)
    expert_of_chunk = jnp.sum((c[:, None] >= chunk_end[None, :]).astype(i32), axis=1)
    chunk_valid = (expert_of_chunk < N_EXPERTS).astype(i32)
    chunk_expert = jnp.minimum(expert_of_chunk, N_EXPERTS - 1)
    chunk_first = chunk_valid * (c == chunk_start[chunk_expert]).astype(i32)

    s = jnp.arange(MAX_CHUNKS * CHUNK_GRANS, dtype=i32)
    e_s = chunk_expert[s // CHUNK_GRANS]
    off = s - chunk_start[e_s] * CHUNK_GRANS
    used = (chunk_valid[s // CHUNK_GRANS] == 1) & (off < per_expert[e_s])
    t_s = jnp.minimum(jnp.sum((off[:, None] >= tile_end.T[e_s]).astype(i32), axis=1), N_TILES - 1)
    src = t_s * TILE_GRANS + local_start[t_s, e_s] + off - tile_start[t_s, e_s]
    src = jnp.where(used, src, TILE_GRANS - 1)

    d = jnp.arange(N_TILES * TILE_GRANS, dtype=i32)
    t_d = d // TILE_GRANS
    q = d % TILE_GRANS
    e_d = jnp.sum((q[:, None] >= local_end[t_d]).astype(i32), axis=1)
    used_d = e_d < N_EXPERTS
    e_d = jnp.minimum(e_d, N_EXPERTS - 1)
    dst = jnp.where(used_d, sorted_start[t_d, e_d] + q - local_start[t_d, e_d], 0)
    return src, dst, chunk_expert, chunk_first, chunk_valid


def _pad_lanes(w):
    return jnp.pad(w, [(0, 0)] * (w.ndim - 1) + [(0, LANES - w.shape[-1])])


def _toeplitz_bias(table, n_q, n_k, reach):
    period = 1
    while period < n_q + n_k:
        period *= 2
    j = jnp.arange(period)
    d = jnp.where(j < n_k, j, j - period)
    u = table[:, jnp.clip(reach - d, -REL_CLIP, REL_CLIP) + REL_CLIP]
    flat = jnp.tile(u, (1, n_q))[:, :n_q * (period - 1)]
    return flat.reshape(table.shape[0], n_q, period - 1)[:, :, :n_k]


def _prompt_bias(table):
    bias = _toeplitz_bias(table, ATT_Q, ATT_K, ATT_REACH)
    qc = jnp.arange(ATT_Q)[:, None] // CHUNK
    kc = jnp.arange(ATT_K)[None, :] // CHUNK
    band = (kc >= qc) & (kc <= qc + BAND_CHUNKS)
    return jnp.where(band[None], bias, NEG)


def _step_bias(table, past):
    bias = _toeplitz_bias(table, DEC_SEQ, past + DEC_SEQ, past)
    return bias[:, :, :past], bias[:, :, past:]


def kernel(x_prompt, x_sample, state_conv, cache_k, cache_v, state_C, state_n, state_m, norm_mix, norm_ffn, w_in, conv_w, rel_bias, gate_bias, mlstm_norm, w_proj_conv, w_proj_att, w_proj_mlstm, w_out, router_group, router_group_bias, router_expert, router_expert_bias, w_gate, w_up, w_down, norm_final):
    past = cache_k.shape[2]
    xp = x_prompt.reshape(SEQ, D_MODEL)
    xs = x_sample.reshape(N_SAMPLE, D_MODEL)
    state_n5 = state_n.reshape(DEPTH, DEC_BATCH, N_HEADS_MLSTM, 1, HEAD_DIM_MLSTM)
    state_m5 = jnp.broadcast_to(state_m[..., None, None], (DEPTH, DEC_BATCH, N_HEADS_MLSTM, 1, LANES))

    w_in_t = jnp.swapaxes(w_in, 1, 2)

    x_mid = moe = None
    outs = {k: [] for k in ("p_conv", "p_k", "p_v", "p_C", "p_n", "p_m", "s_conv", "s_k", "s_v", "s_C", "s_n", "s_m")}
    for l in range(DEPTH):
        gif_hi, gif_lo = _split_hi_lo(_pad_lanes(w_in_t[l, MAIN_WIDTH:MAIN_WIDTH + GIF_WIDTH, :].T))
        gif_b = _pad_lanes(gate_bias[l][None, :])
        g_mix = norm_mix[l][None, :]
        if l == 0:
            x, h, gif = _norm_first(xp, xs, g_mix, gif_hi, gif_lo, gif_b)
        else:
            x, h, gif = _combine_norm(*moe, x_mid, g_mix, gif_hi, gif_lo, gif_b)
        z = _project(h, w_in_t, l, 0, MAIN_WIDTH)
        zg = _project(h, w_in_t, l, MAIN_WIDTH + GIF_WIDTH, N_BRANCH * D_MODEL)

        yb_p = _attn_prompt(z, _prompt_bias(rel_bias[l]))
        bias_past, bias_new = _step_bias(rel_bias[l], past)
        yb_s = _attn_step(z, cache_k, cache_v, bias_past, bias_new, l)

        gnorm = mlstm_norm[l][None, :]
        yc_p, p_c, p_n, p_m = _mlstm_prompt(z, gif, gnorm)
        yc_s, s_c, s_n, s_m = _mlstm_step(z, gif, gnorm, state_C, state_n5, state_m5, l)

        prev = state_conv[l]
        s1 = jnp.zeros((DEC_BATCH, DEC_SEQ, D_CONV), F32).at[:, 0].set(prev[:, 1]).reshape(N_SAMPLE, D_CONV)
        s2 = (jnp.zeros((DEC_BATCH, DEC_SEQ, D_CONV), F32).at[:, 0].set(prev[:, 0]).at[:, 1].set(prev[:, 1])
              .reshape(N_SAMPLE, D_CONV))
        x_mid, tails = _merge(z, zg, s1, s2, conv_w[l], yb_p, yb_s, yc_p, yc_s, x,
                              w_proj_conv[l].astype(BF16), w_proj_att[l].astype(BF16),
                              w_proj_mlstm[l].astype(BF16), w_out[l].astype(BF16))

        r_w = _pad_lanes(jnp.concatenate([router_group[l], router_expert[l]], axis=1))
        r_hi, r_lo = _split_hi_lo(r_w)
        r_b = _pad_lanes(jnp.concatenate([router_group_bias[l], router_expert_bias[l]])[None, :])
        xs_local, lp, gcnt = _route_dispatch(x_mid, norm_ffn[l][None, :], r_hi, r_lo, r_b)
        gcnt = gcnt[:, 0, N_GROUPS:N_GROUPS + N_EXPERTS].astype(jnp.int32)
        src, dst, chunk_expert, chunk_first, chunk_valid = _granule_lists(gcnt)
        ys = _expert_ffn(src, chunk_expert, chunk_first, chunk_valid, xs_local, w_gate, w_up, w_down, l)
        moe = (dst, ys, lp)

        keep = min(ATT_REACH, SEQ)
        k_rows = z[SEQ - keep:, COL_K * D_ATT:(COL_K + 1) * D_ATT].astype(F32)
        v_rows = z[SEQ - keep:, COL_V * D_ATT:(COL_V + 1) * D_ATT].astype(F32)
        outs["p_conv"].append(tails[N_PROMPT_TILES - 1, :, GROUPS_PER_TILE - 1][None])
        outs["p_k"].append(k_rows[:keep].reshape(1, keep, N_HEADS_ATT, HEAD_DIM_ATT))
        outs["p_v"].append(v_rows[:keep].reshape(1, keep, N_HEADS_ATT, HEAD_DIM_ATT))
        outs["p_C"].append(p_c[None])
        outs["p_n"].append(p_n[:, 0][None])
        outs["p_m"].append(p_m[:, 0, 0][None])
        outs["s_conv"].append(jnp.swapaxes(tails[N_PROMPT_TILES], 0, 1))
        outs["s_k"].append(k_rows[keep:].reshape(DEC_BATCH, DEC_SEQ, N_HEADS_ATT, HEAD_DIM_ATT))
        outs["s_v"].append(v_rows[keep:].reshape(DEC_BATCH, DEC_SEQ, N_HEADS_ATT, HEAD_DIM_ATT))
        outs["s_C"].append(s_c)
        outs["s_n"].append(s_n[:, :, 0])
        outs["s_m"].append(s_m[:, :, 0, 0])

    g_fin = norm_final[None, :]
    y_prompt = _combine_final(*moe, x_mid, g_fin, 0, N_PROMPT_TILES).reshape(x_prompt.shape)
    y_sample = _combine_final(*moe, x_mid, g_fin, N_PROMPT_TILES, 1).reshape(x_sample.shape)
    st = {k: jnp.stack(v) for k, v in outs.items()}
    return (y_prompt, y_sample, st["p_conv"], st["p_k"], st["p_v"], st["p_C"], st["p_n"], st["p_m"],
            st["s_conv"], st["s_k"], st["s_v"], st["s_C"], st["s_n"], st["s_m"])
```

```python
import functools

import jax
import jax.numpy as jnp
from jax import lax
from jax.experimental import pallas as pl
from jax.experimental.pallas import tpu as pltpu

D_MODEL = 2048
SEQ = 8192
DEPTH = 2
DEC_BATCH = 8
DEC_SEQ = 32
N_SAMPLE = DEC_BATCH * DEC_SEQ
N_ROWS = SEQ + N_SAMPLE

CHUNK = 64
D_CONV = 1024
CONV_W = 3
N_HEADS_ATT = 8
HEAD_DIM_ATT = 128
D_ATT = N_HEADS_ATT * HEAD_DIM_ATT
BAND_CHUNKS = 8
ATT_REACH = BAND_CHUNKS * CHUNK
REL_CLIP = 128
N_HEADS_MLSTM = 4
HEAD_DIM_MLSTM = 256
D_MLSTM = N_HEADS_MLSTM * HEAD_DIM_MLSTM
N_BRANCH = 3
MAIN_WIDTH = 3 * D_CONV + 3 * D_ATT + 4 * D_MLSTM
GIF_WIDTH = 2 * N_HEADS_MLSTM
N_GROUPS = 4
EXPERTS_PER_GROUP = 4
N_EXPERTS = N_GROUPS * EXPERTS_PER_GROUP
D_EXPERT = 512
EPS = 1e-6

LANES = 128
ROW_TILE = 256
N_TILES = N_ROWS // ROW_TILE
N_PROMPT_TILES = SEQ // ROW_TILE
MM_ROWS = 1056
MM_COLS = 1024
ATT_Q = 256
ATT_K = ATT_Q + ATT_REACH
ATT_VARIANTS = ATT_REACH // ATT_Q + 1
MLSTM_L = 256
NEG = -1e30
VMEM_LIMIT = 56 * 1024 * 1024

GRAN = 16
XS_WIDTH = D_MODEL + LANES
MAX_RUN_PAD = N_EXPERTS * (GRAN - 1)
TILE_GRANS = (2 * ROW_TILE + MAX_RUN_PAD) // GRAN + 1
LOCAL_ROWS = TILE_GRANS * GRAN
CHUNK_GRANS = 16
CHUNK_ROWS = CHUNK_GRANS * GRAN
MAX_CHUNKS = (N_TILES * (TILE_GRANS - 1)) // CHUNK_GRANS + N_EXPERTS

COL_XA, COL_GB, COL_GC, COL_Q, COL_K, COL_V, COL_QM, COL_KM, COL_VM, COL_OM = range(10)

F32 = jnp.float32
BF16 = jnp.bfloat16


def _params(sem):
    return pltpu.CompilerParams(dimension_semantics=sem, vmem_limit_bytes=VMEM_LIMIT)


def _dot(a, b):
    return jnp.dot(a, b, preferred_element_type=F32)


def _dot_nt(a, b):
    return lax.dot_general(a, b, (((1,), (1,)), ((), ())), preferred_element_type=F32)


def _dot_tn(a, b):
    return lax.dot_general(a, b, (((0,), (0,)), ((), ())), preferred_element_type=F32)


def _split_hi_lo(w):
    hi = w.astype(BF16)
    lo = (w - hi.astype(F32)).astype(BF16)
    return hi, lo


def _sigmoid(x):
    return 1.0 / (1.0 + jnp.exp(-x))


def _log_sigmoid(x):
    return jnp.minimum(x, 0.0) - jnp.log1p(jnp.exp(-jnp.abs(x)))


def _rms(x, g):
    ms = jnp.mean(x * x, axis=-1, keepdims=True)
    return x * lax.rsqrt(ms + EPS) * g


def _small_proj(h, whi_ref, wlo_ref, b_ref):
    h_hi = h.astype(BF16)
    h_lo = (h - h_hi.astype(F32)).astype(BF16)
    whi = whi_ref[...]
    return _dot(h_hi, whi) + _dot(h_lo, whi) + _dot(h_hi, wlo_ref[...]) + b_ref[...]


def _norm_first_kernel(xp_ref, xs_ref, g_ref, whi_ref, wlo_ref, b_ref, x_out, h_out, s_out):
    i = pl.program_id(0)
    x = jnp.where(i < N_PROMPT_TILES, xp_ref[...], xs_ref[...])
    x_out[...] = x
    h = _rms(x, g_ref[...])
    h_out[...] = h.astype(BF16)
    s_out[...] = _small_proj(h, whi_ref, wlo_ref, b_ref)


def _route(logits):
    lane = lax.broadcasted_iota(jnp.int32, logits.shape, 1)
    is_g = lane < N_GROUPS
    gl = jnp.where(is_g, logits, NEG)
    gmax = jnp.max(gl, axis=1, keepdims=True)
    g_sel = jnp.min(jnp.where(is_g & (gl == gmax), lane, LANES), axis=1, keepdims=True)
    p_g = 1.0 / jnp.sum(jnp.where(is_g, jnp.exp(gl - gmax), 0.0), axis=1, keepdims=True)
    e_lane = lane - N_GROUPS
    in_g = (e_lane >= 0) & (e_lane < N_EXPERTS) & ((e_lane // EXPERTS_PER_GROUP) == g_sel)
    e1 = jnp.max(jnp.where(in_g, logits, NEG), axis=1, keepdims=True)
    i1 = jnp.min(jnp.where(in_g & (logits == e1), lane, LANES), axis=1, keepdims=True)
    rest = in_g & (lane != i1)
    e2 = jnp.max(jnp.where(rest, logits, NEG), axis=1, keepdims=True)
    i2 = jnp.min(jnp.where(rest & (logits == e2), lane, LANES), axis=1, keepdims=True)
    r = jnp.exp(e2 - e1)
    w1 = p_g / (1.0 + r)
    w2 = w1 * r
    return i1, i2, w1, w2


def _bf16_pieces(w):
    a = w.astype(BF16).astype(F32)
    b = (w - a).astype(BF16).astype(F32)
    return a, b, w - a - b


def _route_dispatch_kernel(x_ref, g_ref, whi_ref, wlo_ref, b_ref, xs_out, lp_out, cnt_out):
    h = _rms(x_ref[...], g_ref[...])
    i1, i2, w1, w2 = _route(_small_proj(h, whi_ref, wlo_ref, b_ref))
    lane = lax.broadcasted_iota(jnp.int32, (ROW_TILE, LANES), 1)
    o1 = lane == i1
    o2 = lane == i2
    onehot = jnp.where(o1 | o2, 1.0, 0.0)
    t_idx = lax.broadcasted_iota(jnp.int32, (ROW_TILE, ROW_TILE), 0)
    s_idx = lax.broadcasted_iota(jnp.int32, (ROW_TILE, ROW_TILE), 1)
    earlier = jnp.where(s_idx < t_idx, 1.0, 0.0).astype(BF16)
    rank = _dot(earlier, onehot.astype(BF16))
    gcnt = jnp.floor((jnp.sum(onehot, axis=0, keepdims=True) + (GRAN - 1)) * (1.0 / GRAN))
    a_idx = lax.broadcasted_iota(jnp.int32, (LANES, LANES), 0)
    b_idx = lax.broadcasted_iota(jnp.int32, (LANES, LANES), 1)
    before = jnp.where(a_idx < b_idx, 1.0, 0.0).astype(BF16)
    gcnt8 = jnp.broadcast_to(gcnt, (8, LANES))
    run_start = _dot(gcnt8.astype(BF16), before)[0:1, :] * GRAN
    pos = run_start + rank
    lpos1 = jnp.sum(jnp.where(o1, pos, 0.0), axis=1, keepdims=True)
    lpos2 = jnp.sum(jnp.where(o2, pos, 0.0), axis=1, keepdims=True)
    lp = jnp.where(lane == 0, lpos1, jnp.where(lane == 1, lpos2, -1.0))
    lp_out[...] = lp
    cnt_out[...] = gcnt8

    meta = jnp.zeros((ROW_TILE, LANES), F32)
    fields = ((i1 - N_GROUPS).astype(F32),) + _bf16_pieces(w1) + ((i2 - N_GROUPS).astype(F32),) + _bf16_pieces(w2)
    for k, val in enumerate(fields):
        meta = jnp.where(lane == k, val, meta)
    rows = jnp.concatenate([h.astype(BF16), meta.astype(BF16)], axis=1)
    lp_t = lp.T
    p_idx = lax.broadcasted_iota(jnp.int32, (LOCAL_ROWS, ROW_TILE), 0).astype(F32)
    perm = jnp.where((p_idx == lp_t[0:1, :]) | (p_idx == lp_t[1:2, :]), 1.0, 0.0).astype(BF16)
    xs_out[...] = _dot(perm, rows).astype(BF16)


def _row_spec(width):
    return pl.BlockSpec((ROW_TILE, width), lambda i: (i, 0))


def _const_spec(shape):
    return pl.BlockSpec(shape, lambda i: (0,) * len(shape))


def _norm_first(xp, xs, g, whi, wlo, b):
    return pl.pallas_call(
        _norm_first_kernel,
        grid=(N_TILES,),
        in_specs=[
            pl.BlockSpec((ROW_TILE, D_MODEL), lambda i: (jnp.minimum(i, N_PROMPT_TILES - 1), 0)),
            _const_spec((ROW_TILE, D_MODEL)),
            _const_spec((1, D_MODEL)),
            _const_spec((D_MODEL, LANES)),
            _const_spec((D_MODEL, LANES)),
            _const_spec((1, LANES)),
        ],
        out_specs=[_row_spec(D_MODEL), _row_spec(D_MODEL), _row_spec(LANES)],
        out_shape=[
            jax.ShapeDtypeStruct((N_ROWS, D_MODEL), F32),
            jax.ShapeDtypeStruct((N_ROWS, D_MODEL), BF16),
            jax.ShapeDtypeStruct((N_ROWS, LANES), F32),
        ],
        compiler_params=_params(("parallel",)),
        name="norm_first",
    )(xp, xs, g, whi, wlo, b)


def _route_dispatch(x, g, whi, wlo, b):
    return pl.pallas_call(
        _route_dispatch_kernel,
        grid=(N_TILES,),
        in_specs=[
            _row_spec(D_MODEL),
            _const_spec((1, D_MODEL)),
            _const_spec((D_MODEL, LANES)),
            _const_spec((D_MODEL, LANES)),
            _const_spec((1, LANES)),
        ],
        out_specs=[
            pl.BlockSpec((LOCAL_ROWS, XS_WIDTH), lambda i: (i, 0)),
            _row_spec(LANES),
            pl.BlockSpec((None, 8, LANES), lambda i: (i, 0, 0)),
        ],
        out_shape=[
            jax.ShapeDtypeStruct((N_TILES * LOCAL_ROWS, XS_WIDTH), BF16),
            jax.ShapeDtypeStruct((N_ROWS, LANES), F32),
            jax.ShapeDtypeStruct((N_TILES, 8, LANES), F32),
        ],
        compiler_params=_params(("parallel",)),
        name="route_dispatch",
    )(x, g, whi, wlo, b)


def _mm_kernel(h_ref, wt_ref, o_ref, wb_ref):
    @pl.when(pl.program_id(1) == 0)
    def _():
        wb_ref[...] = wt_ref[0].T.astype(BF16)

    o_ref[...] = _dot(h_ref[...], wb_ref[...]).astype(o_ref.dtype)


def _project(h, wt, layer, row0, n_cols):
    return pl.pallas_call(
        _mm_kernel,
        grid=(n_cols // MM_COLS, N_ROWS // MM_ROWS),
        in_specs=[
            pl.BlockSpec((MM_ROWS, D_MODEL), lambda j, i: (i, 0)),
            pl.BlockSpec((pl.Element(1), pl.Element(MM_COLS), pl.Element(D_MODEL)),
                         lambda j, i: (layer, pl.multiple_of(row0 + j * MM_COLS, 8), 0)),
        ],
        out_specs=pl.BlockSpec((MM_ROWS, MM_COLS), lambda j, i: (i, j)),
        out_shape=jax.ShapeDtypeStruct((N_ROWS, n_cols), BF16),
        scratch_shapes=[pltpu.VMEM((D_MODEL, MM_COLS), BF16)],
        compiler_params=_params(("parallel", "arbitrary")),
        name="project",
    )(h, wt)


def _attn_prompt_kernel(q_ref, k0_ref, k1_ref, k2_ref, v0_ref, v1_ref, v2_ref, bias_ref, o_ref):
    scale = HEAD_DIM_ATT ** -0.5
    ones = jnp.ones((ATT_K, HEAD_DIM_ATT), BF16)
    for h in range(N_HEADS_ATT):
        sl = slice(h * HEAD_DIM_ATT, (h + 1) * HEAD_DIM_ATT)
        q = q_ref[:, sl]
        kk = jnp.concatenate([k0_ref[:, sl], k1_ref[:, sl], k2_ref[:, sl]], axis=0)
        vv = jnp.concatenate([v0_ref[:, sl], v1_ref[:, sl], v2_ref[:, sl]], axis=0)
        s = _dot_nt(q, kk) * scale + bias_ref[h]
        mx = jnp.max(s, axis=1, keepdims=True)
        p = jnp.exp(s - mx).astype(BF16)
        od = _dot(p, jnp.concatenate([vv, ones], axis=1))
        o = od[:, :HEAD_DIM_ATT] / od[:, HEAD_DIM_ATT:HEAD_DIM_ATT + 1]
        o_ref[:, sl] = o.astype(o_ref.dtype)


def _attn_prompt(z, bias):
    def kv_spec(col, back):
        return pl.BlockSpec((ATT_Q, D_ATT), lambda j: (jnp.maximum(j - back, 0), col))

    return pl.pallas_call(
        _attn_prompt_kernel,
        grid=(SEQ // ATT_Q,),
        in_specs=[
            pl.BlockSpec((ATT_Q, D_ATT), lambda j: (j, COL_Q)),
            kv_spec(COL_K, 2), kv_spec(COL_K, 1), kv_spec(COL_K, 0),
            kv_spec(COL_V, 2), kv_spec(COL_V, 1), kv_spec(COL_V, 0),
            pl.BlockSpec((None, N_HEADS_ATT, ATT_Q, ATT_K), lambda j: (jnp.minimum(j, ATT_VARIANTS - 1), 0, 0, 0)),
        ],
        out_specs=pl.BlockSpec((ATT_Q, D_ATT), lambda j: (j, 0)),
        out_shape=jax.ShapeDtypeStruct((SEQ, D_ATT), BF16),
        compiler_params=_params(("parallel",)),
        name="attn_prompt",
    )(z, z, z, z, z, z, z, bias)


def _attn_step_kernel(q_ref, k_ref, v_ref, ck_ref, cv_ref, bp_ref, bn_ref, o_ref):
    scale = HEAD_DIM_ATT ** -0.5
    for h in range(N_HEADS_ATT):
        sl = slice(h * HEAD_DIM_ATT, (h + 1) * HEAD_DIM_ATT)
        q = q_ref[:, sl]
        s_past = _dot_nt(q, ck_ref[:, h, :].astype(BF16)) * scale + bp_ref[h]
        s_new = _dot_nt(q, k_ref[:, sl]) * scale + bn_ref[h]
        mx = jnp.maximum(jnp.max(s_past, axis=1, keepdims=True), jnp.max(s_new, axis=1, keepdims=True))
        p_past = jnp.exp(s_past - mx)
        p_new = jnp.exp(s_new - mx)
        den = jnp.sum(p_past, axis=1, keepdims=True) + jnp.sum(p_new, axis=1, keepdims=True)
        o = _dot(p_past.astype(BF16), cv_ref[:, h, :].astype(BF16)) + _dot(p_new.astype(BF16), v_ref[:, sl])
        o_ref[:, sl] = (o / den).astype(o_ref.dtype)


def _attn_step(z, cache_k, cache_v, bias_past, bias_new, layer):
    first = SEQ // DEC_SEQ
    past = cache_k.shape[2]

    def z_spec(col):
        return pl.BlockSpec((DEC_SEQ, D_ATT), lambda b: (first + b, col))

    cache_spec = pl.BlockSpec((None, None, past, N_HEADS_ATT, HEAD_DIM_ATT), lambda b: (layer, b, 0, 0, 0))
    return pl.pallas_call(
        _attn_step_kernel,
        grid=(DEC_BATCH,),
        in_specs=[
            z_spec(COL_Q), z_spec(COL_K), z_spec(COL_V), cache_spec, cache_spec,
            _const_spec((N_HEADS_ATT, DEC_SEQ, past)),
            _const_spec((N_HEADS_ATT, DEC_SEQ, DEC_SEQ)),
        ],
        out_specs=pl.BlockSpec((DEC_SEQ, D_ATT), lambda b: (b, 0)),
        out_shape=jax.ShapeDtypeStruct((N_SAMPLE, D_ATT), BF16),
        compiler_params=_params(("parallel",)),
        name="attn_step",
    )(z, z, z, cache_k, cache_v, bias_past, bias_new)


def _mlstm_block(q, k, v, om, gates, gnorm, c0, n0, m0):
    li_col, lf_col, li_row, lf_row = gates
    L = q.shape[0]
    kscale = HEAD_DIM_MLSTM ** -0.5
    t_idx = lax.broadcasted_iota(jnp.int32, (L, L), 0)
    s_idx = lax.broadcasted_iota(jnp.int32, (L, L), 1)
    causal = s_idx <= t_idx
    b_col = jnp.sum(jnp.where(causal, lf_row, 0.0), axis=1, keepdims=True)
    b_row = jnp.sum(jnp.where(t_idx <= s_idx, lf_col, 0.0), axis=0, keepdims=True)
    d = jnp.where(causal, b_col - b_row + li_row, NEG)
    inter = b_col + m0
    m_col = jnp.maximum(inter, jnp.max(d, axis=1, keepdims=True))
    w = jnp.exp(d - m_col)
    sc = jnp.exp(inter - m_col)
    qk = _dot_nt(q, k) * kscale * w
    num = sc * _dot_nt(q, c0.astype(BF16)) + _dot(qk.astype(BF16), v)
    qf = q.astype(F32)
    den = sc * jnp.sum(qf * n0, axis=1, keepdims=True) + jnp.sum(qk, axis=1, keepdims=True)
    hh = num / jnp.maximum(jnp.abs(den), jnp.exp(-m_col))
    mu = jnp.mean(hh, axis=1, keepdims=True)
    cen = hh - mu
    var = jnp.mean(cen * cen, axis=1, keepdims=True)
    y = _sigmoid(om.astype(F32)) * (cen * lax.rsqrt(var + EPS) * gnorm)
    m_last = m_col[L - 1:L, :]
    b_last = b_col[L - 1:L, :]
    decay = jnp.exp(b_last + m0 - m_last)
    ws = jnp.exp(b_last - b_col + li_col - m_last) * kscale
    vs = (v.astype(F32) * ws).astype(BF16)
    c1 = decay * c0 + _dot_tn(vs, k)
    n1 = decay * n0 + jnp.sum(k.astype(F32) * ws, axis=0, keepdims=True)
    return y, c1, n1, m_last


def _gate_views(gif, head):
    gt = gif.T
    li_col = gif[:, head:head + 1]
    lf_col = _log_sigmoid(gif[:, N_HEADS_MLSTM + head:N_HEADS_MLSTM + head + 1])
    li_row = gt[head:head + 1, :]
    lf_row = _log_sigmoid(gt[N_HEADS_MLSTM + head:N_HEADS_MLSTM + head + 1, :])
    return li_col, lf_col, li_row, lf_row


def _mlstm_prompt_kernel(q_ref, k_ref, v_ref, om_ref, gif_ref, gn_ref, y_ref, c_out, n_out, m_out,
                         c_scr, n_scr, m_scr):
    step = pl.program_id(0)

    @pl.when(step == 0)
    def _():
        c_scr[...] = jnp.zeros_like(c_scr)
        n_scr[...] = jnp.zeros_like(n_scr)
        m_scr[...] = jnp.zeros_like(m_scr)

    gif = gif_ref[...]
    for h in range(N_HEADS_MLSTM):
        sl = slice(h * HEAD_DIM_MLSTM, (h + 1) * HEAD_DIM_MLSTM)
        y, c1, n1, m1 = _mlstm_block(q_ref[:, sl], k_ref[:, sl], v_ref[:, sl], om_ref[:, sl],
                                     _gate_views(gif, h), gn_ref[:, sl],
                                     c_scr[h], n_scr[h], m_scr[h][:, :1])
        y_ref[:, sl] = y.astype(y_ref.dtype)
        c_scr[h] = c1
        n_scr[h] = n1
        m_scr[h] = jnp.broadcast_to(m1, (1, LANES))

    @pl.when(step == pl.num_programs(0) - 1)
    def _():
        c_out[...] = c_scr[...]
        n_out[...] = n_scr[...]
        m_out[...] = m_scr[...]


def _mlstm_prompt(z, gif, gnorm):
    def z_spec(col):
        return pl.BlockSpec((MLSTM_L, D_MLSTM), lambda c: (c, col))

    state_shapes = [
        jax.ShapeDtypeStruct((N_HEADS_MLSTM, HEAD_DIM_MLSTM, HEAD_DIM_MLSTM), F32),
        jax.ShapeDtypeStruct((N_HEADS_MLSTM, 1, HEAD_DIM_MLSTM), F32),
        jax.ShapeDtypeStruct((N_HEADS_MLSTM, 1, LANES), F32),
    ]
    return pl.pallas_call(
        _mlstm_prompt_kernel,
        grid=(SEQ // MLSTM_L,),
        in_specs=[
            z_spec(COL_QM), z_spec(COL_KM), z_spec(COL_VM), z_spec(COL_OM),
            pl.BlockSpec((MLSTM_L, LANES), lambda c: (c, 0)),
            _const_spec((1, D_MLSTM)),
        ],
        out_specs=[pl.BlockSpec((MLSTM_L, D_MLSTM), lambda c: (c, 0))]
        + [_const_spec(s.shape) for s in state_shapes],
        out_shape=[jax.ShapeDtypeStruct((SEQ, D_MLSTM), BF16)] + state_shapes,
        scratch_shapes=[pltpu.VMEM(s.shape, F32) for s in state_shapes],
        compiler_params=_params(("arbitrary",)),
        name="mlstm_prompt",
    )(z, z, z, z, gif, gnorm)


def _mlstm_step_kernel(q_ref, k_ref, v_ref, om_ref, gif_ref, gn_ref, c_ref, n_ref, m_ref,
                       y_ref, c_out, n_out, m_out):
    gif = gif_ref[...]
    for h in range(N_HEADS_MLSTM):
        sl = slice(h * HEAD_DIM_MLSTM, (h + 1) * HEAD_DIM_MLSTM)
        y, c1, n1, m1 = _mlstm_block(q_ref[:, sl], k_ref[:, sl], v_ref[:, sl], om_ref[:, sl],
                                     _gate_views(gif, h), gn_ref[:, sl],
                                     c_ref[h], n_ref[h], m_ref[h][:, :1])
        y_ref[:, sl] = y.astype(y_ref.dtype)
        c_out[h] = c1
        n_out[h] = n1
        m_out[h] = jnp.broadcast_to(m1, (1, LANES))


def _mlstm_step(z, gif, gnorm, state_c, state_n, state_m, layer):
    first = SEQ // DEC_SEQ

    def z_spec(col):
        return pl.BlockSpec((DEC_SEQ, D_MLSTM), lambda b: (first + b, col))

    def st_in(shape):
        return pl.BlockSpec((None, None) + shape, lambda b: (layer, b) + (0,) * len(shape))

    def st_out(shape):
        return pl.BlockSpec((None,) + shape, lambda b: (b,) + (0,) * len(shape))

    shapes = [(N_HEADS_MLSTM, HEAD_DIM_MLSTM, HEAD_DIM_MLSTM), (N_HEADS_MLSTM, 1, HEAD_DIM_MLSTM),
              (N_HEADS_MLSTM, 1, LANES)]
    return pl.pallas_call(
        _mlstm_step_kernel,
        grid=(DEC_BATCH,),
        in_specs=[
            z_spec(COL_QM), z_spec(COL_KM), z_spec(COL_VM), z_spec(COL_OM),
            pl.BlockSpec((DEC_SEQ, LANES), lambda b: (first + b, 0)),
            _const_spec((1, D_MLSTM)),
        ] + [st_in(s) for s in shapes],
        out_specs=[pl.BlockSpec((DEC_SEQ, D_MLSTM), lambda b: (b, 0))] + [st_out(s) for s in shapes],
        out_shape=[jax.ShapeDtypeStruct((N_SAMPLE, D_MLSTM), BF16)]
        + [jax.ShapeDtypeStruct((DEC_BATCH,) + s, F32) for s in shapes],
        compiler_params=_params(("parallel",)),
        name="mlstm_step",
    )(z, z, z, z, gif, gnorm, state_c, state_n, state_m)


HALO = 16
GROUPS_PER_TILE = ROW_TILE // DEC_SEQ


def _merge_kernel(xa_ref, gb_ref, gc_ref, xah_ref, gch_ref, s1_ref, s2_ref, cw_ref,
                  ybp_ref, ybs_ref, ycp_ref, ycs_ref, zg_ref, x_ref,
                  wpc_ref, wpa_ref, wpm_ref, wout_ref, x_out, tail_out, u_scr):
    i = pl.program_id(0)
    is_s = i >= N_PROMPT_TILES
    row = lax.broadcasted_iota(jnp.int32, (ROW_TILE, 1), 0)
    pos = jnp.where(is_s, row % DEC_SEQ, row)
    u = gc_ref[...].astype(F32) * xa_ref[...].astype(F32)
    u_halo = gch_ref[...].astype(F32) * xah_ref[...].astype(F32)
    keep = jnp.logical_and(i > 0, jnp.logical_not(is_s))
    h1 = jnp.where(keep, u_halo[HALO - 1:HALO, :], 0.0)
    h2 = jnp.where(keep, u_halo[HALO - 2:HALO - 1, :], 0.0)
    f1 = jnp.where(is_s, s1_ref[...], h1)
    f2 = jnp.where(is_s, s2_ref[...], jnp.where(row == 0, h2, h1))
    u_m1 = jnp.where(pos >= 1, pltpu.roll(u, 1, 0), f1)
    u_m2 = jnp.where(pos >= 2, pltpu.roll(u, 2, 0), f2)
    cw = cw_ref[...]
    y = cw[0:1, :] * u_m2 + cw[1:2, :] * u_m1 + cw[2:3, :] * u
    ya = gb_ref[...].astype(F32) * y

    u_scr[...] = u
    for g in range(GROUPS_PER_TILE):
        for j in range(CONV_W - 1):
            src = (g + 1) * DEC_SEQ - (CONV_W - 1) + j
            tail_out[j, g:g + 1, :] = u_scr[src:src + 1, :]

    yb = jnp.where(is_s, ybs_ref[...], ybp_ref[...])
    yc = jnp.where(is_s, ycs_ref[...], ycp_ref[...])
    g = _sigmoid(zg_ref[...].astype(F32))
    merged = (g[:, 0:D_MODEL] * _dot(ya.astype(BF16), wpc_ref[...])
              + g[:, D_MODEL:2 * D_MODEL] * _dot(yb, wpa_ref[...])
              + g[:, 2 * D_MODEL:3 * D_MODEL] * _dot(yc, wpm_ref[...]))
    x_out[...] = x_ref[...] + _dot(merged.astype(BF16), wout_ref[...])


def _merge(z, zg, s1, s2, conv_w, yb_p, yb_s, yc_p, yc_s, x, wpc, wpa, wpm, wout):
    halo_blocks = ROW_TILE // HALO

    def z_spec(col):
        return pl.BlockSpec((ROW_TILE, D_CONV), lambda i: (i, col))

    def halo_spec(col):
        return pl.BlockSpec((HALO, D_CONV), lambda i: (jnp.maximum(i * halo_blocks - 1, 0), col))

    def prompt_spec(width):
        return pl.BlockSpec((ROW_TILE, width), lambda i: (jnp.minimum(i, N_PROMPT_TILES - 1), 0))

    def weight_spec(shape):
        return pl.BlockSpec(shape, lambda i: (0, 0), pipeline_mode=pl.Buffered(1))

    return pl.pallas_call(
        _merge_kernel,
        grid=(N_TILES,),
        in_specs=[
            z_spec(COL_XA), z_spec(COL_GB), z_spec(COL_GC), halo_spec(COL_XA), halo_spec(COL_GC),
            _const_spec((ROW_TILE, D_CONV)), _const_spec((ROW_TILE, D_CONV)), _const_spec((CONV_W, D_CONV)),
            prompt_spec(D_ATT), _const_spec((ROW_TILE, D_ATT)),
            prompt_spec(D_MLSTM), _const_spec((ROW_TILE, D_MLSTM)),
            _row_spec(N_BRANCH * D_MODEL), _row_spec(D_MODEL),
            weight_spec((D_CONV, D_MODEL)), weight_spec((D_ATT, D_MODEL)), weight_spec((D_MLSTM, D_MODEL)),
            weight_spec((D_MODEL, D_MODEL)),
        ],
        out_specs=[
            _row_spec(D_MODEL),
            pl.BlockSpec((None, CONV_W - 1, GROUPS_PER_TILE, D_CONV), lambda i: (i, 0, 0, 0)),
        ],
        out_shape=[
            jax.ShapeDtypeStruct((N_ROWS, D_MODEL), F32),
            jax.ShapeDtypeStruct((N_TILES, CONV_W - 1, GROUPS_PER_TILE, D_CONV), F32),
        ],
        scratch_shapes=[pltpu.VMEM((ROW_TILE, D_CONV), F32)],
        compiler_params=_params(("parallel",)),
        name="merge",
    )(z, z, z, z, z, s1, s2, conv_w, yb_p, yb_s, yc_p, yc_s, zg, x, wpc, wpa, wpm, wout)


def _granule_copies(idx_ref, first, n, src_hbm, buf, sem, slot):
    return [
        pltpu.make_async_copy(
            src_hbm.at[pl.ds(pl.multiple_of(idx_ref[first + k] * GRAN, GRAN), GRAN), :],
            buf.at[slot, pl.ds(k * GRAN, GRAN), :],
            sem.at[slot])
        for k in range(n)
    ]


def _prefetched_gather(idx_ref, n, src_hbm, buf, sem, first_group=0):
    step = pl.program_id(0)

    @pl.when(step == 0)
    def _():
        for cp in _granule_copies(idx_ref, first_group * n, n, src_hbm, buf, sem, 0):
            cp.start()

    @pl.when(step + 1 < pl.num_programs(0))
    def _():
        for cp in _granule_copies(idx_ref, (first_group + step + 1) * n, n, src_hbm, buf, sem, (step + 1) % 2):
            cp.start()

    slot = step % 2
    for cp in _granule_copies(idx_ref, (first_group + step) * n, n, src_hbm, buf, sem, slot):
        cp.wait()
    return slot


def _expert_ffn_kernel(src_ref, expert_ref, first_ref, valid_ref, next_ref, xs_hbm, wg_hbm, wu_hbm, wd_hbm,
                       o_ref, buf, sem, wg_st, wu_st, wd_st, wsem, wg_bf, wu_bf, wd_bf, *, layer):
    c = pl.program_id(0)
    slot = _prefetched_gather(src_ref, CHUNK_GRANS, xs_hbm, buf, sem)

    def weight_copies(e):
        return [pltpu.make_async_copy(wg_hbm.at[layer, e], wg_st, wsem.at[0]),
                pltpu.make_async_copy(wu_hbm.at[layer, e], wu_st, wsem.at[1]),
                pltpu.make_async_copy(wd_hbm.at[layer, e], wd_st, wsem.at[2])]

    @pl.when(c == 0)
    def _():
        for cp in weight_copies(expert_ref[0]):
            cp.start()

    @pl.when(first_ref[c] == 1)
    def _():
        for cp in weight_copies(expert_ref[c]):
            cp.wait()
        wg_bf[...] = wg_st[...].astype(BF16)
        wu_bf[...] = wu_st[...].astype(BF16)
        wd_bf[...] = wd_st[...].astype(BF16)

        @pl.when(next_ref[c] >= 0)
        def _():
            for cp in weight_copies(next_ref[c]):
                cp.start()

    @pl.when(valid_ref[c] == 1)
    def _():
        rows = buf[slot]
        x = rows[:, :D_MODEL]
        meta = rows[:, D_MODEL:].astype(F32)
        w_first = meta[:, 1:2] + meta[:, 2:3] + meta[:, 3:4]
        w_second = meta[:, 5:6] + meta[:, 6:7] + meta[:, 7:8]
        w = jnp.where(meta[:, 0:1] == expert_ref[c].astype(F32), w_first, w_second)
        gate = _dot(x, wg_bf[...])
        a = gate * _sigmoid(gate) * _dot(x, wu_bf[...]) * w
        o_ref[...] = _dot(a.astype(BF16), wd_bf[...]).astype(o_ref.dtype)

    @pl.when(valid_ref[c] == 0)
    def _():
        o_ref[...] = jnp.zeros_like(o_ref)


def _expert_ffn(src, chunk_expert, chunk_first, chunk_valid, chunk_next, xs, w_gate, w_up, w_down, layer):
    any_spec = pl.BlockSpec(memory_space=pl.ANY)
    grid_spec = pltpu.PrefetchScalarGridSpec(
        num_scalar_prefetch=5,
        grid=(MAX_CHUNKS,),
        in_specs=[any_spec, any_spec, any_spec, any_spec],
        out_specs=pl.BlockSpec((CHUNK_ROWS, D_MODEL), lambda c, *_: (c, 0)),
        scratch_shapes=[
            pltpu.VMEM((2, CHUNK_ROWS, XS_WIDTH), BF16),
            pltpu.SemaphoreType.DMA((2,)),
            pltpu.VMEM((D_MODEL, D_EXPERT), F32),
            pltpu.VMEM((D_MODEL, D_EXPERT), F32),
            pltpu.VMEM((D_EXPERT, D_MODEL), F32),
            pltpu.SemaphoreType.DMA((3,)),
            pltpu.VMEM((D_MODEL, D_EXPERT), BF16),
            pltpu.VMEM((D_MODEL, D_EXPERT), BF16),
            pltpu.VMEM((D_EXPERT, D_MODEL), BF16),
        ],
    )
    return pl.pallas_call(
        functools.partial(_expert_ffn_kernel, layer=layer),
        grid_spec=grid_spec,
        out_shape=jax.ShapeDtypeStruct((MAX_CHUNKS * CHUNK_ROWS, D_MODEL), BF16),
        compiler_params=_params(("arbitrary",)),
        name="expert_ffn",
    )(src, chunk_expert, chunk_first, chunk_valid, chunk_next, xs, w_gate, w_up, w_down)


def _combine(dst_ref, ys_hbm, lp_ref, x_ref, buf, sem, first_tile=0):
    slot = _prefetched_gather(dst_ref, TILE_GRANS, ys_hbm, buf, sem, first_tile)
    lp = lp_ref[...]
    p_idx = lax.broadcasted_iota(jnp.int32, (ROW_TILE, LOCAL_ROWS), 1).astype(F32)
    pick = jnp.where((p_idx == lp[:, 0:1]) | (p_idx == lp[:, 1:2]), 1.0, 0.0).astype(BF16)
    return x_ref[...] + _dot(pick, buf[slot])


def _combine_norm_kernel(dst_ref, ys_hbm, lp_ref, x_ref, g_ref, whi_ref, wlo_ref, b_ref,
                         x_out, h_out, s_out, buf, sem):
    x = _combine(dst_ref, ys_hbm, lp_ref, x_ref, buf, sem)
    x_out[...] = x
    h = _rms(x, g_ref[...])
    h_out[...] = h.astype(BF16)
    s_out[...] = _small_proj(h, whi_ref, wlo_ref, b_ref)


def _combine_final_kernel(dst_ref, ys_hbm, lp_ref, x_ref, g_ref, o_ref, buf, sem, *, first_tile):
    o_ref[...] = _rms(_combine(dst_ref, ys_hbm, lp_ref, x_ref, buf, sem, first_tile), g_ref[...])


_COMBINE_SCRATCH = [pltpu.VMEM((2, LOCAL_ROWS, D_MODEL), BF16), pltpu.SemaphoreType.DMA((2,))]


def _combine_norm(dst, ys, lp, x, g, whi, wlo, b):
    def row(width):
        return pl.BlockSpec((ROW_TILE, width), lambda i, dst: (i, 0))

    def const(shape):
        return pl.BlockSpec(shape, lambda i, dst: (0,) * len(shape))

    grid_spec = pltpu.PrefetchScalarGridSpec(
        num_scalar_prefetch=1,
        grid=(N_TILES,),
        in_specs=[pl.BlockSpec(memory_space=pl.ANY), row(LANES), row(D_MODEL), const((1, D_MODEL)),
                  const((D_MODEL, LANES)), const((D_MODEL, LANES)), const((1, LANES))],
        out_specs=[row(D_MODEL), row(D_MODEL), row(LANES)],
        scratch_shapes=_COMBINE_SCRATCH,
    )
    return pl.pallas_call(
        _combine_norm_kernel,
        grid_spec=grid_spec,
        out_shape=[
            jax.ShapeDtypeStruct((N_ROWS, D_MODEL), F32),
            jax.ShapeDtypeStruct((N_ROWS, D_MODEL), BF16),
            jax.ShapeDtypeStruct((N_ROWS, LANES), F32),
        ],
        compiler_params=_params(("arbitrary",)),
        name="combine_norm",
    )(dst, ys, lp, x, g, whi, wlo, b)


def _combine_final(dst, ys, lp, x, g, first_tile, n_tiles):
    def row(width):
        return pl.BlockSpec((ROW_TILE, width), lambda i, dst: (i + first_tile, 0))

    grid_spec = pltpu.PrefetchScalarGridSpec(
        num_scalar_prefetch=1,
        grid=(n_tiles,),
        in_specs=[pl.BlockSpec(memory_space=pl.ANY), row(LANES), row(D_MODEL),
                  pl.BlockSpec((1, D_MODEL), lambda i, dst: (0, 0))],
        out_specs=pl.BlockSpec((ROW_TILE, D_MODEL), lambda i, dst: (i, 0)),
        scratch_shapes=_COMBINE_SCRATCH,
    )
    return pl.pallas_call(
        functools.partial(_combine_final_kernel, first_tile=first_tile),
        grid_spec=grid_spec,
        out_shape=jax.ShapeDtypeStruct((n_tiles * ROW_TILE, D_MODEL), F32),
        compiler_params=_params(("arbitrary",)),
        name="combine_final",
    )(dst, ys, lp, x, g)


N_SRC = MAX_CHUNKS * CHUNK_GRANS
N_DST = N_TILES * TILE_GRANS


def _granule_kernel(gcnt_ref, src_ref, dst_ref, expert_ref, first_ref, valid_ref, next_ref, local_ref):
    def fill(ref, n, value):
        def body(k, carry):
            ref[k] = value
            return carry
        lax.fori_loop(0, n, body, 0)

    fill(src_ref, N_SRC, TILE_GRANS - 1)
    fill(dst_ref, N_DST, 0)
    fill(expert_ref, MAX_CHUNKS, N_EXPERTS - 1)
    fill(first_ref, MAX_CHUNKS, 0)
    fill(valid_ref, MAX_CHUNKS, 0)
    fill(next_ref, MAX_CHUNKS + 1, -1)
    fill(local_ref, N_TILES, 0)

    def per_expert(e, carry):
        chunk, prev_first = carry
        slot0 = chunk * CHUNK_GRANS

        def per_tile(t, pos):
            n = gcnt_ref[t * N_EXPERTS + e]
            local0 = t * TILE_GRANS + local_ref[t]

            def per_granule(k, c2):
                src_ref[pos + k] = local0 + k
                dst_ref[local0 + k] = pos + k
                return c2

            lax.fori_loop(0, n, per_granule, 0)
            local_ref[t] = local_ref[t] + n
            return pos + n

        end = lax.fori_loop(0, N_TILES, per_tile, slot0)
        n_chunks = (end - slot0 + CHUNK_GRANS - 1) // CHUNK_GRANS

        def per_chunk(k, c2):
            expert_ref[chunk + k] = e
            valid_ref[chunk + k] = 1
            return c2

        lax.fori_loop(0, n_chunks, per_chunk, 0)
        owns = n_chunks > 0

        @pl.when(owns)
        def _():
            first_ref[chunk] = 1
            next_ref[prev_first] = e

        return chunk + n_chunks, jnp.where(owns, chunk, prev_first)

    lax.fori_loop(0, N_EXPERTS, per_expert, (0, MAX_CHUNKS))


def _granule_lists(gcnt):
    smem = pl.BlockSpec(memory_space=pltpu.SMEM)
    sizes = (N_SRC, N_DST, MAX_CHUNKS, MAX_CHUNKS, MAX_CHUNKS, MAX_CHUNKS + 1)
    return pl.pallas_call(
        _granule_kernel,
        in_specs=[smem],
        out_specs=[smem] * len(sizes),
        out_shape=[jax.ShapeDtypeStruct((n,), jnp.int32) for n in sizes],
        scratch_shapes=[pltpu.SMEM((N_TILES,), jnp.int32)],
        name="granule_lists",
    )(gcnt.reshape(N_TILES * N_EXPERTS))


def _pad_lanes(w):
    return jnp.pad(w, [(0, 0)] * (w.ndim - 1) + [(0, LANES - w.shape[-1])])


def _toeplitz_bias(table, n_q, n_k, reach):
    period = 1
    while period < n_q + n_k:
        period *= 2
    j = jnp.arange(period)
    d = jnp.where(j < n_k, j, j - period)
    u = table[:, jnp.clip(reach - d, -REL_CLIP, REL_CLIP) + REL_CLIP]
    flat = jnp.tile(u, (1, n_q))[:, :n_q * (period - 1)]
    return flat.reshape(table.shape[0], n_q, period - 1)[:, :, :n_k]


def _prompt_bias(table):
    bias = _toeplitz_bias(table, ATT_Q, ATT_K, ATT_REACH)
    qc = jnp.arange(ATT_Q)[:, None] // CHUNK
    m = jnp.arange(ATT_K)[None, :]
    kc = m // CHUNK
    band = (kc >= qc) & (kc <= qc + BAND_CHUNKS)
    first_key = ATT_REACH - ATT_Q * jnp.arange(ATT_VARIANTS)
    ok = band[None] & (m[None] >= first_key[:, None, None])
    return jnp.where(ok[:, None], bias[None], NEG)


def _step_bias(table, past):
    bias = _toeplitz_bias(table, DEC_SEQ, past + DEC_SEQ, past)
    return bias[:, :, :past], bias[:, :, past:]


def kernel(x_prompt, x_sample, state_conv, cache_k, cache_v, state_C, state_n, state_m, norm_mix, norm_ffn, w_in, conv_w, rel_bias, gate_bias, mlstm_norm, w_proj_conv, w_proj_att, w_proj_mlstm, w_out, router_group, router_group_bias, router_expert, router_expert_bias, w_gate, w_up, w_down, norm_final):
    past = cache_k.shape[2]
    xp = x_prompt.reshape(SEQ, D_MODEL)
    xs = x_sample.reshape(N_SAMPLE, D_MODEL)
    state_n5 = state_n.reshape(DEPTH, DEC_BATCH, N_HEADS_MLSTM, 1, HEAD_DIM_MLSTM)
    state_m5 = jnp.broadcast_to(state_m[..., None, None], (DEPTH, DEC_BATCH, N_HEADS_MLSTM, 1, LANES))

    w_in_t = jnp.swapaxes(w_in, 1, 2)

    x_mid = moe = None
    outs = {k: [] for k in ("p_conv", "p_k", "p_v", "p_C", "p_n", "p_m", "s_conv", "s_k", "s_v", "s_C", "s_n", "s_m")}
    for l in range(DEPTH):
        gif_hi, gif_lo = _split_hi_lo(_pad_lanes(w_in_t[l, MAIN_WIDTH:MAIN_WIDTH + GIF_WIDTH, :].T))
        gif_b = _pad_lanes(gate_bias[l][None, :])
        g_mix = norm_mix[l][None, :]
        if l == 0:
            x, h, gif = _norm_first(xp, xs, g_mix, gif_hi, gif_lo, gif_b)
        else:
            x, h, gif = _combine_norm(*moe, x_mid, g_mix, gif_hi, gif_lo, gif_b)
        z = _project(h, w_in_t, l, 0, MAIN_WIDTH)
        zg = _project(h, w_in_t, l, MAIN_WIDTH + GIF_WIDTH, N_BRANCH * D_MODEL)

        yb_p = _attn_prompt(z, _prompt_bias(rel_bias[l]))
        bias_past, bias_new = _step_bias(rel_bias[l], past)
        yb_s = _attn_step(z, cache_k, cache_v, bias_past, bias_new, l)

        gnorm = mlstm_norm[l][None, :]
        yc_p, p_c, p_n, p_m = _mlstm_prompt(z, gif, gnorm)
        yc_s, s_c, s_n, s_m = _mlstm_step(z, gif, gnorm, state_C, state_n5, state_m5, l)

        prev = state_conv[l]
        s1 = jnp.zeros((DEC_BATCH, DEC_SEQ, D_CONV), F32).at[:, 0].set(prev[:, 1]).reshape(N_SAMPLE, D_CONV)
        s2 = (jnp.zeros((DEC_BATCH, DEC_SEQ, D_CONV), F32).at[:, 0].set(prev[:, 0]).at[:, 1].set(prev[:, 1])
              .reshape(N_SAMPLE, D_CONV))
        x_mid, tails = _merge(z, zg, s1, s2, conv_w[l], yb_p, yb_s, yc_p, yc_s, x,
                              w_proj_conv[l].astype(BF16), w_proj_att[l].astype(BF16),
                              w_proj_mlstm[l].astype(BF16), w_out[l].astype(BF16))

        r_w = _pad_lanes(jnp.concatenate([router_group[l], router_expert[l]], axis=1))
        r_hi, r_lo = _split_hi_lo(r_w)
        r_b = _pad_lanes(jnp.concatenate([router_group_bias[l], router_expert_bias[l]])[None, :])
        xs_local, lp, gcnt = _route_dispatch(x_mid, norm_ffn[l][None, :], r_hi, r_lo, r_b)
        gcnt = gcnt[:, 0, N_GROUPS:N_GROUPS + N_EXPERTS].astype(jnp.int32)
        src, dst, chunk_expert, chunk_first, chunk_valid, chunk_next = _granule_lists(gcnt)
        ys = _expert_ffn(src, chunk_expert, chunk_first, chunk_valid, chunk_next, xs_local, w_gate, w_up, w_down, l)
        moe = (dst, ys, lp)

        keep = min(ATT_REACH, SEQ)
        k_rows = z[SEQ - keep:, COL_K * D_ATT:(COL_K + 1) * D_ATT].astype(F32)
        v_rows = z[SEQ - keep:, COL_V * D_ATT:(COL_V + 1) * D_ATT].astype(F32)
        outs["p_conv"].append(tails[N_PROMPT_TILES - 1, :, GROUPS_PER_TILE - 1][None])
        outs["p_k"].append(k_rows[:keep].reshape(1, keep, N_HEADS_ATT, HEAD_DIM_ATT))
        outs["p_v"].append(v_rows[:keep].reshape(1, keep, N_HEADS_ATT, HEAD_DIM_ATT))
        outs["p_C"].append(p_c[None])
        outs["p_n"].append(p_n[:, 0][None])
        outs["p_m"].append(p_m[:, 0, 0][None])
        outs["s_conv"].append(jnp.swapaxes(tails[N_PROMPT_TILES], 0, 1))
        outs["s_k"].append(k_rows[keep:].reshape(DEC_BATCH, DEC_SEQ, N_HEADS_ATT, HEAD_DIM_ATT))
        outs["s_v"].append(v_rows[keep:].reshape(DEC_BATCH, DEC_SEQ, N_HEADS_ATT, HEAD_DIM_ATT))
        outs["s_C"].append(s_c)
        outs["s_n"].append(s_n[:, :, 0])
        outs["s_m"].append(s_m[:, :, 0, 0])

    g_fin = norm_final[None, :]
    y_prompt = _combine_final(*moe, x_mid, g_fin, 0, N_PROMPT_TILES).reshape(x_prompt.shape)
    y_sample = _combine_final(*moe, x_mid, g_fin, N_PROMPT_TILES, 1).reshape(x_sample.shape)
    st = {k: jnp.stack(v) for k, v in outs.items()}
    return (y_prompt, y_sample, st["p_conv"], st["p_k"], st["p_v"], st["p_C"], st["p_n"], st["p_m"],
            st["s_conv"], st["s_k"], st["s_v"], st["s_C"], st["s_n"], st["s_m"])
```

```python
import functools
import math

import jax
import jax.numpy as jnp
from jax import lax
from jax.experimental import pallas as pl
from jax.experimental.pallas import tpu as pltpu

D_MODEL = 2048
SEQ = 8192
DEPTH = 2
DEC_BATCH = 8
DEC_SEQ = 32
N_SAMPLE = DEC_BATCH * DEC_SEQ
N_ROWS = SEQ + N_SAMPLE

CHUNK = 64
D_CONV = 1024
CONV_W = 3
N_HEADS_ATT = 8
HEAD_DIM_ATT = 128
D_ATT = N_HEADS_ATT * HEAD_DIM_ATT
BAND_CHUNKS = 8
ATT_REACH = BAND_CHUNKS * CHUNK
REL_CLIP = 128
N_HEADS_MLSTM = 4
HEAD_DIM_MLSTM = 256
D_MLSTM = N_HEADS_MLSTM * HEAD_DIM_MLSTM
N_BRANCH = 3
MAIN_WIDTH = 3 * D_CONV + 3 * D_ATT + 4 * D_MLSTM
GIF_WIDTH = 2 * N_HEADS_MLSTM
N_GROUPS = 4
EXPERTS_PER_GROUP = 4
N_EXPERTS = N_GROUPS * EXPERTS_PER_GROUP
D_EXPERT = 512
EPS = 1e-6

LANES = 128
BF16_ROWS = 16
ROW_TILE = 256
N_TILES = N_ROWS // ROW_TILE
N_PROMPT_TILES = SEQ // ROW_TILE
MM_ROWS = 1056
MM_COLS = 1024
ATT_Q = 256
ATT_K = ATT_Q + ATT_REACH
ATT_VARIANTS = ATT_REACH // ATT_Q + 1
MLSTM_L = 256
NEG = -1e30
VMEM_LIMIT = 56 * 1024 * 1024

GRAN = BF16_ROWS
XS_WIDTH = D_MODEL + LANES
MAX_RUN_PAD = N_EXPERTS * (GRAN - 1)
TILE_GRANS = (2 * ROW_TILE + MAX_RUN_PAD) // GRAN + 1
LOCAL_ROWS = TILE_GRANS * GRAN
CHUNK_GRANS = 32
CHUNK_ROWS = CHUNK_GRANS * GRAN
MAX_CHUNKS = (N_TILES * (TILE_GRANS - 1)) // CHUNK_GRANS + N_EXPERTS

COL_XA, COL_GB, COL_GC, COL_Q, COL_K, COL_V, COL_QM, COL_KM, COL_VM, COL_OM = range(10)

F32 = jnp.float32
BF16 = jnp.bfloat16


def _params(sem):
    return pltpu.CompilerParams(dimension_semantics=sem, vmem_limit_bytes=VMEM_LIMIT)


def _dot(a, b):
    return jnp.dot(a, b, preferred_element_type=F32)


def _dot_nt(a, b):
    return lax.dot_general(a, b, (((1,), (1,)), ((), ())), preferred_element_type=F32)


def _dot_tn(a, b):
    return lax.dot_general(a, b, (((0,), (0,)), ((), ())), preferred_element_type=F32)


def _split_hi_lo(w):
    hi = w.astype(BF16)
    lo = (w - hi.astype(F32)).astype(BF16)
    return hi, lo


def _sigmoid(x):
    return 1.0 / (1.0 + jnp.exp(-x))


def _log_sigmoid(x):
    return jnp.minimum(x, 0.0) - jnp.log1p(jnp.exp(-jnp.abs(x)))


def _rms(x, g):
    ms = jnp.mean(x * x, axis=-1, keepdims=True)
    return x * lax.rsqrt(ms + EPS) * g


def _small_proj(h, whi_ref, wlo_ref, b_ref):
    h_hi = h.astype(BF16)
    h_lo = (h - h_hi.astype(F32)).astype(BF16)
    whi = whi_ref[...]
    return _dot(h_hi, whi) + _dot(h_lo, whi) + _dot(h_hi, wlo_ref[...]) + b_ref[...]


def _norm_first_kernel(xp_ref, xs_ref, g_ref, whi_ref, wlo_ref, b_ref, x_out, h_out, s_out):
    i = pl.program_id(0)
    x = jnp.where(i < N_PROMPT_TILES, xp_ref[...], xs_ref[...])
    x_out[...] = x
    h = _rms(x, g_ref[...])
    h_out[...] = h.astype(BF16)
    s_out[...] = _small_proj(h, whi_ref, wlo_ref, b_ref)


def _route(logits):
    lane = lax.broadcasted_iota(jnp.int32, logits.shape, 1)
    is_g = lane < N_GROUPS
    gl = jnp.where(is_g, logits, NEG)
    gmax = jnp.max(gl, axis=1, keepdims=True)
    g_sel = jnp.min(jnp.where(is_g & (gl == gmax), lane, LANES), axis=1, keepdims=True)
    p_g = 1.0 / jnp.sum(jnp.where(is_g, jnp.exp(gl - gmax), 0.0), axis=1, keepdims=True)
    e_lane = lane - N_GROUPS
    in_g = (e_lane >= 0) & (e_lane < N_EXPERTS) & ((e_lane // EXPERTS_PER_GROUP) == g_sel)
    e1 = jnp.max(jnp.where(in_g, logits, NEG), axis=1, keepdims=True)
    i1 = jnp.min(jnp.where(in_g & (logits == e1), lane, LANES), axis=1, keepdims=True)
    rest = in_g & (lane != i1)
    e2 = jnp.max(jnp.where(rest, logits, NEG), axis=1, keepdims=True)
    i2 = jnp.min(jnp.where(rest & (logits == e2), lane, LANES), axis=1, keepdims=True)
    r = jnp.exp(e2 - e1)
    w1 = p_g / (1.0 + r)
    w2 = w1 * r
    return i1, i2, w1, w2


def _bf16_pieces(w):
    a = w.astype(BF16).astype(F32)
    b = (w - a).astype(BF16).astype(F32)
    return a, b, w - a - b


def _route_dispatch_kernel(x_ref, g_ref, whi_ref, wlo_ref, b_ref, xs_out, lp_out, cnt_out):
    h = _rms(x_ref[...], g_ref[...])
    i1, i2, w1, w2 = _route(_small_proj(h, whi_ref, wlo_ref, b_ref))
    lane = lax.broadcasted_iota(jnp.int32, (ROW_TILE, LANES), 1)
    o1 = lane == i1
    o2 = lane == i2
    onehot = jnp.where(o1 | o2, 1.0, 0.0)
    t_idx = lax.broadcasted_iota(jnp.int32, (ROW_TILE, ROW_TILE), 0)
    s_idx = lax.broadcasted_iota(jnp.int32, (ROW_TILE, ROW_TILE), 1)
    earlier = jnp.where(s_idx < t_idx, 1.0, 0.0).astype(BF16)
    rank = _dot(earlier, onehot.astype(BF16))
    gcnt = jnp.floor((jnp.sum(onehot, axis=0, keepdims=True) + (GRAN - 1)) * (1.0 / GRAN))
    a_idx = lax.broadcasted_iota(jnp.int32, (LANES, LANES), 0)
    b_idx = lax.broadcasted_iota(jnp.int32, (LANES, LANES), 1)
    before = jnp.where(a_idx < b_idx, 1.0, 0.0).astype(BF16)
    gcnt8 = jnp.broadcast_to(gcnt, (8, LANES))
    run_start = _dot(gcnt8.astype(BF16), before)[0:1, :] * GRAN
    pos = run_start + rank
    lpos1 = jnp.sum(jnp.where(o1, pos, 0.0), axis=1, keepdims=True)
    lpos2 = jnp.sum(jnp.where(o2, pos, 0.0), axis=1, keepdims=True)
    lp = jnp.where(lane == 0, lpos1, jnp.where(lane == 1, lpos2, -1.0))
    lp_out[...] = lp
    cnt_out[...] = gcnt8

    meta = jnp.zeros((ROW_TILE, LANES), F32)
    fields = ((i1 - N_GROUPS).astype(F32),) + _bf16_pieces(w1) + ((i2 - N_GROUPS).astype(F32),) + _bf16_pieces(w2)
    for k, val in enumerate(fields):
        meta = jnp.where(lane == k, val, meta)
    rows = jnp.concatenate([h.astype(BF16), meta.astype(BF16)], axis=1)
    lp_t = lp.T
    p_idx = lax.broadcasted_iota(jnp.int32, (LOCAL_ROWS, ROW_TILE), 0).astype(F32)
    perm = jnp.where((p_idx == lp_t[0:1, :]) | (p_idx == lp_t[1:2, :]), 1.0, 0.0).astype(BF16)
    xs_out[...] = _dot(perm, rows).astype(BF16)


def _row_spec(width):
    return pl.BlockSpec((ROW_TILE, width), lambda i: (i, 0))


def _const_spec(shape):
    return pl.BlockSpec(shape, lambda i: (0,) * len(shape))


def _norm_first(xp, xs, g, whi, wlo, b):
    return pl.pallas_call(
        _norm_first_kernel,
        grid=(N_TILES,),
        in_specs=[
            pl.BlockSpec((ROW_TILE, D_MODEL), lambda i: (jnp.minimum(i, N_PROMPT_TILES - 1), 0)),
            _const_spec((ROW_TILE, D_MODEL)),
            _const_spec((1, D_MODEL)),
            _const_spec((D_MODEL, LANES)),
            _const_spec((D_MODEL, LANES)),
            _const_spec((1, LANES)),
        ],
        out_specs=[_row_spec(D_MODEL), _row_spec(D_MODEL), _row_spec(LANES)],
        out_shape=[
            jax.ShapeDtypeStruct((N_ROWS, D_MODEL), F32),
            jax.ShapeDtypeStruct((N_ROWS, D_MODEL), BF16),
            jax.ShapeDtypeStruct((N_ROWS, LANES), F32),
        ],
        compiler_params=_params(("parallel",)),
        name="norm_first",
    )(xp, xs, g, whi, wlo, b)


def _route_dispatch(x, g, whi, wlo, b):
    return pl.pallas_call(
        _route_dispatch_kernel,
        grid=(N_TILES,),
        in_specs=[
            _row_spec(D_MODEL),
            _const_spec((1, D_MODEL)),
            _const_spec((D_MODEL, LANES)),
            _const_spec((D_MODEL, LANES)),
            _const_spec((1, LANES)),
        ],
        out_specs=[
            pl.BlockSpec((LOCAL_ROWS, XS_WIDTH), lambda i: (i, 0)),
            _row_spec(LANES),
            pl.BlockSpec((None, 8, LANES), lambda i: (i, 0, 0)),
        ],
        out_shape=[
            jax.ShapeDtypeStruct((N_TILES * LOCAL_ROWS, XS_WIDTH), BF16),
            jax.ShapeDtypeStruct((N_ROWS, LANES), F32),
            jax.ShapeDtypeStruct((N_TILES, 8, LANES), F32),
        ],
        compiler_params=_params(("parallel",)),
        name="route_dispatch",
    )(x, g, whi, wlo, b)


def _mm_kernel(h_ref, wt_ref, o_ref, wb_ref):
    @pl.when(pl.program_id(1) == 0)
    def _():
        wb_ref[...] = wt_ref[0].T.astype(BF16)

    o_ref[...] = _dot(h_ref[...], wb_ref[...]).astype(o_ref.dtype)


def _project(h, wt, layer, row0, n_cols):
    return pl.pallas_call(
        _mm_kernel,
        grid=(n_cols // MM_COLS, N_ROWS // MM_ROWS),
        in_specs=[
            pl.BlockSpec((MM_ROWS, D_MODEL), lambda j, i: (i, 0)),
            pl.BlockSpec((pl.Element(1), pl.Element(MM_COLS), pl.Element(D_MODEL)),
                         lambda j, i: (layer, pl.multiple_of(row0 + j * MM_COLS, 8), 0)),
        ],
        out_specs=pl.BlockSpec((MM_ROWS, MM_COLS), lambda j, i: (i, j)),
        out_shape=jax.ShapeDtypeStruct((N_ROWS, n_cols), BF16),
        scratch_shapes=[pltpu.VMEM((D_MODEL, MM_COLS), BF16)],
        compiler_params=_params(("parallel", "arbitrary")),
        name="project",
    )(h, wt)


def _attn_prompt_kernel(q_ref, k0_ref, k1_ref, k2_ref, v0_ref, v1_ref, v2_ref, bias_ref, o_ref):
    scale = HEAD_DIM_ATT ** -0.5
    ones = jnp.ones((ATT_K, HEAD_DIM_ATT), BF16)
    for h in range(N_HEADS_ATT):
        sl = slice(h * HEAD_DIM_ATT, (h + 1) * HEAD_DIM_ATT)
        q = q_ref[:, sl]
        kk = jnp.concatenate([k0_ref[:, sl], k1_ref[:, sl], k2_ref[:, sl]], axis=0)
        vv = jnp.concatenate([v0_ref[:, sl], v1_ref[:, sl], v2_ref[:, sl]], axis=0)
        s = _dot_nt(q, kk) * scale + bias_ref[h]
        mx = jnp.max(s, axis=1, keepdims=True)
        p = jnp.exp(s - mx).astype(BF16)
        od = _dot(p, jnp.concatenate([vv, ones], axis=1))
        o = od[:, :HEAD_DIM_ATT] / od[:, HEAD_DIM_ATT:HEAD_DIM_ATT + 1]
        o_ref[:, sl] = o.astype(o_ref.dtype)


def _attn_prompt(z, bias):
    def kv_spec(col, back):
        return pl.BlockSpec((ATT_Q, D_ATT), lambda j: (jnp.maximum(j - back, 0), col))

    return pl.pallas_call(
        _attn_prompt_kernel,
        grid=(SEQ // ATT_Q,),
        in_specs=[
            pl.BlockSpec((ATT_Q, D_ATT), lambda j: (j, COL_Q)),
            kv_spec(COL_K, 2), kv_spec(COL_K, 1), kv_spec(COL_K, 0),
            kv_spec(COL_V, 2), kv_spec(COL_V, 1), kv_spec(COL_V, 0),
            pl.BlockSpec((None, N_HEADS_ATT, ATT_Q, ATT_K), lambda j: (jnp.minimum(j, ATT_VARIANTS - 1), 0, 0, 0)),
        ],
        out_specs=pl.BlockSpec((ATT_Q, D_ATT), lambda j: (j, 0)),
        out_shape=jax.ShapeDtypeStruct((SEQ, D_ATT), BF16),
        compiler_params=_params(("parallel",)),
        name="attn_prompt",
    )(z, z, z, z, z, z, z, bias)


def _attn_step_kernel(q_ref, k_ref, v_ref, ck_ref, cv_ref, bp_ref, bn_ref, o_ref):
    scale = HEAD_DIM_ATT ** -0.5
    ck = ck_ref[...].astype(BF16)
    cv = cv_ref[...].astype(BF16)
    for h in range(N_HEADS_ATT):
        sl = slice(h * HEAD_DIM_ATT, (h + 1) * HEAD_DIM_ATT)
        q = q_ref[:, sl]
        s_past = _dot_nt(q, ck) * scale + bp_ref[h]
        s_new = _dot_nt(q, k_ref[:, sl]) * scale + bn_ref[h]
        mx = jnp.maximum(jnp.max(s_past, axis=1, keepdims=True), jnp.max(s_new, axis=1, keepdims=True))
        p_past = jnp.exp(s_past - mx)
        p_new = jnp.exp(s_new - mx)
        den = jnp.sum(p_past, axis=1, keepdims=True) + jnp.sum(p_new, axis=1, keepdims=True)
        o = _dot(p_past.astype(BF16), cv) + _dot(p_new.astype(BF16), v_ref[:, sl])
        o_ref[:, sl] = (o / den).astype(o_ref.dtype)


def _attn_step(z, cache_k, cache_v, bias_past, bias_new, layer):
    first = SEQ // DEC_SEQ
    past_rows = cache_k.shape[2]

    def z_spec(col):
        return pl.BlockSpec((DEC_SEQ, D_ATT), lambda b: (first + b, col))

    cache_spec = pl.BlockSpec((None, None, past_rows, HEAD_DIM_ATT), lambda b: (layer, b, 0, 0))
    return pl.pallas_call(
        _attn_step_kernel,
        grid=(DEC_BATCH,),
        in_specs=[
            z_spec(COL_Q), z_spec(COL_K), z_spec(COL_V), cache_spec, cache_spec,
            _const_spec((N_HEADS_ATT, DEC_SEQ, past_rows)),
            _const_spec((N_HEADS_ATT, DEC_SEQ, DEC_SEQ)),
        ],
        out_specs=pl.BlockSpec((DEC_SEQ, D_ATT), lambda b: (b, 0)),
        out_shape=jax.ShapeDtypeStruct((N_SAMPLE, D_ATT), BF16),
        compiler_params=_params(("parallel",)),
        name="attn_step",
    )(z, z, z, cache_k, cache_v, bias_past, bias_new)


def _mlstm_block(q, k, v, om, gates, gnorm, c0, n0, m0):
    li_col, lf_col, li_row, lf_row = gates
    L = q.shape[0]
    kscale = HEAD_DIM_MLSTM ** -0.5
    t_idx = lax.broadcasted_iota(jnp.int32, (L, L), 0)
    s_idx = lax.broadcasted_iota(jnp.int32, (L, L), 1)
    causal = s_idx <= t_idx
    b_col = jnp.sum(jnp.where(causal, lf_row, 0.0), axis=1, keepdims=True)
    b_row = jnp.sum(jnp.where(t_idx <= s_idx, lf_col, 0.0), axis=0, keepdims=True)
    d = jnp.where(causal, b_col - b_row + li_row, NEG)
    inter = b_col + m0
    m_col = jnp.maximum(inter, jnp.max(d, axis=1, keepdims=True))
    w = jnp.exp(d - (m_col - math.log(kscale)))
    sc = jnp.exp(inter - m_col)
    qk = _dot_nt(q, k) * w
    num = sc * _dot_nt(q, c0.astype(BF16)) + _dot(qk.astype(BF16), v)
    qn = _dot_nt(q, jnp.broadcast_to(n0, (BF16_ROWS, HEAD_DIM_MLSTM)).astype(BF16))[:, 0:1]
    den = sc * qn + jnp.sum(qk, axis=1, keepdims=True)
    hh = num * (1.0 / jnp.maximum(jnp.abs(den), jnp.exp(-m_col)))
    mu = jnp.mean(hh, axis=1, keepdims=True)
    cen = hh - mu
    var = jnp.mean(cen * cen, axis=1, keepdims=True)
    y = _sigmoid(om.astype(F32)) * (cen * lax.rsqrt(var + EPS) * gnorm)
    m_last = m_col[L - 1:L, :]
    b_last = b_col[L - 1:L, :]
    decay = jnp.exp(b_last + m0 - m_last)
    ws = jnp.exp(b_last - b_col + li_col - m_last) * kscale
    vs = (v.astype(F32) * ws).astype(BF16)
    c1 = decay * c0 + _dot_tn(vs, k)
    n1 = decay * n0 + jnp.sum(k.astype(F32) * ws, axis=0, keepdims=True)
    return y, c1, n1, m_last


def _log_gates(gif):
    lane = lax.broadcasted_iota(jnp.int32, gif.shape, 1)
    lg = jnp.where(lane < N_HEADS_MLSTM, gif, _log_sigmoid(gif))
    return lg, lg.T


def _gate_views(log_gates, head):
    lg, lg_t = log_gates
    f = N_HEADS_MLSTM + head
    return lg[:, head:head + 1], lg[:, f:f + 1], lg_t[head:head + 1, :], lg_t[f:f + 1, :]


def _mlstm_prompt_kernel(q_ref, k_ref, v_ref, om_ref, gif_ref, gn_ref, y_ref, c_out, n_out, m_out,
                         c_scr, n_scr, m_scr):
    step = pl.program_id(0)

    @pl.when(step == 0)
    def _():
        c_scr[...] = jnp.zeros_like(c_scr)
        n_scr[...] = jnp.zeros_like(n_scr)
        m_scr[...] = jnp.zeros_like(m_scr)

    log_gates = _log_gates(gif_ref[...])
    for h in range(N_HEADS_MLSTM):
        sl = slice(h * HEAD_DIM_MLSTM, (h + 1) * HEAD_DIM_MLSTM)
        y, c1, n1, m1 = _mlstm_block(q_ref[:, sl], k_ref[:, sl], v_ref[:, sl], om_ref[:, sl],
                                     _gate_views(log_gates, h), gn_ref[:, sl],
                                     c_scr[h], n_scr[h], m_scr[h][:, :1])
        y_ref[:, sl] = y.astype(y_ref.dtype)
        c_scr[h] = c1
        n_scr[h] = n1
        m_scr[h] = jnp.broadcast_to(m1, (1, LANES))

    @pl.when(step == pl.num_programs(0) - 1)
    def _():
        c_out[...] = c_scr[...]
        n_out[...] = n_scr[...]
        m_out[...] = m_scr[...]


def _mlstm_prompt(z, gif, gnorm):
    def z_spec(col):
        return pl.BlockSpec((MLSTM_L, D_MLSTM), lambda c: (c, col))

    state_shapes = [
        jax.ShapeDtypeStruct((N_HEADS_MLSTM, HEAD_DIM_MLSTM, HEAD_DIM_MLSTM), F32),
        jax.ShapeDtypeStruct((N_HEADS_MLSTM, 1, HEAD_DIM_MLSTM), F32),
        jax.ShapeDtypeStruct((N_HEADS_MLSTM, 1, LANES), F32),
    ]
    return pl.pallas_call(
        _mlstm_prompt_kernel,
        grid=(SEQ // MLSTM_L,),
        in_specs=[
            z_spec(COL_QM), z_spec(COL_KM), z_spec(COL_VM), z_spec(COL_OM),
            pl.BlockSpec((MLSTM_L, LANES), lambda c: (c, 0)),
            _const_spec((1, D_MLSTM)),
        ],
        out_specs=[pl.BlockSpec((MLSTM_L, D_MLSTM), lambda c: (c, 0))]
        + [_const_spec(s.shape) for s in state_shapes],
        out_shape=[jax.ShapeDtypeStruct((SEQ, D_MLSTM), BF16)] + state_shapes,
        scratch_shapes=[pltpu.VMEM(s.shape, F32) for s in state_shapes],
        compiler_params=_params(("arbitrary",)),
        name="mlstm_prompt",
    )(z, z, z, z, gif, gnorm)


def _mlstm_step_kernel(q_ref, k_ref, v_ref, om_ref, gif_ref, gn_ref, c_ref, n_ref, m_ref,
                       y_ref, c_out, n_out, m_out):
    log_gates = _log_gates(gif_ref[...])
    for h in range(N_HEADS_MLSTM):
        sl = slice(h * HEAD_DIM_MLSTM, (h + 1) * HEAD_DIM_MLSTM)
        y, c1, n1, m1 = _mlstm_block(q_ref[:, sl], k_ref[:, sl], v_ref[:, sl], om_ref[:, sl],
                                     _gate_views(log_gates, h), gn_ref[:, sl],
                                     c_ref[h], n_ref[h], m_ref[h][:, :1])
        y_ref[:, sl] = y.astype(y_ref.dtype)
        c_out[h] = c1
        n_out[h] = n1
        m_out[h] = jnp.broadcast_to(m1, (1, LANES))


def _mlstm_step(z, gif, gnorm, state_c, state_n, state_m, layer):
    first = SEQ // DEC_SEQ

    def z_spec(col):
        return pl.BlockSpec((DEC_SEQ, D_MLSTM), lambda b: (first + b, col))

    def st_in(shape):
        return pl.BlockSpec((None, None) + shape, lambda b: (layer, b) + (0,) * len(shape))

    def st_out(shape):
        return pl.BlockSpec((None,) + shape, lambda b: (b,) + (0,) * len(shape))

    shapes = [(N_HEADS_MLSTM, HEAD_DIM_MLSTM, HEAD_DIM_MLSTM), (N_HEADS_MLSTM, 1, HEAD_DIM_MLSTM),
              (N_HEADS_MLSTM, 1, LANES)]
    return pl.pallas_call(
        _mlstm_step_kernel,
        grid=(DEC_BATCH,),
        in_specs=[
            z_spec(COL_QM), z_spec(COL_KM), z_spec(COL_VM), z_spec(COL_OM),
            pl.BlockSpec((DEC_SEQ, LANES), lambda b: (first + b, 0)),
            _const_spec((1, D_MLSTM)),
        ] + [st_in(s) for s in shapes],
        out_specs=[pl.BlockSpec((DEC_SEQ, D_MLSTM), lambda b: (b, 0))] + [st_out(s) for s in shapes],
        out_shape=[jax.ShapeDtypeStruct((N_SAMPLE, D_MLSTM), BF16)]
        + [jax.ShapeDtypeStruct((DEC_BATCH,) + s, F32) for s in shapes],
        compiler_params=_params(("parallel",)),
        name="mlstm_step",
    )(z, z, z, z, gif, gnorm, state_c, state_n, state_m)


HALO = 16
GROUPS_PER_TILE = ROW_TILE // DEC_SEQ


def _merge_kernel(xa_ref, gb_ref, gc_ref, xah_ref, gch_ref, s1_ref, s2_ref, cw_ref,
                  ybp_ref, ybs_ref, ycp_ref, ycs_ref, zg_ref, x_ref,
                  wpc_ref, wpa_ref, wpm_ref, wout_ref, x_out, tail_out, u_scr):
    i = pl.program_id(0)
    is_s = i >= N_PROMPT_TILES
    row = lax.broadcasted_iota(jnp.int32, (ROW_TILE, 1), 0)
    pos = jnp.where(is_s, row % DEC_SEQ, row)
    u = gc_ref[...].astype(F32) * xa_ref[...].astype(F32)
    u_halo = gch_ref[...].astype(F32) * xah_ref[...].astype(F32)
    keep = jnp.logical_and(i > 0, jnp.logical_not(is_s))
    h1 = jnp.where(keep, u_halo[HALO - 1:HALO, :], 0.0)
    h2 = jnp.where(keep, u_halo[HALO - 2:HALO - 1, :], 0.0)
    f1 = jnp.where(is_s, s1_ref[...], h1)
    f2 = jnp.where(is_s, s2_ref[...], jnp.where(row == 0, h2, h1))
    u_m1 = jnp.where(pos >= 1, pltpu.roll(u, 1, 0), f1)
    u_m2 = jnp.where(pos >= 2, pltpu.roll(u, 2, 0), f2)
    cw = cw_ref[...]
    y = cw[0:1, :] * u_m2 + cw[1:2, :] * u_m1 + cw[2:3, :] * u
    ya = gb_ref[...].astype(F32) * y

    u_scr[...] = u
    for g in range(GROUPS_PER_TILE):
        for j in range(CONV_W - 1):
            src = (g + 1) * DEC_SEQ - (CONV_W - 1) + j
            tail_out[j, g:g + 1, :] = u_scr[src:src + 1, :]

    yb = jnp.where(is_s, ybs_ref[...], ybp_ref[...])
    yc = jnp.where(is_s, ycs_ref[...], ycp_ref[...])
    g = _sigmoid(zg_ref[...].astype(F32))
    merged = (g[:, 0:D_MODEL] * _dot(ya.astype(BF16), wpc_ref[...])
              + g[:, D_MODEL:2 * D_MODEL] * _dot(yb, wpa_ref[...])
              + g[:, 2 * D_MODEL:3 * D_MODEL] * _dot(yc, wpm_ref[...]))
    x_out[...] = x_ref[...] + _dot(merged.astype(BF16), wout_ref[...])


def _merge(z, zg, s1, s2, conv_w, yb_p, yb_s, yc_p, yc_s, x, wpc, wpa, wpm, wout):
    halo_blocks = ROW_TILE // HALO

    def z_spec(col):
        return pl.BlockSpec((ROW_TILE, D_CONV), lambda i: (i, col))

    def halo_spec(col):
        return pl.BlockSpec((HALO, D_CONV), lambda i: (jnp.maximum(i * halo_blocks - 1, 0), col))

    def prompt_spec(width):
        return pl.BlockSpec((ROW_TILE, width), lambda i: (jnp.minimum(i, N_PROMPT_TILES - 1), 0))

    def weight_spec(shape):
        return pl.BlockSpec(shape, lambda i: (0, 0), pipeline_mode=pl.Buffered(1))

    return pl.pallas_call(
        _merge_kernel,
        grid=(N_TILES,),
        in_specs=[
            z_spec(COL_XA), z_spec(COL_GB), z_spec(COL_GC), halo_spec(COL_XA), halo_spec(COL_GC),
            _const_spec((ROW_TILE, D_CONV)), _const_spec((ROW_TILE, D_CONV)), _const_spec((CONV_W, D_CONV)),
            prompt_spec(D_ATT), _const_spec((ROW_TILE, D_ATT)),
            prompt_spec(D_MLSTM), _const_spec((ROW_TILE, D_MLSTM)),
            _row_spec(N_BRANCH * D_MODEL), _row_spec(D_MODEL),
            weight_spec((D_CONV, D_MODEL)), weight_spec((D_ATT, D_MODEL)), weight_spec((D_MLSTM, D_MODEL)),
            weight_spec((D_MODEL, D_MODEL)),
        ],
        out_specs=[
            _row_spec(D_MODEL),
            pl.BlockSpec((None, CONV_W - 1, GROUPS_PER_TILE, D_CONV), lambda i: (i, 0, 0, 0)),
        ],
        out_shape=[
            jax.ShapeDtypeStruct((N_ROWS, D_MODEL), F32),
            jax.ShapeDtypeStruct((N_TILES, CONV_W - 1, GROUPS_PER_TILE, D_CONV), F32),
        ],
        scratch_shapes=[pltpu.VMEM((ROW_TILE, D_CONV), F32)],
        compiler_params=_params(("parallel",)),
        name="merge",
    )(z, z, z, z, z, s1, s2, conv_w, yb_p, yb_s, yc_p, yc_s, zg, x, wpc, wpa, wpm, wout)


def _granule_copies(idx_ref, first, n, src_hbm, buf, sem, slot):
    return [
        pltpu.make_async_copy(
            src_hbm.at[pl.ds(pl.multiple_of(idx_ref[first + k] * GRAN, GRAN), GRAN), :],
            buf.at[slot, pl.ds(k * GRAN, GRAN), :],
            sem.at[slot])
        for k in range(n)
    ]


GATHER_AHEAD = 2
GATHER_SLOTS = GATHER_AHEAD + 1


def _prefetched_gather(idx_ref, n, src_hbm, buf, sem, first_group=0):
    step = pl.program_id(0)
    n_steps = pl.num_programs(0)

    def request(ahead):
        slot = (step + ahead) % GATHER_SLOTS
        for cp in _granule_copies(idx_ref, (first_group + step + ahead) * n, n, src_hbm, buf, sem, slot):
            cp.start()

    for ahead in range(GATHER_AHEAD):
        pl.when(jnp.logical_and(step == 0, ahead < n_steps))(functools.partial(request, ahead))
    pl.when(step + GATHER_AHEAD < n_steps)(functools.partial(request, GATHER_AHEAD))

    slot = step % GATHER_SLOTS
    for cp in _granule_copies(idx_ref, (first_group + step) * n, n, src_hbm, buf, sem, slot):
        cp.wait()
    return slot


def _expert_ffn_kernel(src_ref, expert_ref, first_ref, valid_ref, next_ref, xs_hbm, wg_hbm, wu_hbm, wd_hbm,
                       o_ref, buf, sem, wg_st, wu_st, wd_st, wsem, wg_bf, wu_bf, wd_bf, *, layer):
    c = pl.program_id(0)
    slot = _prefetched_gather(src_ref, CHUNK_GRANS, xs_hbm, buf, sem)

    def weight_copies(e):
        return [pltpu.make_async_copy(wg_hbm.at[layer, e], wg_st, wsem.at[0]),
                pltpu.make_async_copy(wu_hbm.at[layer, e], wu_st, wsem.at[1]),
                pltpu.make_async_copy(wd_hbm.at[layer, e], wd_st, wsem.at[2])]

    @pl.when(c == 0)
    def _():
        for cp in weight_copies(expert_ref[0]):
            cp.start()

    @pl.when(first_ref[c] == 1)
    def _():
        for cp in weight_copies(expert_ref[c]):
            cp.wait()
        wg_bf[...] = wg_st[...].astype(BF16)
        wu_bf[...] = wu_st[...].astype(BF16)
        wd_bf[...] = wd_st[...].astype(BF16)

        @pl.when(next_ref[c] >= 0)
        def _():
            for cp in weight_copies(next_ref[c]):
                cp.start()

    @pl.when(valid_ref[c] == 1)
    def _():
        rows = buf[slot]
        x = rows[:, :D_MODEL]
        meta = rows[:, D_MODEL:].astype(F32)
        w_first = meta[:, 1:2] + meta[:, 2:3] + meta[:, 3:4]
        w_second = meta[:, 5:6] + meta[:, 6:7] + meta[:, 7:8]
        w = jnp.where(meta[:, 0:1] == expert_ref[c].astype(F32), w_first, w_second)
        gate = _dot(x, wg_bf[...])
        a = gate * _sigmoid(gate) * _dot(x, wu_bf[...]) * w
        o_ref[...] = _dot(a.astype(BF16), wd_bf[...]).astype(o_ref.dtype)

    @pl.when(valid_ref[c] == 0)
    def _():
        o_ref[...] = jnp.zeros_like(o_ref)


def _expert_ffn(src, chunk_expert, chunk_first, chunk_valid, chunk_next, xs, w_gate, w_up, w_down, layer):
    any_spec = pl.BlockSpec(memory_space=pl.ANY)
    grid_spec = pltpu.PrefetchScalarGridSpec(
        num_scalar_prefetch=5,
        grid=(MAX_CHUNKS,),
        in_specs=[any_spec, any_spec, any_spec, any_spec],
        out_specs=pl.BlockSpec((CHUNK_ROWS, D_MODEL), lambda c, *_: (c, 0)),
        scratch_shapes=[
            pltpu.VMEM((GATHER_SLOTS, CHUNK_ROWS, XS_WIDTH), BF16),
            pltpu.SemaphoreType.DMA((GATHER_SLOTS,)),
            pltpu.VMEM((D_MODEL, D_EXPERT), F32),
            pltpu.VMEM((D_MODEL, D_EXPERT), F32),
            pltpu.VMEM((D_EXPERT, D_MODEL), F32),
            pltpu.SemaphoreType.DMA((3,)),
            pltpu.VMEM((D_MODEL, D_EXPERT), BF16),
            pltpu.VMEM((D_MODEL, D_EXPERT), BF16),
            pltpu.VMEM((D_EXPERT, D_MODEL), BF16),
        ],
    )
    return pl.pallas_call(
        functools.partial(_expert_ffn_kernel, layer=layer),
        grid_spec=grid_spec,
        out_shape=jax.ShapeDtypeStruct((MAX_CHUNKS * CHUNK_ROWS, D_MODEL), BF16),
        compiler_params=_params(("arbitrary",)),
        name="expert_ffn",
    )(src, chunk_expert, chunk_first, chunk_valid, chunk_next, xs, w_gate, w_up, w_down)


def _combine(dst_ref, ys_hbm, lp_ref, x_ref, buf, sem, first_tile=0):
    slot = _prefetched_gather(dst_ref, TILE_GRANS, ys_hbm, buf, sem, first_tile)
    lp = lp_ref[...]
    p_idx = lax.broadcasted_iota(jnp.int32, (ROW_TILE, LOCAL_ROWS), 1).astype(F32)
    pick = jnp.where((p_idx == lp[:, 0:1]) | (p_idx == lp[:, 1:2]), 1.0, 0.0).astype(BF16)
    return x_ref[...] + _dot(pick, buf[slot])


def _combine_norm_kernel(dst_ref, ys_hbm, lp_ref, x_ref, g_ref, whi_ref, wlo_ref, b_ref,
                         x_out, h_out, s_out, buf, sem):
    x = _combine(dst_ref, ys_hbm, lp_ref, x_ref, buf, sem)
    x_out[...] = x
    h = _rms(x, g_ref[...])
    h_out[...] = h.astype(BF16)
    s_out[...] = _small_proj(h, whi_ref, wlo_ref, b_ref)


def _combine_final_kernel(dst_ref, ys_hbm, lp_ref, x_ref, g_ref, o_ref, buf, sem, *, first_tile):
    o_ref[...] = _rms(_combine(dst_ref, ys_hbm, lp_ref, x_ref, buf, sem, first_tile), g_ref[...])


_COMBINE_SCRATCH = [pltpu.VMEM((GATHER_SLOTS, LOCAL_ROWS, D_MODEL), BF16),
                    pltpu.SemaphoreType.DMA((GATHER_SLOTS,))]


def _combine_norm(dst, ys, lp, x, g, whi, wlo, b):
    def row(width):
        return pl.BlockSpec((ROW_TILE, width), lambda i, dst: (i, 0))

    def const(shape):
        return pl.BlockSpec(shape, lambda i, dst: (0,) * len(shape))

    grid_spec = pltpu.PrefetchScalarGridSpec(
        num_scalar_prefetch=1,
        grid=(N_TILES,),
        in_specs=[pl.BlockSpec(memory_space=pl.ANY), row(LANES), row(D_MODEL), const((1, D_MODEL)),
                  const((D_MODEL, LANES)), const((D_MODEL, LANES)), const((1, LANES))],
        out_specs=[row(D_MODEL), row(D_MODEL), row(LANES)],
        scratch_shapes=_COMBINE_SCRATCH,
    )
    return pl.pallas_call(
        _combine_norm_kernel,
        grid_spec=grid_spec,
        out_shape=[
            jax.ShapeDtypeStruct((N_ROWS, D_MODEL), F32),
            jax.ShapeDtypeStruct((N_ROWS, D_MODEL), BF16),
            jax.ShapeDtypeStruct((N_ROWS, LANES), F32),
        ],
        compiler_params=_params(("arbitrary",)),
        name="combine_norm",
    )(dst, ys, lp, x, g, whi, wlo, b)


def _combine_final(dst, ys, lp, x, g, first_tile, n_tiles):
    def row(width):
        return pl.BlockSpec((ROW_TILE, width), lambda i, dst: (i + first_tile, 0))

    grid_spec = pltpu.PrefetchScalarGridSpec(
        num_scalar_prefetch=1,
        grid=(n_tiles,),
        in_specs=[pl.BlockSpec(memory_space=pl.ANY), row(LANES), row(D_MODEL),
                  pl.BlockSpec((1, D_MODEL), lambda i, dst: (0, 0))],
        out_specs=pl.BlockSpec((ROW_TILE, D_MODEL), lambda i, dst: (i, 0)),
        scratch_shapes=_COMBINE_SCRATCH,
    )
    return pl.pallas_call(
        functools.partial(_combine_final_kernel, first_tile=first_tile),
        grid_spec=grid_spec,
        out_shape=jax.ShapeDtypeStruct((n_tiles * ROW_TILE, D_MODEL), F32),
        compiler_params=_params(("arbitrary",)),
        name="combine_final",
    )(dst, ys, lp, x, g)


N_SRC = MAX_CHUNKS * CHUNK_GRANS
N_DST = N_TILES * TILE_GRANS


def _granule_kernel(gcnt_ref, src_ref, dst_ref, expert_ref, first_ref, valid_ref, next_ref, local_ref):
    def fill(ref, n, value):
        def body(k, carry):
            ref[k] = value
            return carry
        lax.fori_loop(0, n, body, 0)

    fill(src_ref, N_SRC, TILE_GRANS - 1)
    fill(dst_ref, N_DST, 0)
    fill(expert_ref, MAX_CHUNKS, N_EXPERTS - 1)
    fill(first_ref, MAX_CHUNKS, 0)
    fill(valid_ref, MAX_CHUNKS, 0)
    fill(next_ref, MAX_CHUNKS + 1, -1)
    fill(local_ref, N_TILES, 0)

    def per_expert(e, carry):
        chunk, prev_first = carry
        slot0 = chunk * CHUNK_GRANS

        def per_tile(t, pos):
            n = gcnt_ref[t * N_EXPERTS + e]
            local0 = t * TILE_GRANS + local_ref[t]

            def per_granule(k, c2):
                src_ref[pos + k] = local0 + k
                dst_ref[local0 + k] = pos + k
                return c2

            lax.fori_loop(0, n, per_granule, 0)
            local_ref[t] = local_ref[t] + n
            return pos + n

        end = lax.fori_loop(0, N_TILES, per_tile, slot0)
        n_chunks = (end - slot0 + CHUNK_GRANS - 1) // CHUNK_GRANS

        def per_chunk(k, c2):
            expert_ref[chunk + k] = e
            valid_ref[chunk + k] = 1
            return c2

        lax.fori_loop(0, n_chunks, per_chunk, 0)
        owns = n_chunks > 0

        @pl.when(owns)
        def _():
            first_ref[chunk] = 1
            next_ref[prev_first] = e

        return chunk + n_chunks, jnp.where(owns, chunk, prev_first)

    lax.fori_loop(0, N_EXPERTS, per_expert, (0, MAX_CHUNKS))


def _granule_lists(gcnt):
    smem = pl.BlockSpec(memory_space=pltpu.SMEM)
    sizes = (N_SRC, N_DST, MAX_CHUNKS, MAX_CHUNKS, MAX_CHUNKS, MAX_CHUNKS + 1)
    return pl.pallas_call(
        _granule_kernel,
        in_specs=[smem],
        out_specs=[smem] * len(sizes),
        out_shape=[jax.ShapeDtypeStruct((n,), jnp.int32) for n in sizes],
        scratch_shapes=[pltpu.SMEM((N_TILES,), jnp.int32)],
        name="granule_lists",
    )(gcnt.reshape(N_TILES * N_EXPERTS))


def _pad_lanes(w):
    return jnp.pad(w, [(0, 0)] * (w.ndim - 1) + [(0, LANES - w.shape[-1])])


def _toeplitz_bias(table, n_q, n_k, reach):
    period = 1
    while period < n_q + n_k:
        period *= 2
    j = jnp.arange(period)
    d = jnp.where(j < n_k, j, j - period)
    u = table[:, jnp.clip(reach - d, -REL_CLIP, REL_CLIP) + REL_CLIP]
    flat = jnp.tile(u, (1, n_q))[:, :n_q * (period - 1)]
    return flat.reshape(table.shape[0], n_q, period - 1)[:, :, :n_k]


def _prompt_bias(table):
    bias = _toeplitz_bias(table, ATT_Q, ATT_K, ATT_REACH)
    qc = jnp.arange(ATT_Q)[:, None] // CHUNK
    m = jnp.arange(ATT_K)[None, :]
    kc = m // CHUNK
    band = (kc >= qc) & (kc <= qc + BAND_CHUNKS)
    first_key = ATT_REACH - ATT_Q * jnp.arange(ATT_VARIANTS)
    ok = band[None] & (m[None] >= first_key[:, None, None])
    return jnp.where(ok[:, None], bias[None], NEG)


def _step_bias(table, past):
    bias = _toeplitz_bias(table, DEC_SEQ, past + DEC_SEQ, past)
    own_head = jnp.eye(N_HEADS_ATT, dtype=bool)[:, None, None, :]
    bias_past = jnp.where(own_head, bias[:, :, :past, None], NEG)
    return bias_past.reshape(N_HEADS_ATT, DEC_SEQ, past * N_HEADS_ATT), bias[:, :, past:]


def kernel(x_prompt, x_sample, state_conv, cache_k, cache_v, state_C, state_n, state_m, norm_mix, norm_ffn, w_in, conv_w, rel_bias, gate_bias, mlstm_norm, w_proj_conv, w_proj_att, w_proj_mlstm, w_out, router_group, router_group_bias, router_expert, router_expert_bias, w_gate, w_up, w_down, norm_final):
    past = cache_k.shape[2]
    cache_k2 = cache_k.reshape(DEPTH, DEC_BATCH, past * N_HEADS_ATT, HEAD_DIM_ATT)
    cache_v2 = cache_v.reshape(DEPTH, DEC_BATCH, past * N_HEADS_ATT, HEAD_DIM_ATT)
    xp = x_prompt.reshape(SEQ, D_MODEL)
    xs = x_sample.reshape(N_SAMPLE, D_MODEL)
    state_n5 = state_n.reshape(DEPTH, DEC_BATCH, N_HEADS_MLSTM, 1, HEAD_DIM_MLSTM)
    state_m5 = jnp.broadcast_to(state_m[..., None, None], (DEPTH, DEC_BATCH, N_HEADS_MLSTM, 1, LANES))

    w_in_t = jnp.swapaxes(w_in, 1, 2)

    x_mid = moe = None
    outs = {k: [] for k in ("p_conv", "p_k", "p_v", "p_C", "p_n", "p_m", "s_conv", "s_k", "s_v", "s_C", "s_n", "s_m")}
    for l in range(DEPTH):
        gif_hi, gif_lo = _split_hi_lo(_pad_lanes(w_in_t[l, MAIN_WIDTH:MAIN_WIDTH + GIF_WIDTH, :].T))
        gif_b = _pad_lanes(gate_bias[l][None, :])
        g_mix = norm_mix[l][None, :]
        if l == 0:
            x, h, gif = _norm_first(xp, xs, g_mix, gif_hi, gif_lo, gif_b)
        else:
            x, h, gif = _combine_norm(*moe, x_mid, g_mix, gif_hi, gif_lo, gif_b)
        z = _project(h, w_in_t, l, 0, MAIN_WIDTH)
        zg = _project(h, w_in_t, l, MAIN_WIDTH + GIF_WIDTH, N_BRANCH * D_MODEL)

        yb_p = _attn_prompt(z, _prompt_bias(rel_bias[l]))
        bias_past, bias_new = _step_bias(rel_bias[l], past)
        yb_s = _attn_step(z, cache_k2, cache_v2, bias_past, bias_new, l)

        gnorm = mlstm_norm[l][None, :]
        yc_p, p_c, p_n, p_m = _mlstm_prompt(z, gif, gnorm)
        yc_s, s_c, s_n, s_m = _mlstm_step(z, gif, gnorm, state_C, state_n5, state_m5, l)

        prev = state_conv[l]
        s1 = jnp.zeros((DEC_BATCH, DEC_SEQ, D_CONV), F32).at[:, 0].set(prev[:, 1]).reshape(N_SAMPLE, D_CONV)
        s2 = (jnp.zeros((DEC_BATCH, DEC_SEQ, D_CONV), F32).at[:, 0].set(prev[:, 0]).at[:, 1].set(prev[:, 1])
              .reshape(N_SAMPLE, D_CONV))
        x_mid, tails = _merge(z, zg, s1, s2, conv_w[l], yb_p, yb_s, yc_p, yc_s, x,
                              w_proj_conv[l].astype(BF16), w_proj_att[l].astype(BF16),
                              w_proj_mlstm[l].astype(BF16), w_out[l].astype(BF16))

        r_w = _pad_lanes(jnp.concatenate([router_group[l], router_expert[l]], axis=1))
        r_hi, r_lo = _split_hi_lo(r_w)
        r_b = _pad_lanes(jnp.concatenate([router_group_bias[l], router_expert_bias[l]])[None, :])
        xs_local, lp, gcnt = _route_dispatch(x_mid, norm_ffn[l][None, :], r_hi, r_lo, r_b)
        gcnt = gcnt[:, 0, N_GROUPS:N_GROUPS + N_EXPERTS].astype(jnp.int32)
        src, dst, chunk_expert, chunk_first, chunk_valid, chunk_next = _granule_lists(gcnt)
        ys = _expert_ffn(src, chunk_expert, chunk_first, chunk_valid, chunk_next, xs_local, w_gate, w_up, w_down, l)
        moe = (dst, ys, lp)

        keep = min(ATT_REACH, SEQ)
        k_rows = z[SEQ - keep:, COL_K * D_ATT:(COL_K + 1) * D_ATT].astype(F32)
        v_rows = z[SEQ - keep:, COL_V * D_ATT:(COL_V + 1) * D_ATT].astype(F32)
        outs["p_conv"].append(tails[N_PROMPT_TILES - 1, :, GROUPS_PER_TILE - 1][None])
        outs["p_k"].append(k_rows[:keep].reshape(1, keep, N_HEADS_ATT, HEAD_DIM_ATT))
        outs["p_v"].append(v_rows[:keep].reshape(1, keep, N_HEADS_ATT, HEAD_DIM_ATT))
        outs["p_C"].append(p_c[None])
        outs["p_n"].append(p_n[:, 0][None])
        outs["p_m"].append(p_m[:, 0, 0][None])
        outs["s_conv"].append(jnp.swapaxes(tails[N_PROMPT_TILES], 0, 1))
        outs["s_k"].append(k_rows[keep:].reshape(DEC_BATCH, DEC_SEQ, N_HEADS_ATT, HEAD_DIM_ATT))
        outs["s_v"].append(v_rows[keep:].reshape(DEC_BATCH, DEC_SEQ, N_HEADS_ATT, HEAD_DIM_ATT))
        outs["s_C"].append(s_c)
        outs["s_n"].append(s_n[:, :, 0])
        outs["s_m"].append(s_m[:, :, 0, 0])

    g_fin = norm_final[None, :]
    y_prompt = _combine_final(*moe, x_mid, g_fin, 0, N_PROMPT_TILES).reshape(x_prompt.shape)
    y_sample = _combine_final(*moe, x_mid, g_fin, N_PROMPT_TILES, 1).reshape(x_sample.shape)
    st = {k: jnp.stack(v) for k, v in outs.items()}
    return (y_prompt, y_sample, st["p_conv"], st["p_k"], st["p_v"], st["p_C"], st["p_n"], st["p_m"],
            st["s_conv"], st["s_k"], st["s_v"], st["s_C"], st["s_n"], st["s_m"])
```

```python
import functools
import math

import jax
import jax.numpy as jnp
from jax import lax
from jax.experimental import pallas as pl
from jax.experimental.pallas import tpu as pltpu

D_MODEL = 2048
SEQ = 8192
DEPTH = 2
DEC_BATCH = 8
DEC_SEQ = 32
N_SAMPLE = DEC_BATCH * DEC_SEQ
N_ROWS = SEQ + N_SAMPLE

CHUNK = 64
D_CONV = 1024
CONV_W = 3
N_HEADS_ATT = 8
HEAD_DIM_ATT = 128
D_ATT = N_HEADS_ATT * HEAD_DIM_ATT
BAND_CHUNKS = 8
ATT_REACH = BAND_CHUNKS * CHUNK
REL_CLIP = 128
N_HEADS_MLSTM = 4
HEAD_DIM_MLSTM = 256
D_MLSTM = N_HEADS_MLSTM * HEAD_DIM_MLSTM
N_BRANCH = 3
MAIN_WIDTH = 3 * D_CONV + 3 * D_ATT + 4 * D_MLSTM
GIF_WIDTH = 2 * N_HEADS_MLSTM
N_GROUPS = 4
EXPERTS_PER_GROUP = 4
N_EXPERTS = N_GROUPS * EXPERTS_PER_GROUP
D_EXPERT = 512
EPS = 1e-6

LANES = 128
BF16_ROWS = 16
ROW_TILE = 256
N_TILES = N_ROWS // ROW_TILE
N_PROMPT_TILES = SEQ // ROW_TILE
MM_ROWS = 1056
MM_COLS = 1024
ATT_Q = 256
ATT_K = ATT_Q + ATT_REACH
ATT_VARIANTS = ATT_REACH // ATT_Q + 1
MLSTM_L = 256
NEG = -1e30
VMEM_LIMIT = 56 * 1024 * 1024

GRAN = BF16_ROWS
XS_WIDTH = D_MODEL + LANES
MAX_RUN_PAD = N_EXPERTS * (GRAN - 1)
TILE_GRANS = (2 * ROW_TILE + MAX_RUN_PAD) // GRAN + 1
LOCAL_ROWS = TILE_GRANS * GRAN
CHUNK_GRANS = 16
CHUNK_ROWS = CHUNK_GRANS * GRAN
MAX_CHUNKS = (N_TILES * (TILE_GRANS - 1)) // CHUNK_GRANS + N_EXPERTS

COL_XA, COL_GB, COL_GC, COL_Q, COL_K, COL_V, COL_QM, COL_KM, COL_VM, COL_OM = range(10)

F32 = jnp.float32
BF16 = jnp.bfloat16


def _params(sem):
    return pltpu.CompilerParams(dimension_semantics=sem, vmem_limit_bytes=VMEM_LIMIT)


def _dot(a, b):
    return jnp.dot(a, b, preferred_element_type=F32)


def _dot_nt(a, b):
    return lax.dot_general(a, b, (((1,), (1,)), ((), ())), preferred_element_type=F32)


def _dot_tn(a, b):
    return lax.dot_general(a, b, (((0,), (0,)), ((), ())), preferred_element_type=F32)


def _split_hi_lo(w):
    hi = w.astype(BF16)
    lo = (w - hi.astype(F32)).astype(BF16)
    return hi, lo


def _sigmoid(x):
    return 1.0 / (1.0 + jnp.exp(-x))


def _log_sigmoid(x):
    return jnp.minimum(x, 0.0) - jnp.log1p(jnp.exp(-jnp.abs(x)))


def _rms(x, g):
    ms = jnp.mean(x * x, axis=-1, keepdims=True)
    return x * lax.rsqrt(ms + EPS) * g


def _small_proj(h, whi_ref, wlo_ref, b_ref):
    h_hi = h.astype(BF16)
    h_lo = (h - h_hi.astype(F32)).astype(BF16)
    whi = whi_ref[...]
    return _dot(h_hi, whi) + _dot(h_lo, whi) + _dot(h_hi, wlo_ref[...]) + b_ref[...]


def _norm_first_kernel(xp_ref, xs_ref, g_ref, whi_ref, wlo_ref, b_ref, x_out, h_out, s_out):
    i = pl.program_id(0)
    x = jnp.where(i < N_PROMPT_TILES, xp_ref[...], xs_ref[...])
    x_out[...] = x
    h = _rms(x, g_ref[...])
    h_out[...] = h.astype(BF16)
    s_out[...] = _small_proj(h, whi_ref, wlo_ref, b_ref)


def _route(logits):
    lane = lax.broadcasted_iota(jnp.int32, logits.shape, 1)
    is_g = lane < N_GROUPS
    gl = jnp.where(is_g, logits, NEG)
    gmax = jnp.max(gl, axis=1, keepdims=True)
    g_sel = jnp.min(jnp.where(is_g & (gl == gmax), lane, LANES), axis=1, keepdims=True)
    p_g = 1.0 / jnp.sum(jnp.where(is_g, jnp.exp(gl - gmax), 0.0), axis=1, keepdims=True)
    e_lane = lane - N_GROUPS
    in_g = (e_lane >= 0) & (e_lane < N_EXPERTS) & ((e_lane // EXPERTS_PER_GROUP) == g_sel)
    e1 = jnp.max(jnp.where(in_g, logits, NEG), axis=1, keepdims=True)
    i1 = jnp.min(jnp.where(in_g & (logits == e1), lane, LANES), axis=1, keepdims=True)
    rest = in_g & (lane != i1)
    e2 = jnp.max(jnp.where(rest, logits, NEG), axis=1, keepdims=True)
    i2 = jnp.min(jnp.where(rest & (logits == e2), lane, LANES), axis=1, keepdims=True)
    r = jnp.exp(e2 - e1)
    w1 = p_g / (1.0 + r)
    w2 = w1 * r
    return i1, i2, w1, w2


def _bf16_pieces(w):
    a = w.astype(BF16).astype(F32)
    b = (w - a).astype(BF16).astype(F32)
    return a, b, w - a - b


def _route_dispatch_kernel(x_ref, g_ref, whi_ref, wlo_ref, b_ref, xs_out, lp_out, cnt_out):
    h = _rms(x_ref[...], g_ref[...])
    i1, i2, w1, w2 = _route(_small_proj(h, whi_ref, wlo_ref, b_ref))
    lane = lax.broadcasted_iota(jnp.int32, (ROW_TILE, LANES), 1)
    o1 = lane == i1
    o2 = lane == i2
    onehot = jnp.where(o1 | o2, 1.0, 0.0)
    t_idx = lax.broadcasted_iota(jnp.int32, (ROW_TILE, ROW_TILE), 0)
    s_idx = lax.broadcasted_iota(jnp.int32, (ROW_TILE, ROW_TILE), 1)
    earlier = jnp.where(s_idx < t_idx, 1.0, 0.0).astype(BF16)
    rank = _dot(earlier, onehot.astype(BF16))
    gcnt = jnp.floor((jnp.sum(onehot, axis=0, keepdims=True) + (GRAN - 1)) * (1.0 / GRAN))
    a_idx = lax.broadcasted_iota(jnp.int32, (LANES, LANES), 0)
    b_idx = lax.broadcasted_iota(jnp.int32, (LANES, LANES), 1)
    before = jnp.where(a_idx < b_idx, 1.0, 0.0).astype(BF16)
    gcnt8 = jnp.broadcast_to(gcnt, (8, LANES))
    run_start = _dot(gcnt8.astype(BF16), before)[0:1, :] * GRAN
    pos = run_start + rank
    lpos1 = jnp.sum(jnp.where(o1, pos, 0.0), axis=1, keepdims=True)
    lpos2 = jnp.sum(jnp.where(o2, pos, 0.0), axis=1, keepdims=True)
    lp = jnp.where(lane == 0, lpos1, jnp.where(lane == 1, lpos2, -1.0))
    lp_out[...] = lp
    cnt_out[...] = gcnt8

    meta = jnp.zeros((ROW_TILE, LANES), F32)
    fields = ((i1 - N_GROUPS).astype(F32),) + _bf16_pieces(w1) + ((i2 - N_GROUPS).astype(F32),) + _bf16_pieces(w2)
    for k, val in enumerate(fields):
        meta = jnp.where(lane == k, val, meta)
    rows = jnp.concatenate([h.astype(BF16), meta.astype(BF16)], axis=1)
    lp_t = lp.T
    p_idx = lax.broadcasted_iota(jnp.int32, (LOCAL_ROWS, ROW_TILE), 0).astype(F32)
    perm = jnp.where((p_idx == lp_t[0:1, :]) | (p_idx == lp_t[1:2, :]), 1.0, 0.0).astype(BF16)
    xs_out[...] = _dot(perm, rows).astype(BF16)


def _row_spec(width):
    return pl.BlockSpec((ROW_TILE, width), lambda i: (i, 0))


def _const_spec(shape):
    return pl.BlockSpec(shape, lambda i: (0,) * len(shape))


def _norm_first(xp, xs, g, whi, wlo, b):
    return pl.pallas_call(
        _norm_first_kernel,
        grid=(N_TILES,),
        in_specs=[
            pl.BlockSpec((ROW_TILE, D_MODEL), lambda i: (jnp.minimum(i, N_PROMPT_TILES - 1), 0)),
            _const_spec((ROW_TILE, D_MODEL)),
            _const_spec((1, D_MODEL)),
            _const_spec((D_MODEL, LANES)),
            _const_spec((D_MODEL, LANES)),
            _const_spec((1, LANES)),
        ],
        out_specs=[_row_spec(D_MODEL), _row_spec(D_MODEL), _row_spec(LANES)],
        out_shape=[
            jax.ShapeDtypeStruct((N_ROWS, D_MODEL), F32),
            jax.ShapeDtypeStruct((N_ROWS, D_MODEL), BF16),
            jax.ShapeDtypeStruct((N_ROWS, LANES), F32),
        ],
        compiler_params=_params(("parallel",)),
        name="norm_first",
    )(xp, xs, g, whi, wlo, b)


def _route_dispatch(x, g, whi, wlo, b):
    return pl.pallas_call(
        _route_dispatch_kernel,
        grid=(N_TILES,),
        in_specs=[
            _row_spec(D_MODEL),
            _const_spec((1, D_MODEL)),
            _const_spec((D_MODEL, LANES)),
            _const_spec((D_MODEL, LANES)),
            _const_spec((1, LANES)),
        ],
        out_specs=[
            pl.BlockSpec((LOCAL_ROWS, XS_WIDTH), lambda i: (i, 0)),
            _row_spec(LANES),
            pl.BlockSpec((None, 8, LANES), lambda i: (i, 0, 0)),
        ],
        out_shape=[
            jax.ShapeDtypeStruct((N_TILES * LOCAL_ROWS, XS_WIDTH), BF16),
            jax.ShapeDtypeStruct((N_ROWS, LANES), F32),
            jax.ShapeDtypeStruct((N_TILES, 8, LANES), F32),
        ],
        compiler_params=_params(("parallel",)),
        name="route_dispatch",
    )(x, g, whi, wlo, b)


def _mm_kernel(h_ref, wt_ref, o_ref, wb_ref):
    @pl.when(pl.program_id(1) == 0)
    def _():
        wb_ref[...] = wt_ref[0].T.astype(BF16)

    o_ref[...] = _dot(h_ref[...], wb_ref[...]).astype(o_ref.dtype)


def _project(h, wt, layer, row0, n_cols):
    return pl.pallas_call(
        _mm_kernel,
        grid=(n_cols // MM_COLS, N_ROWS // MM_ROWS),
        in_specs=[
            pl.BlockSpec((MM_ROWS, D_MODEL), lambda j, i: (i, 0)),
            pl.BlockSpec((pl.Element(1), pl.Element(MM_COLS), pl.Element(D_MODEL)),
                         lambda j, i: (layer, pl.multiple_of(row0 + j * MM_COLS, 8), 0)),
        ],
        out_specs=pl.BlockSpec((MM_ROWS, MM_COLS), lambda j, i: (i, j)),
        out_shape=jax.ShapeDtypeStruct((N_ROWS, n_cols), BF16),
        scratch_shapes=[pltpu.VMEM((D_MODEL, MM_COLS), BF16)],
        compiler_params=_params(("parallel", "arbitrary")),
        name="project",
    )(h, wt)


def _attn_prompt_head(h, q_ref, k_refs, v_refs, bias_ref, o_ref):
    sl = slice(h * HEAD_DIM_ATT, (h + 1) * HEAD_DIM_ATT)
    q = q_ref[:, sl]
    kk = jnp.concatenate([r[:, sl] for r in k_refs], axis=0)
    vv = jnp.concatenate([r[:, sl] for r in v_refs], axis=0)
    s = _dot_nt(q, kk) * (HEAD_DIM_ATT ** -0.5) + bias_ref[h]
    mx = jnp.max(s, axis=1, keepdims=True)
    p = jnp.exp(s - mx).astype(BF16)
    od = _dot(p, jnp.concatenate([vv, jnp.ones((ATT_K, HEAD_DIM_ATT), BF16)], axis=1))
    o = od[:, :HEAD_DIM_ATT] * (1.0 / od[:, HEAD_DIM_ATT:HEAD_DIM_ATT + 1])
    o_ref[:, sl] = o.astype(o_ref.dtype)


def _attn_step_kernel(q_ref, k_ref, v_ref, ck_ref, cv_ref, bp_ref, bn_ref, o_ref):
    scale = HEAD_DIM_ATT ** -0.5
    for h in range(N_HEADS_ATT):
        sl = slice(h * HEAD_DIM_ATT, (h + 1) * HEAD_DIM_ATT)
        q = q_ref[:, sl]
        s_past = _dot_nt(q, ck_ref[:, h, :].astype(BF16)) * scale + bp_ref[h]
        s_new = _dot_nt(q, k_ref[:, sl]) * scale + bn_ref[h]
        mx = jnp.maximum(jnp.max(s_past, axis=1, keepdims=True), jnp.max(s_new, axis=1, keepdims=True))
        p_past = jnp.exp(s_past - mx)
        p_new = jnp.exp(s_new - mx)
        den = jnp.sum(p_past, axis=1, keepdims=True) + jnp.sum(p_new, axis=1, keepdims=True)
        o = _dot(p_past.astype(BF16), cv_ref[:, h, :].astype(BF16)) + _dot(p_new.astype(BF16), v_ref[:, sl])
        o_ref[:, sl] = (o / den).astype(o_ref.dtype)


def _attn_step(z, cache_k, cache_v, bias_past, bias_new, layer):
    first = SEQ // DEC_SEQ
    past = cache_k.shape[2]

    def z_spec(col):
        return pl.BlockSpec((DEC_SEQ, D_ATT), lambda b: (first + b, col))

    cache_spec = pl.BlockSpec((None, None, past, N_HEADS_ATT, HEAD_DIM_ATT), lambda b: (layer, b, 0, 0, 0))
    return pl.pallas_call(
        _attn_step_kernel,
        grid=(DEC_BATCH,),
        in_specs=[
            z_spec(COL_Q), z_spec(COL_K), z_spec(COL_V), cache_spec, cache_spec,
            _const_spec((N_HEADS_ATT, DEC_SEQ, past)),
            _const_spec((N_HEADS_ATT, DEC_SEQ, DEC_SEQ)),
        ],
        out_specs=pl.BlockSpec((DEC_SEQ, D_ATT), lambda b: (b, 0)),
        out_shape=jax.ShapeDtypeStruct((N_SAMPLE, D_ATT), BF16),
        compiler_params=_params(("parallel",)),
        name="attn_step",
    )(z, z, z, cache_k, cache_v, bias_past, bias_new)


def _mlstm_block(q, k, v, om, gates, gnorm, c0, n0, m0):
    li_col, lf_col, li_row, lf_row = gates
    L = q.shape[0]
    kscale = HEAD_DIM_MLSTM ** -0.5
    t_idx = lax.broadcasted_iota(jnp.int32, (L, L), 0)
    s_idx = lax.broadcasted_iota(jnp.int32, (L, L), 1)
    causal = s_idx <= t_idx
    b_col = jnp.sum(jnp.where(causal, lf_row, 0.0), axis=1, keepdims=True)
    b_row = jnp.sum(jnp.where(t_idx <= s_idx, lf_col, 0.0), axis=0, keepdims=True)
    d = jnp.where(causal, b_col - b_row + li_row, NEG)
    inter = b_col + m0
    m_col = jnp.maximum(inter, jnp.max(d, axis=1, keepdims=True))
    w = jnp.exp(d - (m_col - math.log(kscale)))
    sc = jnp.exp(inter - m_col)
    qk = _dot_nt(q, k) * w
    num = sc * _dot_nt(q, c0.astype(BF16)) + _dot(qk.astype(BF16), v)
    qn = _dot_nt(q, jnp.broadcast_to(n0, (BF16_ROWS, HEAD_DIM_MLSTM)).astype(BF16))[:, 0:1]
    den = sc * qn + jnp.sum(qk, axis=1, keepdims=True)
    hh = num * (1.0 / jnp.maximum(jnp.abs(den), jnp.exp(-m_col)))
    mu = jnp.mean(hh, axis=1, keepdims=True)
    cen = hh - mu
    var = jnp.mean(cen * cen, axis=1, keepdims=True)
    y = _sigmoid(om.astype(F32)) * (cen * lax.rsqrt(var + EPS) * gnorm)
    m_last = m_col[L - 1:L, :]
    b_last = b_col[L - 1:L, :]
    decay = jnp.exp(b_last + m0 - m_last)
    ws = jnp.exp(b_last - b_col + li_col - m_last) * kscale
    vs = (v.astype(F32) * ws).astype(BF16)
    c1 = decay * c0 + _dot_tn(vs, k)
    n1 = decay * n0 + jnp.sum(k.astype(F32) * ws, axis=0, keepdims=True)
    return y, c1, n1, m_last


def _log_gates(gif):
    lane = lax.broadcasted_iota(jnp.int32, gif.shape, 1)
    lg = jnp.where(lane < N_HEADS_MLSTM, gif, _log_sigmoid(gif))
    return lg, lg.T


def _gate_views(log_gates, head):
    lg, lg_t = log_gates
    f = N_HEADS_MLSTM + head
    return lg[:, head:head + 1], lg[:, f:f + 1], lg_t[head:head + 1, :], lg_t[f:f + 1, :]


def _mixers_prompt_kernel(q_ref, k0_ref, k1_ref, k2_ref, v0_ref, v1_ref, v2_ref, bias_ref,
                          qm_ref, km_ref, vm_ref, om_ref, gif_ref, gn_ref,
                          yb_ref, yc_ref, c_out, n_out, m_out, c_scr, n_scr, m_scr):
    step = pl.program_id(0)

    @pl.when(step == 0)
    def _():
        c_scr[...] = jnp.zeros_like(c_scr)
        n_scr[...] = jnp.zeros_like(n_scr)
        m_scr[...] = jnp.zeros_like(m_scr)

    log_gates = _log_gates(gif_ref[...])
    att_per_mlstm = N_HEADS_ATT // N_HEADS_MLSTM
    for h in range(N_HEADS_MLSTM):
        sl = slice(h * HEAD_DIM_MLSTM, (h + 1) * HEAD_DIM_MLSTM)
        y, c1, n1, m1 = _mlstm_block(qm_ref[:, sl], km_ref[:, sl], vm_ref[:, sl], om_ref[:, sl],
                                     _gate_views(log_gates, h), gn_ref[:, sl],
                                     c_scr[h], n_scr[h], m_scr[h][:, :1])
        yc_ref[:, sl] = y.astype(yc_ref.dtype)
        c_scr[h] = c1
        n_scr[h] = n1
        m_scr[h] = jnp.broadcast_to(m1, (1, LANES))
        for ha in range(h * att_per_mlstm, (h + 1) * att_per_mlstm):
            _attn_prompt_head(ha, q_ref, (k0_ref, k1_ref, k2_ref), (v0_ref, v1_ref, v2_ref), bias_ref, yb_ref)

    @pl.when(step == pl.num_programs(0) - 1)
    def _():
        c_out[...] = c_scr[...]
        n_out[...] = n_scr[...]
        m_out[...] = m_scr[...]


def _mixers_prompt(z, bias, gif, gnorm):
    assert ATT_Q == MLSTM_L

    def z_spec(col):
        return pl.BlockSpec((ATT_Q, D_ATT), lambda j: (j, col))

    def kv_spec(col, back):
        return pl.BlockSpec((ATT_Q, D_ATT), lambda j: (jnp.maximum(j - back, 0), col))

    state_shapes = [
        jax.ShapeDtypeStruct((N_HEADS_MLSTM, HEAD_DIM_MLSTM, HEAD_DIM_MLSTM), F32),
        jax.ShapeDtypeStruct((N_HEADS_MLSTM, 1, HEAD_DIM_MLSTM), F32),
        jax.ShapeDtypeStruct((N_HEADS_MLSTM, 1, LANES), F32),
    ]
    return pl.pallas_call(
        _mixers_prompt_kernel,
        grid=(SEQ // ATT_Q,),
        in_specs=[
            z_spec(COL_Q),
            kv_spec(COL_K, 2), kv_spec(COL_K, 1), kv_spec(COL_K, 0),
            kv_spec(COL_V, 2), kv_spec(COL_V, 1), kv_spec(COL_V, 0),
            pl.BlockSpec((None, N_HEADS_ATT, ATT_Q, ATT_K), lambda j: (jnp.minimum(j, ATT_VARIANTS - 1), 0, 0, 0)),
            z_spec(COL_QM), z_spec(COL_KM), z_spec(COL_VM), z_spec(COL_OM),
            pl.BlockSpec((MLSTM_L, LANES), lambda j: (j, 0)),
            _const_spec((1, D_MLSTM)),
        ],
        out_specs=[pl.BlockSpec((ATT_Q, D_ATT), lambda j: (j, 0)), pl.BlockSpec((MLSTM_L, D_MLSTM), lambda j: (j, 0))]
        + [_const_spec(s.shape) for s in state_shapes],
        out_shape=[jax.ShapeDtypeStruct((SEQ, D_ATT), BF16), jax.ShapeDtypeStruct((SEQ, D_MLSTM), BF16)]
        + state_shapes,
        scratch_shapes=[pltpu.VMEM(s.shape, F32) for s in state_shapes],
        compiler_params=_params(("arbitrary",)),
        name="mixers_prompt",
    )(z, z, z, z, z, z, z, bias, z, z, z, z, gif, gnorm)


def _mlstm_step_kernel(q_ref, k_ref, v_ref, om_ref, gif_ref, gn_ref, c_ref, n_ref, m_ref,
                       y_ref, c_out, n_out, m_out):
    log_gates = _log_gates(gif_ref[...])
    for h in range(N_HEADS_MLSTM):
        sl = slice(h * HEAD_DIM_MLSTM, (h + 1) * HEAD_DIM_MLSTM)
        y, c1, n1, m1 = _mlstm_block(q_ref[:, sl], k_ref[:, sl], v_ref[:, sl], om_ref[:, sl],
                                     _gate_views(log_gates, h), gn_ref[:, sl],
                                     c_ref[h], n_ref[h], m_ref[h][:, :1])
        y_ref[:, sl] = y.astype(y_ref.dtype)
        c_out[h] = c1
        n_out[h] = n1
        m_out[h] = jnp.broadcast_to(m1, (1, LANES))


def _mlstm_step(z, gif, gnorm, state_c, state_n, state_m, layer):
    first = SEQ // DEC_SEQ

    def z_spec(col):
        return pl.BlockSpec((DEC_SEQ, D_MLSTM), lambda b: (first + b, col))

    def st_in(shape):
        return pl.BlockSpec((None, None) + shape, lambda b: (layer, b) + (0,) * len(shape))

    def st_out(shape):
        return pl.BlockSpec((None,) + shape, lambda b: (b,) + (0,) * len(shape))

    shapes = [(N_HEADS_MLSTM, HEAD_DIM_MLSTM, HEAD_DIM_MLSTM), (N_HEADS_MLSTM, 1, HEAD_DIM_MLSTM),
              (N_HEADS_MLSTM, 1, LANES)]
    return pl.pallas_call(
        _mlstm_step_kernel,
        grid=(DEC_BATCH,),
        in_specs=[
            z_spec(COL_QM), z_spec(COL_KM), z_spec(COL_VM), z_spec(COL_OM),
            pl.BlockSpec((DEC_SEQ, LANES), lambda b: (first + b, 0)),
            _const_spec((1, D_MLSTM)),
        ] + [st_in(s) for s in shapes],
        out_specs=[pl.BlockSpec((DEC_SEQ, D_MLSTM), lambda b: (b, 0))] + [st_out(s) for s in shapes],
        out_shape=[jax.ShapeDtypeStruct((N_SAMPLE, D_MLSTM), BF16)]
        + [jax.ShapeDtypeStruct((DEC_BATCH,) + s, F32) for s in shapes],
        compiler_params=_params(("parallel",)),
        name="mlstm_step",
    )(z, z, z, z, gif, gnorm, state_c, state_n, state_m)


HALO = 16
GROUPS_PER_TILE = ROW_TILE // DEC_SEQ


def _merge_kernel(xa_ref, gb_ref, gc_ref, xah_ref, gch_ref, s1_ref, s2_ref, cw_ref,
                  ybp_ref, ybs_ref, ycp_ref, ycs_ref, zg_ref, x_ref,
                  wpc_ref, wpa_ref, wpm_ref, wout_ref, x_out, tail_out, u_scr):
    i = pl.program_id(0)
    is_s = i >= N_PROMPT_TILES
    row = lax.broadcasted_iota(jnp.int32, (ROW_TILE, 1), 0)
    pos = jnp.where(is_s, row % DEC_SEQ, row)
    u = gc_ref[...].astype(F32) * xa_ref[...].astype(F32)
    u_halo = gch_ref[...].astype(F32) * xah_ref[...].astype(F32)
    keep = jnp.logical_and(i > 0, jnp.logical_not(is_s))
    h1 = jnp.where(keep, u_halo[HALO - 1:HALO, :], 0.0)
    h2 = jnp.where(keep, u_halo[HALO - 2:HALO - 1, :], 0.0)
    f1 = jnp.where(is_s, s1_ref[...], h1)
    f2 = jnp.where(is_s, s2_ref[...], jnp.where(row == 0, h2, h1))
    u_m1 = jnp.where(pos >= 1, pltpu.roll(u, 1, 0), f1)
    u_m2 = jnp.where(pos >= 2, pltpu.roll(u, 2, 0), f2)
    cw = cw_ref[...]
    y = cw[0:1, :] * u_m2 + cw[1:2, :] * u_m1 + cw[2:3, :] * u
    ya = gb_ref[...].astype(F32) * y

    u_scr[...] = u
    for g in range(GROUPS_PER_TILE):
        for j in range(CONV_W - 1):
            src = (g + 1) * DEC_SEQ - (CONV_W - 1) + j
            tail_out[j, g:g + 1, :] = u_scr[src:src + 1, :]

    yb = jnp.where(is_s, ybs_ref[...], ybp_ref[...])
    yc = jnp.where(is_s, ycs_ref[...], ycp_ref[...])
    g = _sigmoid(zg_ref[...].astype(F32))
    merged = (g[:, 0:D_MODEL] * _dot(ya.astype(BF16), wpc_ref[...])
              + g[:, D_MODEL:2 * D_MODEL] * _dot(yb, wpa_ref[...])
              + g[:, 2 * D_MODEL:3 * D_MODEL] * _dot(yc, wpm_ref[...]))
    x_out[...] = x_ref[...] + _dot(merged.astype(BF16), wout_ref[...])


def _merge(z, zg, s1, s2, conv_w, yb_p, yb_s, yc_p, yc_s, x, wpc, wpa, wpm, wout):
    halo_blocks = ROW_TILE // HALO

    def z_spec(col):
        return pl.BlockSpec((ROW_TILE, D_CONV), lambda i: (i, col))

    def halo_spec(col):
        return pl.BlockSpec((HALO, D_CONV), lambda i: (jnp.maximum(i * halo_blocks - 1, 0), col))

    def prompt_spec(width):
        return pl.BlockSpec((ROW_TILE, width), lambda i: (jnp.minimum(i, N_PROMPT_TILES - 1), 0))

    def weight_spec(shape):
        return pl.BlockSpec(shape, lambda i: (0, 0), pipeline_mode=pl.Buffered(1))

    return pl.pallas_call(
        _merge_kernel,
        grid=(N_TILES,),
        in_specs=[
            z_spec(COL_XA), z_spec(COL_GB), z_spec(COL_GC), halo_spec(COL_XA), halo_spec(COL_GC),
            _const_spec((ROW_TILE, D_CONV)), _const_spec((ROW_TILE, D_CONV)), _const_spec((CONV_W, D_CONV)),
            prompt_spec(D_ATT), _const_spec((ROW_TILE, D_ATT)),
            prompt_spec(D_MLSTM), _const_spec((ROW_TILE, D_MLSTM)),
            _row_spec(N_BRANCH * D_MODEL), _row_spec(D_MODEL),
            weight_spec((D_CONV, D_MODEL)), weight_spec((D_ATT, D_MODEL)), weight_spec((D_MLSTM, D_MODEL)),
            weight_spec((D_MODEL, D_MODEL)),
        ],
        out_specs=[
            _row_spec(D_MODEL),
            pl.BlockSpec((None, CONV_W - 1, GROUPS_PER_TILE, D_CONV), lambda i: (i, 0, 0, 0)),
        ],
        out_shape=[
            jax.ShapeDtypeStruct((N_ROWS, D_MODEL), F32),
            jax.ShapeDtypeStruct((N_TILES, CONV_W - 1, GROUPS_PER_TILE, D_CONV), F32),
        ],
        scratch_shapes=[pltpu.VMEM((ROW_TILE, D_CONV), F32)],
        compiler_params=_params(("parallel",)),
        name="merge",
    )(z, z, z, z, z, s1, s2, conv_w, yb_p, yb_s, yc_p, yc_s, zg, x, wpc, wpa, wpm, wout)


def _granule_copies(idx_ref, first, n, src_hbm, buf, sem, slot):
    return [
        pltpu.make_async_copy(
            src_hbm.at[pl.ds(pl.multiple_of(idx_ref[first + k] * GRAN, GRAN), GRAN), :],
            buf.at[slot, pl.ds(k * GRAN, GRAN), :],
            sem.at[slot])
        for k in range(n)
    ]


GATHER_AHEAD = 2
GATHER_SLOTS = GATHER_AHEAD + 1


def _prefetched_gather(idx_ref, n, src_hbm, buf, sem, first_group=0):
    step = pl.program_id(0)
    n_steps = pl.num_programs(0)

    def request(ahead):
        slot = (step + ahead) % GATHER_SLOTS
        for cp in _granule_copies(idx_ref, (first_group + step + ahead) * n, n, src_hbm, buf, sem, slot):
            cp.start()

    for ahead in range(GATHER_AHEAD):
        pl.when(jnp.logical_and(step == 0, ahead < n_steps))(functools.partial(request, ahead))
    pl.when(step + GATHER_AHEAD < n_steps)(functools.partial(request, GATHER_AHEAD))

    slot = step % GATHER_SLOTS
    for cp in _granule_copies(idx_ref, (first_group + step) * n, n, src_hbm, buf, sem, slot):
        cp.wait()
    return slot


def _expert_ffn_kernel(src_ref, expert_ref, first_ref, valid_ref, next_ref, xs_hbm, wg_hbm, wu_hbm, wd_hbm,
                       o_ref, buf, sem, wg_st, wu_st, wd_st, wsem, wg_bf, wu_bf, wd_bf, *, layer):
    c = pl.program_id(0)
    slot = _prefetched_gather(src_ref, CHUNK_GRANS, xs_hbm, buf, sem)

    def weight_copies(e):
        return [pltpu.make_async_copy(wg_hbm.at[layer, e], wg_st, wsem.at[0]),
                pltpu.make_async_copy(wu_hbm.at[layer, e], wu_st, wsem.at[1]),
                pltpu.make_async_copy(wd_hbm.at[layer, e], wd_st, wsem.at[2])]

    @pl.when(c == 0)
    def _():
        for cp in weight_copies(expert_ref[0]):
            cp.start()

    @pl.when(first_ref[c] == 1)
    def _():
        for cp in weight_copies(expert_ref[c]):
            cp.wait()
        wg_bf[...] = wg_st[...].astype(BF16)
        wu_bf[...] = wu_st[...].astype(BF16)
        wd_bf[...] = wd_st[...].astype(BF16)

        @pl.when(next_ref[c] >= 0)
        def _():
            for cp in weight_copies(next_ref[c]):
                cp.start()

    @pl.when(valid_ref[c] == 1)
    def _():
        rows = buf[slot]
        x = rows[:, :D_MODEL]
        meta = rows[:, D_MODEL:].astype(F32)
        w_first = meta[:, 1:2] + meta[:, 2:3] + meta[:, 3:4]
        w_second = meta[:, 5:6] + meta[:, 6:7] + meta[:, 7:8]
        w = jnp.where(meta[:, 0:1] == expert_ref[c].astype(F32), w_first, w_second)
        gate = _dot(x, wg_bf[...])
        a = gate * _sigmoid(gate) * _dot(x, wu_bf[...]) * w
        o_ref[...] = _dot(a.astype(BF16), wd_bf[...]).astype(o_ref.dtype)

    @pl.when(valid_ref[c] == 0)
    def _():
        o_ref[...] = jnp.zeros_like(o_ref)


def _expert_ffn(src, chunk_expert, chunk_first, chunk_valid, chunk_next, xs, w_gate, w_up, w_down, layer):
    any_spec = pl.BlockSpec(memory_space=pl.ANY)
    grid_spec = pltpu.PrefetchScalarGridSpec(
        num_scalar_prefetch=5,
        grid=(MAX_CHUNKS,),
        in_specs=[any_spec, any_spec, any_spec, any_spec],
        out_specs=pl.BlockSpec((CHUNK_ROWS, D_MODEL), lambda c, *_: (c, 0)),
        scratch_shapes=[
            pltpu.VMEM((GATHER_SLOTS, CHUNK_ROWS, XS_WIDTH), BF16),
            pltpu.SemaphoreType.DMA((GATHER_SLOTS,)),
            pltpu.VMEM((D_MODEL, D_EXPERT), F32),
            pltpu.VMEM((D_MODEL, D_EXPERT), F32),
            pltpu.VMEM((D_EXPERT, D_MODEL), F32),
            pltpu.SemaphoreType.DMA((3,)),
            pltpu.VMEM((D_MODEL, D_EXPERT), BF16),
            pltpu.VMEM((D_MODEL, D_EXPERT), BF16),
            pltpu.VMEM((D_EXPERT, D_MODEL), BF16),
        ],
    )
    return pl.pallas_call(
        functools.partial(_expert_ffn_kernel, layer=layer),
        grid_spec=grid_spec,
        out_shape=jax.ShapeDtypeStruct((MAX_CHUNKS * CHUNK_ROWS, D_MODEL), BF16),
        compiler_params=_params(("arbitrary",)),
        name="expert_ffn",
    )(src, chunk_expert, chunk_first, chunk_valid, chunk_next, xs, w_gate, w_up, w_down)


def _combine(dst_ref, ys_hbm, lp_ref, x_ref, buf, sem, first_tile=0):
    slot = _prefetched_gather(dst_ref, TILE_GRANS, ys_hbm, buf, sem, first_tile)
    lp = lp_ref[...]
    p_idx = lax.broadcasted_iota(jnp.int32, (ROW_TILE, LOCAL_ROWS), 1).astype(F32)
    pick = jnp.where((p_idx == lp[:, 0:1]) | (p_idx == lp[:, 1:2]), 1.0, 0.0).astype(BF16)
    return x_ref[...] + _dot(pick, buf[slot])


def _combine_norm_kernel(dst_ref, ys_hbm, lp_ref, x_ref, g_ref, whi_ref, wlo_ref, b_ref,
                         x_out, h_out, s_out, buf, sem):
    x = _combine(dst_ref, ys_hbm, lp_ref, x_ref, buf, sem)
    x_out[...] = x
    h = _rms(x, g_ref[...])
    h_out[...] = h.astype(BF16)
    s_out[...] = _small_proj(h, whi_ref, wlo_ref, b_ref)


def _combine_final_kernel(dst_ref, ys_hbm, lp_ref, x_ref, g_ref, o_ref, buf, sem, *, first_tile):
    o_ref[...] = _rms(_combine(dst_ref, ys_hbm, lp_ref, x_ref, buf, sem, first_tile), g_ref[...])


_COMBINE_SCRATCH = [pltpu.VMEM((GATHER_SLOTS, LOCAL_ROWS, D_MODEL), BF16),
                    pltpu.SemaphoreType.DMA((GATHER_SLOTS,))]


def _combine_norm(dst, ys, lp, x, g, whi, wlo, b):
    def row(width):
        return pl.BlockSpec((ROW_TILE, width), lambda i, dst: (i, 0))

    def const(shape):
        return pl.BlockSpec(shape, lambda i, dst: (0,) * len(shape))

    grid_spec = pltpu.PrefetchScalarGridSpec(
        num_scalar_prefetch=1,
        grid=(N_TILES,),
        in_specs=[pl.BlockSpec(memory_space=pl.ANY), row(LANES), row(D_MODEL), const((1, D_MODEL)),
                  const((D_MODEL, LANES)), const((D_MODEL, LANES)), const((1, LANES))],
        out_specs=[row(D_MODEL), row(D_MODEL), row(LANES)],
        scratch_shapes=_COMBINE_SCRATCH,
    )
    return pl.pallas_call(
        _combine_norm_kernel,
        grid_spec=grid_spec,
        out_shape=[
            jax.ShapeDtypeStruct((N_ROWS, D_MODEL), F32),
            jax.ShapeDtypeStruct((N_ROWS, D_MODEL), BF16),
            jax.ShapeDtypeStruct((N_ROWS, LANES), F32),
        ],
        compiler_params=_params(("arbitrary",)),
        name="combine_norm",
    )(dst, ys, lp, x, g, whi, wlo, b)


def _combine_final(dst, ys, lp, x, g, first_tile, n_tiles):
    def row(width):
        return pl.BlockSpec((ROW_TILE, width), lambda i, dst: (i + first_tile, 0))

    grid_spec = pltpu.PrefetchScalarGridSpec(
        num_scalar_prefetch=1,
        grid=(n_tiles,),
        in_specs=[pl.BlockSpec(memory_space=pl.ANY), row(LANES), row(D_MODEL),
                  pl.BlockSpec((1, D_MODEL), lambda i, dst: (0, 0))],
        out_specs=pl.BlockSpec((ROW_TILE, D_MODEL), lambda i, dst: (i, 0)),
        scratch_shapes=_COMBINE_SCRATCH,
    )
    return pl.pallas_call(
        functools.partial(_combine_final_kernel, first_tile=first_tile),
        grid_spec=grid_spec,
        out_shape=jax.ShapeDtypeStruct((n_tiles * ROW_TILE, D_MODEL), F32),
        compiler_params=_params(("arbitrary",)),
        name="combine_final",
    )(dst, ys, lp, x, g)


N_SRC = MAX_CHUNKS * CHUNK_GRANS
N_DST = N_TILES * TILE_GRANS


def _granule_kernel(gcnt_ref, src_ref, dst_ref, expert_ref, first_ref, valid_ref, next_ref, local_ref):
    def fill(ref, n, value):
        def body(k, carry):
            ref[k] = value
            return carry
        lax.fori_loop(0, n, body, 0)

    fill(src_ref, N_SRC, TILE_GRANS - 1)
    fill(dst_ref, N_DST, 0)
    fill(expert_ref, MAX_CHUNKS, N_EXPERTS - 1)
    fill(first_ref, MAX_CHUNKS, 0)
    fill(valid_ref, MAX_CHUNKS, 0)
    fill(next_ref, MAX_CHUNKS + 1, -1)
    fill(local_ref, N_TILES, 0)

    def per_expert(e, carry):
        chunk, prev_first = carry
        slot0 = chunk * CHUNK_GRANS

        def per_tile(t, pos):
            n = gcnt_ref[t * N_EXPERTS + e]
            local0 = t * TILE_GRANS + local_ref[t]

            def per_granule(k, c2):
                src_ref[pos + k] = local0 + k
                dst_ref[local0 + k] = pos + k
                return c2

            lax.fori_loop(0, n, per_granule, 0)
            local_ref[t] = local_ref[t] + n
            return pos + n

        end = lax.fori_loop(0, N_TILES, per_tile, slot0)
        n_chunks = (end - slot0 + CHUNK_GRANS - 1) // CHUNK_GRANS

        def per_chunk(k, c2):
            expert_ref[chunk + k] = e
            valid_ref[chunk + k] = 1
            return c2

        lax.fori_loop(0, n_chunks, per_chunk, 0)
        owns = n_chunks > 0

        @pl.when(owns)
        def _():
            first_ref[chunk] = 1
            next_ref[prev_first] = e

        return chunk + n_chunks, jnp.where(owns, chunk, prev_first)

    lax.fori_loop(0, N_EXPERTS, per_expert, (0, MAX_CHUNKS))


def _granule_lists(gcnt):
    smem = pl.BlockSpec(memory_space=pltpu.SMEM)
    sizes = (N_SRC, N_DST, MAX_CHUNKS, MAX_CHUNKS, MAX_CHUNKS, MAX_CHUNKS + 1)
    return pl.pallas_call(
        _granule_kernel,
        in_specs=[smem],
        out_specs=[smem] * len(sizes),
        out_shape=[jax.ShapeDtypeStruct((n,), jnp.int32) for n in sizes],
        scratch_shapes=[pltpu.SMEM((N_TILES,), jnp.int32)],
        name="granule_lists",
    )(gcnt.reshape(N_TILES * N_EXPERTS))


def _pad_lanes(w):
    return jnp.pad(w, [(0, 0)] * (w.ndim - 1) + [(0, LANES - w.shape[-1])])


def _toeplitz_bias(table, n_q, n_k, reach):
    period = 1
    while period < n_q + n_k:
        period *= 2
    j = jnp.arange(period)
    d = jnp.where(j < n_k, j, j - period)
    u = table[:, jnp.clip(reach - d, -REL_CLIP, REL_CLIP) + REL_CLIP]
    flat = jnp.tile(u, (1, n_q))[:, :n_q * (period - 1)]
    return flat.reshape(table.shape[0], n_q, period - 1)[:, :, :n_k]


def _prompt_bias(table):
    bias = _toeplitz_bias(table, ATT_Q, ATT_K, ATT_REACH)
    qc = jnp.arange(ATT_Q)[:, None] // CHUNK
    m = jnp.arange(ATT_K)[None, :]
    kc = m // CHUNK
    band = (kc >= qc) & (kc <= qc + BAND_CHUNKS)
    first_key = ATT_REACH - ATT_Q * jnp.arange(ATT_VARIANTS)
    ok = band[None] & (m[None] >= first_key[:, None, None])
    return jnp.where(ok[:, None], bias[None], NEG)


def _step_bias(table, past):
    bias = _toeplitz_bias(table, DEC_SEQ, past + DEC_SEQ, past)
    return bias[:, :, :past], bias[:, :, past:]


def kernel(x_prompt, x_sample, state_conv, cache_k, cache_v, state_C, state_n, state_m, norm_mix, norm_ffn, w_in, conv_w, rel_bias, gate_bias, mlstm_norm, w_proj_conv, w_proj_att, w_proj_mlstm, w_out, router_group, router_group_bias, router_expert, router_expert_bias, w_gate, w_up, w_down, norm_final):
    past = cache_k.shape[2]
    xp = x_prompt.reshape(SEQ, D_MODEL)
    xs = x_sample.reshape(N_SAMPLE, D_MODEL)
    state_n5 = state_n.reshape(DEPTH, DEC_BATCH, N_HEADS_MLSTM, 1, HEAD_DIM_MLSTM)
    state_m5 = jnp.broadcast_to(state_m[..., None, None], (DEPTH, DEC_BATCH, N_HEADS_MLSTM, 1, LANES))

    w_in_t = jnp.swapaxes(w_in, 1, 2)

    x_mid = moe = None
    outs = {k: [] for k in ("p_conv", "p_k", "p_v", "p_C", "p_n", "p_m", "s_conv", "s_k", "s_v", "s_C", "s_n", "s_m")}
    for l in range(DEPTH):
        gif_hi, gif_lo = _split_hi_lo(_pad_lanes(w_in_t[l, MAIN_WIDTH:MAIN_WIDTH + GIF_WIDTH, :].T))
        gif_b = _pad_lanes(gate_bias[l][None, :])
        g_mix = norm_mix[l][None, :]
        if l == 0:
            x, h, gif = _norm_first(xp, xs, g_mix, gif_hi, gif_lo, gif_b)
        else:
            x, h, gif = _combine_norm(*moe, x_mid, g_mix, gif_hi, gif_lo, gif_b)
        z = _project(h, w_in_t, l, 0, MAIN_WIDTH)
        zg = _project(h, w_in_t, l, MAIN_WIDTH + GIF_WIDTH, N_BRANCH * D_MODEL)

        gnorm = mlstm_norm[l][None, :]
        yb_p, yc_p, p_c, p_n, p_m = _mixers_prompt(z, _prompt_bias(rel_bias[l]), gif, gnorm)
        bias_past, bias_new = _step_bias(rel_bias[l], past)
        yb_s = _attn_step(z, cache_k, cache_v, bias_past, bias_new, l)
        yc_s, s_c, s_n, s_m = _mlstm_step(z, gif, gnorm, state_C, state_n5, state_m5, l)

        prev = state_conv[l]
        s1 = jnp.zeros((DEC_BATCH, DEC_SEQ, D_CONV), F32).at[:, 0].set(prev[:, 1]).reshape(N_SAMPLE, D_CONV)
        s2 = (jnp.zeros((DEC_BATCH, DEC_SEQ, D_CONV), F32).at[:, 0].set(prev[:, 0]).at[:, 1].set(prev[:, 1])
              .reshape(N_SAMPLE, D_CONV))
        x_mid, tails = _merge(z, zg, s1, s2, conv_w[l], yb_p, yb_s, yc_p, yc_s, x,
                              w_proj_conv[l].astype(BF16), w_proj_att[l].astype(BF16),
                              w_proj_mlstm[l].astype(BF16), w_out[l].astype(BF16))

        r_w = _pad_lanes(jnp.concatenate([router_group[l], router_expert[l]], axis=1))
        r_hi, r_lo = _split_hi_lo(r_w)
        r_b = _pad_lanes(jnp.concatenate([router_group_bias[l], router_expert_bias[l]])[None, :])
        xs_local, lp, gcnt = _route_dispatch(x_mid, norm_ffn[l][None, :], r_hi, r_lo, r_b)
        gcnt = gcnt[:, 0, N_GROUPS:N_GROUPS + N_EXPERTS].astype(jnp.int32)
        src, dst, chunk_expert, chunk_first, chunk_valid, chunk_next = _granule_lists(gcnt)
        ys = _expert_ffn(src, chunk_expert, chunk_first, chunk_valid, chunk_next, xs_local, w_gate, w_up, w_down, l)
        moe = (dst, ys, lp)

        keep = min(ATT_REACH, SEQ)
        k_rows = z[SEQ - keep:, COL_K * D_ATT:(COL_K + 1) * D_ATT].astype(F32)
        v_rows = z[SEQ - keep:, COL_V * D_ATT:(COL_V + 1) * D_ATT].astype(F32)
        outs["p_conv"].append(tails[N_PROMPT_TILES - 1, :, GROUPS_PER_TILE - 1][None])
        outs["p_k"].append(k_rows[:keep].reshape(1, keep, N_HEADS_ATT, HEAD_DIM_ATT))
        outs["p_v"].append(v_rows[:keep].reshape(1, keep, N_HEADS_ATT, HEAD_DIM_ATT))
        outs["p_C"].append(p_c[None])
        outs["p_n"].append(p_n[:, 0][None])
        outs["p_m"].append(p_m[:, 0, 0][None])
        outs["s_conv"].append(jnp.swapaxes(tails[N_PROMPT_TILES], 0, 1))
        outs["s_k"].append(k_rows[keep:].reshape(DEC_BATCH, DEC_SEQ, N_HEADS_ATT, HEAD_DIM_ATT))
        outs["s_v"].append(v_rows[keep:].reshape(DEC_BATCH, DEC_SEQ, N_HEADS_ATT, HEAD_DIM_ATT))
        outs["s_C"].append(s_c)
        outs["s_n"].append(s_n[:, :, 0])
        outs["s_m"].append(s_m[:, :, 0, 0])

    g_fin = norm_final[None, :]
    y_prompt = _combine_final(*moe, x_mid, g_fin, 0, N_PROMPT_TILES).reshape(x_prompt.shape)
    y_sample = _combine_final(*moe, x_mid, g_fin, N_PROMPT_TILES, 1).reshape(x_sample.shape)
    st = {k: jnp.stack(v) for k, v in outs.items()}
    return (y_prompt, y_sample, st["p_conv"], st["p_k"], st["p_v"], st["p_C"], st["p_n"], st["p_m"],
            st["s_conv"], st["s_k"], st["s_v"], st["s_C"], st["s_n"], st["s_m"])
```

```python
import functools
import math

import jax
import jax.numpy as jnp
from jax import lax
from jax.experimental import pallas as pl
from jax.experimental.pallas import tpu as pltpu

D_MODEL = 2048
SEQ = 8192
DEPTH = 2
DEC_BATCH = 8
DEC_SEQ = 32
N_SAMPLE = DEC_BATCH * DEC_SEQ
N_ROWS = SEQ + N_SAMPLE

CHUNK = 64
D_CONV = 1024
CONV_W = 3
N_HEADS_ATT = 8
HEAD_DIM_ATT = 128
D_ATT = N_HEADS_ATT * HEAD_DIM_ATT
BAND_CHUNKS = 8
ATT_REACH = BAND_CHUNKS * CHUNK
REL_CLIP = 128
N_HEADS_MLSTM = 4
HEAD_DIM_MLSTM = 256
D_MLSTM = N_HEADS_MLSTM * HEAD_DIM_MLSTM
N_BRANCH = 3
MAIN_WIDTH = 3 * D_CONV + 3 * D_ATT + 4 * D_MLSTM
GIF_WIDTH = 2 * N_HEADS_MLSTM
N_GROUPS = 4
EXPERTS_PER_GROUP = 4
N_EXPERTS = N_GROUPS * EXPERTS_PER_GROUP
D_EXPERT = 512
EPS = 1e-6

LANES = 128
BF16_ROWS = 16
ROW_TILE = 256
N_TILES = N_ROWS // ROW_TILE
N_PROMPT_TILES = SEQ // ROW_TILE
MM_ROWS = 1056
MM_COLS = 1024
ATT_Q = 256
ATT_K = ATT_Q + ATT_REACH
ATT_VARIANTS = ATT_REACH // ATT_Q + 1
MLSTM_L = 256
NEG = -1e30
VMEM_LIMIT = 56 * 1024 * 1024

GRAN = BF16_ROWS
XS_WIDTH = D_MODEL + LANES
MAX_RUN_PAD = N_EXPERTS * (GRAN - 1)
TILE_GRANS = (2 * ROW_TILE + MAX_RUN_PAD) // GRAN + 1
LOCAL_ROWS = TILE_GRANS * GRAN
CHUNK_GRANS = 16
CHUNK_ROWS = CHUNK_GRANS * GRAN
MAX_CHUNKS = (N_TILES * (TILE_GRANS - 1)) // CHUNK_GRANS + N_EXPERTS

COL_XA, COL_GB, COL_GC, COL_Q, COL_K, COL_V, COL_QM, COL_KM, COL_VM, COL_OM = range(10)

F32 = jnp.float32
BF16 = jnp.bfloat16


def _params(sem):
    return pltpu.CompilerParams(dimension_semantics=sem, vmem_limit_bytes=VMEM_LIMIT)


def _dot(a, b):
    return jnp.dot(a, b, preferred_element_type=F32)


def _dot_nt(a, b):
    return lax.dot_general(a, b, (((1,), (1,)), ((), ())), preferred_element_type=F32)


def _dot_tn(a, b):
    return lax.dot_general(a, b, (((0,), (0,)), ((), ())), preferred_element_type=F32)


def _split_hi_lo(w):
    hi = w.astype(BF16)
    lo = (w - hi.astype(F32)).astype(BF16)
    return hi, lo


def _sigmoid(x):
    return 1.0 / (1.0 + jnp.exp(-x))


def _log_sigmoid(x):
    return jnp.minimum(x, 0.0) - jnp.log1p(jnp.exp(-jnp.abs(x)))


def _rms(x, g):
    ms = jnp.mean(x * x, axis=-1, keepdims=True)
    return x * lax.rsqrt(ms + EPS) * g


def _small_proj(h, whi_ref, wlo_ref, b_ref):
    h_hi = h.astype(BF16)
    h_lo = (h - h_hi.astype(F32)).astype(BF16)
    whi = whi_ref[...]
    return _dot(h_hi, whi) + _dot(h_lo, whi) + _dot(h_hi, wlo_ref[...]) + b_ref[...]


def _norm_first_kernel(xp_ref, xs_ref, g_ref, whi_ref, wlo_ref, b_ref, h_out, s_out):
    i = pl.program_id(0)
    x = jnp.where(i < N_PROMPT_TILES, xp_ref[...], xs_ref[...])
    h = _rms(x, g_ref[...])
    h_out[...] = h.astype(BF16)
    s_out[...] = _small_proj(h, whi_ref, wlo_ref, b_ref)


def _route(logits):
    lane = lax.broadcasted_iota(jnp.int32, logits.shape, 1)
    is_g = lane < N_GROUPS
    gl = jnp.where(is_g, logits, NEG)
    gmax = jnp.max(gl, axis=1, keepdims=True)
    g_sel = jnp.min(jnp.where(is_g & (gl == gmax), lane, LANES), axis=1, keepdims=True)
    p_g = 1.0 / jnp.sum(jnp.where(is_g, jnp.exp(gl - gmax), 0.0), axis=1, keepdims=True)
    e_lane = lane - N_GROUPS
    in_g = (e_lane >= 0) & (e_lane < N_EXPERTS) & ((e_lane // EXPERTS_PER_GROUP) == g_sel)
    e1 = jnp.max(jnp.where(in_g, logits, NEG), axis=1, keepdims=True)
    i1 = jnp.min(jnp.where(in_g & (logits == e1), lane, LANES), axis=1, keepdims=True)
    rest = in_g & (lane != i1)
    e2 = jnp.max(jnp.where(rest, logits, NEG), axis=1, keepdims=True)
    i2 = jnp.min(jnp.where(rest & (logits == e2), lane, LANES), axis=1, keepdims=True)
    r = jnp.exp(e2 - e1)
    w1 = p_g / (1.0 + r)
    w2 = w1 * r
    return i1, i2, w1, w2


def _bf16_pieces(w):
    a = w.astype(BF16).astype(F32)
    b = (w - a).astype(BF16).astype(F32)
    return a, b, w - a - b


def _route_dispatch_rows(x, g_ref, whi_ref, wlo_ref, b_ref, xs_out, lp_out, cnt_out):
    h = _rms(x, g_ref[...])
    i1, i2, w1, w2 = _route(_small_proj(h, whi_ref, wlo_ref, b_ref))
    lane = lax.broadcasted_iota(jnp.int32, (ROW_TILE, LANES), 1)
    o1 = lane == i1
    o2 = lane == i2
    onehot = jnp.where(o1 | o2, 1.0, 0.0)
    t_idx = lax.broadcasted_iota(jnp.int32, (ROW_TILE, ROW_TILE), 0)
    s_idx = lax.broadcasted_iota(jnp.int32, (ROW_TILE, ROW_TILE), 1)
    earlier = jnp.where(s_idx < t_idx, 1.0, 0.0).astype(BF16)
    rank = _dot(earlier, onehot.astype(BF16))
    gcnt = jnp.floor((jnp.sum(onehot, axis=0, keepdims=True) + (GRAN - 1)) * (1.0 / GRAN))
    a_idx = lax.broadcasted_iota(jnp.int32, (LANES, LANES), 0)
    b_idx = lax.broadcasted_iota(jnp.int32, (LANES, LANES), 1)
    before = jnp.where(a_idx < b_idx, 1.0, 0.0).astype(BF16)
    gcnt8 = jnp.broadcast_to(gcnt, (8, LANES))
    run_start = _dot(gcnt8.astype(BF16), before)[0:1, :] * GRAN
    pos = run_start + rank
    lpos1 = jnp.sum(jnp.where(o1, pos, 0.0), axis=1, keepdims=True)
    lpos2 = jnp.sum(jnp.where(o2, pos, 0.0), axis=1, keepdims=True)
    lp = jnp.where(lane == 0, lpos1, jnp.where(lane == 1, lpos2, -1.0))
    lp_out[...] = lp
    cnt_out[...] = gcnt8

    meta = jnp.zeros((ROW_TILE, LANES), F32)
    fields = ((i1 - N_GROUPS).astype(F32),) + _bf16_pieces(w1) + ((i2 - N_GROUPS).astype(F32),) + _bf16_pieces(w2)
    for k, val in enumerate(fields):
        meta = jnp.where(lane == k, val, meta)
    rows = jnp.concatenate([h.astype(BF16), meta.astype(BF16)], axis=1)
    lp_t = lp.T
    p_idx = lax.broadcasted_iota(jnp.int32, (LOCAL_ROWS, ROW_TILE), 0).astype(F32)
    perm = jnp.where((p_idx == lp_t[0:1, :]) | (p_idx == lp_t[1:2, :]), 1.0, 0.0).astype(BF16)
    xs_out[...] = _dot(perm, rows).astype(BF16)


def _row_spec(width):
    return pl.BlockSpec((ROW_TILE, width), lambda i: (i, 0))


def _const_spec(shape):
    return pl.BlockSpec(shape, lambda i: (0,) * len(shape))


def _norm_first(xp, xs, g, whi, wlo, b):
    return pl.pallas_call(
        _norm_first_kernel,
        grid=(N_TILES,),
        in_specs=[
            pl.BlockSpec((ROW_TILE, D_MODEL), lambda i: (jnp.minimum(i, N_PROMPT_TILES - 1), 0)),
            _const_spec((ROW_TILE, D_MODEL)),
            _const_spec((1, D_MODEL)),
            _const_spec((D_MODEL, LANES)),
            _const_spec((D_MODEL, LANES)),
            _const_spec((1, LANES)),
        ],
        out_specs=[_row_spec(D_MODEL), _row_spec(LANES)],
        out_shape=[
            jax.ShapeDtypeStruct((N_ROWS, D_MODEL), BF16),
            jax.ShapeDtypeStruct((N_ROWS, LANES), F32),
        ],
        compiler_params=_params(("parallel",)),
        name="norm_first",
    )(xp, xs, g, whi, wlo, b)


def _route_dispatch_kernel(x_ref, g_ref, whi_ref, wlo_ref, b_ref, xs_out, lp_out, cnt_out):
    _route_dispatch_rows(x_ref[...], g_ref, whi_ref, wlo_ref, b_ref, xs_out, lp_out, cnt_out)


def _route_dispatch(x, g, whi, wlo, b):
    return pl.pallas_call(
        _route_dispatch_kernel,
        grid=(N_TILES,),
        in_specs=[
            _row_spec(D_MODEL),
            _const_spec((1, D_MODEL)),
            _const_spec((D_MODEL, LANES)),
            _const_spec((D_MODEL, LANES)),
            _const_spec((1, LANES)),
        ],
        out_specs=[
            pl.BlockSpec((LOCAL_ROWS, XS_WIDTH), lambda i: (i, 0)),
            _row_spec(LANES),
            pl.BlockSpec((None, 8, LANES), lambda i: (i, 0, 0)),
        ],
        out_shape=[
            jax.ShapeDtypeStruct((N_TILES * LOCAL_ROWS, XS_WIDTH), BF16),
            jax.ShapeDtypeStruct((N_ROWS, LANES), F32),
            jax.ShapeDtypeStruct((N_TILES, 8, LANES), F32),
        ],
        compiler_params=_params(("parallel",)),
        name="route_dispatch",
    )(x, g, whi, wlo, b)


def _mm_kernel(h_ref, wt_ref, o_ref, wb_ref):
    @pl.when(pl.program_id(1) == 0)
    def _():
        wb_ref[...] = wt_ref[0].T.astype(BF16)

    o_ref[...] = _dot(h_ref[...], wb_ref[...]).astype(o_ref.dtype)


def _project(h, wt, layer, row0, n_cols):
    return pl.pallas_call(
        _mm_kernel,
        grid=(n_cols // MM_COLS, N_ROWS // MM_ROWS),
        in_specs=[
            pl.BlockSpec((MM_ROWS, D_MODEL), lambda j, i: (i, 0)),
            pl.BlockSpec((pl.Element(1), pl.Element(MM_COLS), pl.Element(D_MODEL)),
                         lambda j, i: (layer, pl.multiple_of(row0 + j * MM_COLS, 8), 0)),
        ],
        out_specs=pl.BlockSpec((MM_ROWS, MM_COLS), lambda j, i: (i, j)),
        out_shape=jax.ShapeDtypeStruct((N_ROWS, n_cols), BF16),
        scratch_shapes=[pltpu.VMEM((D_MODEL, MM_COLS), BF16)],
        compiler_params=_params(("parallel", "arbitrary")),
        name="project",
    )(h, wt)


def _attn_prompt_head(h, q_ref, k_refs, v_refs, bias_ref, o_ref):
    sl = slice(h * HEAD_DIM_ATT, (h + 1) * HEAD_DIM_ATT)
    q = q_ref[:, sl]
    kk = jnp.concatenate([r[:, sl] for r in k_refs], axis=0)
    vv = jnp.concatenate([r[:, sl] for r in v_refs], axis=0)
    s = _dot_nt(q, kk) * (HEAD_DIM_ATT ** -0.5) + bias_ref[h]
    mx = jnp.max(s, axis=1, keepdims=True)
    p = jnp.exp(s - mx).astype(BF16)
    od = _dot(p, jnp.concatenate([vv, jnp.ones((ATT_K, HEAD_DIM_ATT), BF16)], axis=1))
    o = od[:, :HEAD_DIM_ATT] * (1.0 / od[:, HEAD_DIM_ATT:HEAD_DIM_ATT + 1])
    o_ref[:, sl] = o.astype(o_ref.dtype)


def _attn_step_kernel(q_ref, k_ref, v_ref, ck_ref, cv_ref, bp_ref, bn_ref, o_ref):
    scale = HEAD_DIM_ATT ** -0.5
    ck = jnp.swapaxes(ck_ref[...], 0, 1).astype(BF16)
    cv = jnp.swapaxes(cv_ref[...], 0, 1).astype(BF16)
    for h in range(N_HEADS_ATT):
        sl = slice(h * HEAD_DIM_ATT, (h + 1) * HEAD_DIM_ATT)
        q = q_ref[:, sl]
        s_past = _dot_nt(q, ck[h]) * scale + bp_ref[h]
        s_new = _dot_nt(q, k_ref[:, sl]) * scale + bn_ref[h]
        mx = jnp.maximum(jnp.max(s_past, axis=1, keepdims=True), jnp.max(s_new, axis=1, keepdims=True))
        p_past = jnp.exp(s_past - mx)
        p_new = jnp.exp(s_new - mx)
        den = jnp.sum(p_past, axis=1, keepdims=True) + jnp.sum(p_new, axis=1, keepdims=True)
        o = _dot(p_past.astype(BF16), cv[h]) + _dot(p_new.astype(BF16), v_ref[:, sl])
        o_ref[:, sl] = (o / den).astype(o_ref.dtype)


def _attn_step(z, cache_k, cache_v, bias_past, bias_new, layer):
    first = SEQ // DEC_SEQ
    past = cache_k.shape[2]

    def z_spec(col):
        return pl.BlockSpec((DEC_SEQ, D_ATT), lambda b: (first + b, col))

    cache_spec = pl.BlockSpec((None, None, past, N_HEADS_ATT, HEAD_DIM_ATT), lambda b: (layer, b, 0, 0, 0))
    return pl.pallas_call(
        _attn_step_kernel,
        grid=(DEC_BATCH,),
        in_specs=[
            z_spec(COL_Q), z_spec(COL_K), z_spec(COL_V), cache_spec, cache_spec,
            _const_spec((N_HEADS_ATT, DEC_SEQ, past)),
            _const_spec((N_HEADS_ATT, DEC_SEQ, DEC_SEQ)),
        ],
        out_specs=pl.BlockSpec((DEC_SEQ, D_ATT), lambda b: (b, 0)),
        out_shape=jax.ShapeDtypeStruct((N_SAMPLE, D_ATT), BF16),
        compiler_params=_params(("parallel",)),
        name="attn_step",
    )(z, z, z, cache_k, cache_v, bias_past, bias_new)


def _mlstm_block(q, k, v, om, gates, gnorm, c0, n0, m0):
    li_col, lf_col, li_row, lf_row = gates
    L = q.shape[0]
    kscale = HEAD_DIM_MLSTM ** -0.5
    t_idx = lax.broadcasted_iota(jnp.int32, (L, L), 0)
    s_idx = lax.broadcasted_iota(jnp.int32, (L, L), 1)
    causal = s_idx <= t_idx
    b_col = jnp.sum(jnp.where(causal, lf_row, 0.0), axis=1, keepdims=True)
    b_row = jnp.sum(jnp.where(t_idx <= s_idx, lf_col, 0.0), axis=0, keepdims=True)
    d = jnp.where(causal, b_col - b_row + li_row, NEG)
    inter = b_col + m0
    m_col = jnp.maximum(inter, jnp.max(d, axis=1, keepdims=True))
    w = jnp.exp(d - (m_col - math.log(kscale)))
    sc = jnp.exp(inter - m_col)
    qk = _dot_nt(q, k) * w
    num = sc * _dot_nt(q, c0.astype(BF16)) + _dot(qk.astype(BF16), v)
    qn = _dot_nt(q, jnp.broadcast_to(n0, (BF16_ROWS, HEAD_DIM_MLSTM)).astype(BF16))[:, 0:1]
    den = sc * qn + jnp.sum(qk, axis=1, keepdims=True)
    hh = num * (1.0 / jnp.maximum(jnp.abs(den), jnp.exp(-m_col)))
    mu = jnp.mean(hh, axis=1, keepdims=True)
    cen = hh - mu
    var = jnp.mean(cen * cen, axis=1, keepdims=True)
    y = _sigmoid(om.astype(F32)) * (cen * lax.rsqrt(var + EPS) * gnorm)
    m_last = m_col[L - 1:L, :]
    b_last = b_col[L - 1:L, :]
    decay = jnp.exp(b_last + m0 - m_last)
    ws = jnp.exp(b_last - b_col + li_col - m_last) * kscale
    vs = (v.astype(F32) * ws).astype(BF16)
    c1 = decay * c0 + _dot_tn(vs, k)
    n1 = decay * n0 + jnp.sum(k.astype(F32) * ws, axis=0, keepdims=True)
    return y, c1, n1, m_last


def _log_gates(gif):
    lane = lax.broadcasted_iota(jnp.int32, gif.shape, 1)
    lg = jnp.where(lane < N_HEADS_MLSTM, gif, _log_sigmoid(gif))
    return lg, lg.T


def _gate_views(log_gates, head):
    lg, lg_t = log_gates
    f = N_HEADS_MLSTM + head
    return lg[:, head:head + 1], lg[:, f:f + 1], lg_t[head:head + 1, :], lg_t[f:f + 1, :]


def _mixers_prompt_kernel(q_ref, k0_ref, k1_ref, k2_ref, v0_ref, v1_ref, v2_ref, bias_ref,
                          qm_ref, km_ref, vm_ref, om_ref, gif_ref, gn_ref,
                          yb_ref, yc_ref, c_out, n_out, m_out, c_scr, n_scr, m_scr):
    step = pl.program_id(0)

    @pl.when(step == 0)
    def _():
        c_scr[...] = jnp.zeros_like(c_scr)
        n_scr[...] = jnp.zeros_like(n_scr)
        m_scr[...] = jnp.zeros_like(m_scr)

    log_gates = _log_gates(gif_ref[...])
    att_per_mlstm = N_HEADS_ATT // N_HEADS_MLSTM
    for h in range(N_HEADS_MLSTM):
        sl = slice(h * HEAD_DIM_MLSTM, (h + 1) * HEAD_DIM_MLSTM)
        y, c1, n1, m1 = _mlstm_block(qm_ref[:, sl], km_ref[:, sl], vm_ref[:, sl], om_ref[:, sl],
                                     _gate_views(log_gates, h), gn_ref[:, sl],
                                     c_scr[h], n_scr[h], m_scr[h][:, :1])
        yc_ref[:, sl] = y.astype(yc_ref.dtype)
        c_scr[h] = c1
        n_scr[h] = n1
        m_scr[h] = jnp.broadcast_to(m1, (1, LANES))
        for ha in range(h * att_per_mlstm, (h + 1) * att_per_mlstm):
            _attn_prompt_head(ha, q_ref, (k0_ref, k1_ref, k2_ref), (v0_ref, v1_ref, v2_ref), bias_ref, yb_ref)

    @pl.when(step == pl.num_programs(0) - 1)
    def _():
        c_out[...] = c_scr[...]
        n_out[...] = n_scr[...]
        m_out[...] = m_scr[...]


def _mixers_prompt(z, bias, gif, gnorm):
    assert ATT_Q == MLSTM_L

    def z_spec(col):
        return pl.BlockSpec((ATT_Q, D_ATT), lambda j: (j, col))

    def kv_spec(col, back):
        return pl.BlockSpec((ATT_Q, D_ATT), lambda j: (jnp.maximum(j - back, 0), col))

    state_shapes = [
        jax.ShapeDtypeStruct((N_HEADS_MLSTM, HEAD_DIM_MLSTM, HEAD_DIM_MLSTM), F32),
        jax.ShapeDtypeStruct((N_HEADS_MLSTM, 1, HEAD_DIM_MLSTM), F32),
        jax.ShapeDtypeStruct((N_HEADS_MLSTM, 1, LANES), F32),
    ]
    return pl.pallas_call(
        _mixers_prompt_kernel,
        grid=(SEQ // ATT_Q,),
        in_specs=[
            z_spec(COL_Q),
            kv_spec(COL_K, 2), kv_spec(COL_K, 1), kv_spec(COL_K, 0),
            kv_spec(COL_V, 2), kv_spec(COL_V, 1), kv_spec(COL_V, 0),
            pl.BlockSpec((None, N_HEADS_ATT, ATT_Q, ATT_K), lambda j: (jnp.minimum(j, ATT_VARIANTS - 1), 0, 0, 0)),
            z_spec(COL_QM), z_spec(COL_KM), z_spec(COL_VM), z_spec(COL_OM),
            pl.BlockSpec((MLSTM_L, LANES), lambda j: (j, 0)),
            _const_spec((1, D_MLSTM)),
        ],
        out_specs=[pl.BlockSpec((ATT_Q, D_ATT), lambda j: (j, 0)), pl.BlockSpec((MLSTM_L, D_MLSTM), lambda j: (j, 0))]
        + [_const_spec(s.shape) for s in state_shapes],
        out_shape=[jax.ShapeDtypeStruct((SEQ, D_ATT), BF16), jax.ShapeDtypeStruct((SEQ, D_MLSTM), BF16)]
        + state_shapes,
        scratch_shapes=[pltpu.VMEM(s.shape, F32) for s in state_shapes],
        compiler_params=_params(("arbitrary",)),
        name="mixers_prompt",
    )(z, z, z, z, z, z, z, bias, z, z, z, z, gif, gnorm)


def _mlstm_step_kernel(q_ref, k_ref, v_ref, om_ref, gif_ref, gn_ref, c_ref, n_ref, m_ref,
                       y_ref, c_out, n_out, m_out):
    log_gates = _log_gates(gif_ref[...])
    for h in range(N_HEADS_MLSTM):
        sl = slice(h * HEAD_DIM_MLSTM, (h + 1) * HEAD_DIM_MLSTM)
        y, c1, n1, m1 = _mlstm_block(q_ref[:, sl], k_ref[:, sl], v_ref[:, sl], om_ref[:, sl],
                                     _gate_views(log_gates, h), gn_ref[:, sl],
                                     c_ref[h], n_ref[h], m_ref[h][:, :1])
        y_ref[:, sl] = y.astype(y_ref.dtype)
        c_out[h] = c1
        n_out[h] = n1
        m_out[h] = jnp.broadcast_to(m1, (1, LANES))


def _mlstm_step(z, gif, gnorm, state_c, state_n, state_m, layer):
    first = SEQ // DEC_SEQ

    def z_spec(col):
        return pl.BlockSpec((DEC_SEQ, D_MLSTM), lambda b: (first + b, col))

    def st_in(shape):
        return pl.BlockSpec((None, None) + shape, lambda b: (layer, b) + (0,) * len(shape))

    def st_out(shape):
        return pl.BlockSpec((None,) + shape, lambda b: (b,) + (0,) * len(shape))

    shapes = [(N_HEADS_MLSTM, HEAD_DIM_MLSTM, HEAD_DIM_MLSTM), (N_HEADS_MLSTM, 1, HEAD_DIM_MLSTM),
              (N_HEADS_MLSTM, 1, LANES)]
    return pl.pallas_call(
        _mlstm_step_kernel,
        grid=(DEC_BATCH,),
        in_specs=[
            z_spec(COL_QM), z_spec(COL_KM), z_spec(COL_VM), z_spec(COL_OM),
            pl.BlockSpec((DEC_SEQ, LANES), lambda b: (first + b, 0)),
            _const_spec((1, D_MLSTM)),
        ] + [st_in(s) for s in shapes],
        out_specs=[pl.BlockSpec((DEC_SEQ, D_MLSTM), lambda b: (b, 0))] + [st_out(s) for s in shapes],
        out_shape=[jax.ShapeDtypeStruct((N_SAMPLE, D_MLSTM), BF16)]
        + [jax.ShapeDtypeStruct((DEC_BATCH,) + s, F32) for s in shapes],
        compiler_params=_params(("parallel",)),
        name="mlstm_step",
    )(z, z, z, z, gif, gnorm, state_c, state_n, state_m)


HALO = 16
GROUPS_PER_TILE = ROW_TILE // DEC_SEQ


def _merge_kernel(xa_ref, gb_ref, gc_ref, xah_ref, gch_ref, s1_ref, s2_ref, cw_ref,
                  ybp_ref, ybs_ref, ycp_ref, ycs_ref, zg_ref, xp_ref, xs_ref,
                  wpc_ref, wpa_ref, wpm_ref, wout_ref, x_out, tail_out, u_scr):
    i = pl.program_id(0)
    is_s = i >= N_PROMPT_TILES
    row = lax.broadcasted_iota(jnp.int32, (ROW_TILE, 1), 0)
    pos = jnp.where(is_s, row % DEC_SEQ, row)
    u = gc_ref[...].astype(F32) * xa_ref[...].astype(F32)
    u_halo = gch_ref[...].astype(F32) * xah_ref[...].astype(F32)
    keep = jnp.logical_and(i > 0, jnp.logical_not(is_s))
    h1 = jnp.where(keep, u_halo[HALO - 1:HALO, :], 0.0)
    h2 = jnp.where(keep, u_halo[HALO - 2:HALO - 1, :], 0.0)
    f1 = jnp.where(is_s, s1_ref[...], h1)
    f2 = jnp.where(is_s, s2_ref[...], jnp.where(row == 0, h2, h1))
    u_m1 = jnp.where(pos >= 1, pltpu.roll(u, 1, 0), f1)
    u_m2 = jnp.where(pos >= 2, pltpu.roll(u, 2, 0), f2)
    cw = cw_ref[...]
    y = cw[0:1, :] * u_m2 + cw[1:2, :] * u_m1 + cw[2:3, :] * u
    ya = gb_ref[...].astype(F32) * y

    u_scr[...] = u
    for g in range(GROUPS_PER_TILE):
        for j in range(CONV_W - 1):
            src = (g + 1) * DEC_SEQ - (CONV_W - 1) + j
            tail_out[j, g:g + 1, :] = u_scr[src:src + 1, :]

    yb = jnp.where(is_s, ybs_ref[...], ybp_ref[...])
    yc = jnp.where(is_s, ycs_ref[...], ycp_ref[...])
    g = _sigmoid(zg_ref[...].astype(F32))
    merged = (g[:, 0:D_MODEL] * _dot(ya.astype(BF16), wpc_ref[...])
              + g[:, D_MODEL:2 * D_MODEL] * _dot(yb, wpa_ref[...])
              + g[:, 2 * D_MODEL:3 * D_MODEL] * _dot(yc, wpm_ref[...]))
    x = jnp.where(is_s, xs_ref[...], xp_ref[...])
    x_out[...] = x + _dot(merged.astype(BF16), wout_ref[...])


def _merge(z, zg, s1, s2, conv_w, yb_p, yb_s, yc_p, yc_s, x_p, x_s, x_s_block, wpc, wpa, wpm, wout):
    halo_blocks = ROW_TILE // HALO

    def z_spec(col):
        return pl.BlockSpec((ROW_TILE, D_CONV), lambda i: (i, col))

    def halo_spec(col):
        return pl.BlockSpec((HALO, D_CONV), lambda i: (jnp.maximum(i * halo_blocks - 1, 0), col))

    def prompt_spec(width):
        return pl.BlockSpec((ROW_TILE, width), lambda i: (jnp.minimum(i, N_PROMPT_TILES - 1), 0))

    def weight_spec(shape):
        return pl.BlockSpec(shape, lambda i: (0, 0), pipeline_mode=pl.Buffered(1))

    return pl.pallas_call(
        _merge_kernel,
        grid=(N_TILES,),
        in_specs=[
            z_spec(COL_XA), z_spec(COL_GB), z_spec(COL_GC), halo_spec(COL_XA), halo_spec(COL_GC),
            _const_spec((ROW_TILE, D_CONV)), _const_spec((ROW_TILE, D_CONV)), _const_spec((CONV_W, D_CONV)),
            prompt_spec(D_ATT), _const_spec((ROW_TILE, D_ATT)),
            prompt_spec(D_MLSTM), _const_spec((ROW_TILE, D_MLSTM)),
            _row_spec(N_BRANCH * D_MODEL),
            prompt_spec(D_MODEL), pl.BlockSpec((ROW_TILE, D_MODEL), lambda i: (x_s_block, 0)),
            weight_spec((D_CONV, D_MODEL)), weight_spec((D_ATT, D_MODEL)), weight_spec((D_MLSTM, D_MODEL)),
            weight_spec((D_MODEL, D_MODEL)),
        ],
        out_specs=[
            _row_spec(D_MODEL),
            pl.BlockSpec((None, CONV_W - 1, GROUPS_PER_TILE, D_CONV), lambda i: (i, 0, 0, 0)),
        ],
        out_shape=[
            jax.ShapeDtypeStruct((N_ROWS, D_MODEL), F32),
            jax.ShapeDtypeStruct((N_TILES, CONV_W - 1, GROUPS_PER_TILE, D_CONV), F32),
        ],
        scratch_shapes=[pltpu.VMEM((ROW_TILE, D_CONV), F32)],
        compiler_params=_params(("parallel",)),
        name="merge",
    )(z, z, z, z, z, s1, s2, conv_w, yb_p, yb_s, yc_p, yc_s, zg, x_p, x_s, wpc, wpa, wpm, wout)


def _granule_copies(idx_ref, first, n, src_hbm, buf, sem, slot):
    return [
        pltpu.make_async_copy(
            src_hbm.at[pl.ds(pl.multiple_of(idx_ref[first + k] * GRAN, GRAN), GRAN), :],
            buf.at[slot, pl.ds(k * GRAN, GRAN), :],
            sem.at[slot])
        for k in range(n)
    ]


GATHER_AHEAD = 2
GATHER_SLOTS = GATHER_AHEAD + 1


def _prefetched_gather(idx_ref, n, src_hbm, buf, sem, first_group=0):
    step = pl.program_id(0)
    n_steps = pl.num_programs(0)

    def request(ahead):
        slot = (step + ahead) % GATHER_SLOTS
        for cp in _granule_copies(idx_ref, (first_group + step + ahead) * n, n, src_hbm, buf, sem, slot):
            cp.start()

    for ahead in range(GATHER_AHEAD):
        pl.when(jnp.logical_and(step == 0, ahead < n_steps))(functools.partial(request, ahead))
    pl.when(step + GATHER_AHEAD < n_steps)(functools.partial(request, GATHER_AHEAD))

    slot = step % GATHER_SLOTS
    for cp in _granule_copies(idx_ref, (first_group + step) * n, n, src_hbm, buf, sem, slot):
        cp.wait()
    return slot


def _expert_ffn_kernel(src_ref, expert_ref, first_ref, valid_ref, next_ref, xs_hbm, wg_hbm, wu_hbm, wd_hbm,
                       o_ref, buf, sem, wg_st, wu_st, wd_st, wsem, wg_bf, wu_bf, wd_bf, *, layer):
    c = pl.program_id(0)
    slot = _prefetched_gather(src_ref, CHUNK_GRANS, xs_hbm, buf, sem)

    def weight_copies(e):
        return [pltpu.make_async_copy(wg_hbm.at[layer, e], wg_st, wsem.at[0]),
                pltpu.make_async_copy(wu_hbm.at[layer, e], wu_st, wsem.at[1]),
                pltpu.make_async_copy(wd_hbm.at[layer, e], wd_st, wsem.at[2])]

    @pl.when(c == 0)
    def _():
        for cp in weight_copies(expert_ref[0]):
            cp.start()

    @pl.when(first_ref[c] == 1)
    def _():
        for cp in weight_copies(expert_ref[c]):
            cp.wait()
        wg_bf[...] = wg_st[...].astype(BF16)
        wu_bf[...] = wu_st[...].astype(BF16)
        wd_bf[...] = wd_st[...].astype(BF16)

        @pl.when(next_ref[c] >= 0)
        def _():
            for cp in weight_copies(next_ref[c]):
                cp.start()

    @pl.when(valid_ref[c] == 1)
    def _():
        rows = buf[slot]
        x = rows[:, :D_MODEL]
        meta = rows[:, D_MODEL:].astype(F32)
        w_first = meta[:, 1:2] + meta[:, 2:3] + meta[:, 3:4]
        w_second = meta[:, 5:6] + meta[:, 6:7] + meta[:, 7:8]
        w = jnp.where(meta[:, 0:1] == expert_ref[c].astype(F32), w_first, w_second)
        gate = _dot(x, wg_bf[...])
        a = gate * _sigmoid(gate) * _dot(x, wu_bf[...]) * w
        o_ref[...] = _dot(a.astype(BF16), wd_bf[...]).astype(o_ref.dtype)

    @pl.when(valid_ref[c] == 0)
    def _():
        o_ref[...] = jnp.zeros_like(o_ref)


def _expert_ffn(src, chunk_expert, chunk_first, chunk_valid, chunk_next, xs, w_gate, w_up, w_down, layer):
    any_spec = pl.BlockSpec(memory_space=pl.ANY)
    grid_spec = pltpu.PrefetchScalarGridSpec(
        num_scalar_prefetch=5,
        grid=(MAX_CHUNKS,),
        in_specs=[any_spec, any_spec, any_spec, any_spec],
        out_specs=pl.BlockSpec((CHUNK_ROWS, D_MODEL), lambda c, *_: (c, 0)),
        scratch_shapes=[
            pltpu.VMEM((GATHER_SLOTS, CHUNK_ROWS, XS_WIDTH), BF16),
            pltpu.SemaphoreType.DMA((GATHER_SLOTS,)),
            pltpu.VMEM((D_MODEL, D_EXPERT), F32),
            pltpu.VMEM((D_MODEL, D_EXPERT), F32),
            pltpu.VMEM((D_EXPERT, D_MODEL), F32),
            pltpu.SemaphoreType.DMA((3,)),
            pltpu.VMEM((D_MODEL, D_EXPERT), BF16),
            pltpu.VMEM((D_MODEL, D_EXPERT), BF16),
            pltpu.VMEM((D_EXPERT, D_MODEL), BF16),
        ],
    )
    return pl.pallas_call(
        functools.partial(_expert_ffn_kernel, layer=layer),
        grid_spec=grid_spec,
        out_shape=jax.ShapeDtypeStruct((MAX_CHUNKS * CHUNK_ROWS, D_MODEL), BF16),
        compiler_params=_params(("arbitrary",)),
        name="expert_ffn",
    )(src, chunk_expert, chunk_first, chunk_valid, chunk_next, xs, w_gate, w_up, w_down)


def _combine(dst_ref, ys_hbm, lp_ref, x_ref, buf, sem, first_tile=0):
    slot = _prefetched_gather(dst_ref, TILE_GRANS, ys_hbm, buf, sem, first_tile)
    lp = lp_ref[...]
    p_idx = lax.broadcasted_iota(jnp.int32, (ROW_TILE, LOCAL_ROWS), 1).astype(F32)
    pick = jnp.where((p_idx == lp[:, 0:1]) | (p_idx == lp[:, 1:2]), 1.0, 0.0).astype(BF16)
    return x_ref[...] + _dot(pick, buf[slot])


def _combine_norm_kernel(dst_ref, ys_hbm, lp_ref, x_ref, g_ref, whi_ref, wlo_ref, b_ref,
                         x_out, h_out, s_out, buf, sem):
    x = _combine(dst_ref, ys_hbm, lp_ref, x_ref, buf, sem)
    x_out[...] = x
    h = _rms(x, g_ref[...])
    h_out[...] = h.astype(BF16)
    s_out[...] = _small_proj(h, whi_ref, wlo_ref, b_ref)


def _combine_final_kernel(dst_ref, ys_hbm, lp_ref, x_ref, g_ref, o_ref, buf, sem, *, first_tile):
    o_ref[...] = _rms(_combine(dst_ref, ys_hbm, lp_ref, x_ref, buf, sem, first_tile), g_ref[...])


_COMBINE_SCRATCH = [pltpu.VMEM((GATHER_SLOTS, LOCAL_ROWS, D_MODEL), BF16),
                    pltpu.SemaphoreType.DMA((GATHER_SLOTS,))]


def _combine_norm(dst, ys, lp, x, g, whi, wlo, b):
    def row(width):
        return pl.BlockSpec((ROW_TILE, width), lambda i, dst: (i, 0))

    def const(shape):
        return pl.BlockSpec(shape, lambda i, dst: (0,) * len(shape))

    grid_spec = pltpu.PrefetchScalarGridSpec(
        num_scalar_prefetch=1,
        grid=(N_TILES,),
        in_specs=[pl.BlockSpec(memory_space=pl.ANY), row(LANES), row(D_MODEL), const((1, D_MODEL)),
                  const((D_MODEL, LANES)), const((D_MODEL, LANES)), const((1, LANES))],
        out_specs=[row(D_MODEL), row(D_MODEL), row(LANES)],
        scratch_shapes=_COMBINE_SCRATCH,
    )
    return pl.pallas_call(
        _combine_norm_kernel,
        grid_spec=grid_spec,
        out_shape=[
            jax.ShapeDtypeStruct((N_ROWS, D_MODEL), F32),
            jax.ShapeDtypeStruct((N_ROWS, D_MODEL), BF16),
            jax.ShapeDtypeStruct((N_ROWS, LANES), F32),
        ],
        compiler_params=_params(("arbitrary",)),
        name="combine_norm",
    )(dst, ys, lp, x, g, whi, wlo, b)


def _combine_final(dst, ys, lp, x, g, first_tile, n_tiles):
    def row(width):
        return pl.BlockSpec((ROW_TILE, width), lambda i, dst: (i + first_tile, 0))

    grid_spec = pltpu.PrefetchScalarGridSpec(
        num_scalar_prefetch=1,
        grid=(n_tiles,),
        in_specs=[pl.BlockSpec(memory_space=pl.ANY), row(LANES), row(D_MODEL),
                  pl.BlockSpec((1, D_MODEL), lambda i, dst: (0, 0))],
        out_specs=pl.BlockSpec((ROW_TILE, D_MODEL), lambda i, dst: (i, 0)),
        scratch_shapes=_COMBINE_SCRATCH,
    )
    return pl.pallas_call(
        functools.partial(_combine_final_kernel, first_tile=first_tile),
        grid_spec=grid_spec,
        out_shape=jax.ShapeDtypeStruct((n_tiles * ROW_TILE, D_MODEL), F32),
        compiler_params=_params(("arbitrary",)),
        name="combine_final",
    )(dst, ys, lp, x, g)


N_SRC = MAX_CHUNKS * CHUNK_GRANS
N_DST = N_TILES * TILE_GRANS


def _granule_kernel(gcnt_ref, src_ref, dst_ref, expert_ref, first_ref, valid_ref, next_ref, local_ref):
    def fill(ref, n, value):
        def body(k, carry):
            ref[k] = value
            return carry
        lax.fori_loop(0, n, body, 0)

    fill(src_ref, N_SRC, TILE_GRANS - 1)
    fill(dst_ref, N_DST, 0)
    fill(expert_ref, MAX_CHUNKS, N_EXPERTS - 1)
    fill(first_ref, MAX_CHUNKS, 0)
    fill(valid_ref, MAX_CHUNKS, 0)
    fill(next_ref, MAX_CHUNKS + 1, -1)
    fill(local_ref, N_TILES, 0)

    def per_expert(e, carry):
        chunk, prev_first = carry
        slot0 = chunk * CHUNK_GRANS

        def per_tile(t, pos):
            n = gcnt_ref[t * N_EXPERTS + e]
            local0 = t * TILE_GRANS + local_ref[t]

            def per_granule(k, c2):
                src_ref[pos + k] = local0 + k
                dst_ref[local0 + k] = pos + k
                return c2

            lax.fori_loop(0, n, per_granule, 0)
            local_ref[t] = local_ref[t] + n
            return pos + n

        end = lax.fori_loop(0, N_TILES, per_tile, slot0)
        n_chunks = (end - slot0 + CHUNK_GRANS - 1) // CHUNK_GRANS

        def per_chunk(k, c2):
            expert_ref[chunk + k] = e
            valid_ref[chunk + k] = 1
            return c2

        lax.fori_loop(0, n_chunks, per_chunk, 0)
        owns = n_chunks > 0

        @pl.when(owns)
        def _():
            first_ref[chunk] = 1
            next_ref[prev_first] = e

        return chunk + n_chunks, jnp.where(owns, chunk, prev_first)

    lax.fori_loop(0, N_EXPERTS, per_expert, (0, MAX_CHUNKS))


def _granule_lists(gcnt):
    smem = pl.BlockSpec(memory_space=pltpu.SMEM)
    sizes = (N_SRC, N_DST, MAX_CHUNKS, MAX_CHUNKS, MAX_CHUNKS, MAX_CHUNKS + 1)
    return pl.pallas_call(
        _granule_kernel,
        in_specs=[smem],
        out_specs=[smem] * len(sizes),
        out_shape=[jax.ShapeDtypeStruct((n,), jnp.int32) for n in sizes],
        scratch_shapes=[pltpu.SMEM((N_TILES,), jnp.int32)],
        name="granule_lists",
    )(gcnt.reshape(N_TILES * N_EXPERTS))


def _pad_lanes(w):
    return jnp.pad(w, [(0, 0)] * (w.ndim - 1) + [(0, LANES - w.shape[-1])])


def _toeplitz_bias(table, n_q, n_k, reach):
    period = 1
    while period < n_q + n_k:
        period *= 2
    j = jnp.arange(period)
    d = jnp.where(j < n_k, j, j - period)
    u = table[:, jnp.clip(reach - d, -REL_CLIP, REL_CLIP) + REL_CLIP]
    flat = jnp.tile(u, (1, n_q))[:, :n_q * (period - 1)]
    return flat.reshape(table.shape[0], n_q, period - 1)[:, :, :n_k]


def _prompt_bias(table):
    bias = _toeplitz_bias(table, ATT_Q, ATT_K, ATT_REACH)
    qc = jnp.arange(ATT_Q)[:, None] // CHUNK
    m = jnp.arange(ATT_K)[None, :]
    kc = m // CHUNK
    band = (kc >= qc) & (kc <= qc + BAND_CHUNKS)
    first_key = ATT_REACH - ATT_Q * jnp.arange(ATT_VARIANTS)
    ok = band[None] & (m[None] >= first_key[:, None, None])
    return jnp.where(ok[:, None], bias[None], NEG)


def _step_bias(table, past):
    bias = _toeplitz_bias(table, DEC_SEQ, past + DEC_SEQ, past)
    return bias[:, :, :past], bias[:, :, past:]


def kernel(x_prompt, x_sample, state_conv, cache_k, cache_v, state_C, state_n, state_m, norm_mix, norm_ffn, w_in, conv_w, rel_bias, gate_bias, mlstm_norm, w_proj_conv, w_proj_att, w_proj_mlstm, w_out, router_group, router_group_bias, router_expert, router_expert_bias, w_gate, w_up, w_down, norm_final):
    past = cache_k.shape[2]
    xp = x_prompt.reshape(SEQ, D_MODEL)
    xs = x_sample.reshape(N_SAMPLE, D_MODEL)
    state_n5 = state_n.reshape(DEPTH, DEC_BATCH, N_HEADS_MLSTM, 1, HEAD_DIM_MLSTM)
    state_m5 = jnp.broadcast_to(state_m[..., None, None], (DEPTH, DEC_BATCH, N_HEADS_MLSTM, 1, LANES))

    w_in_t = jnp.swapaxes(w_in, 1, 2)

    x_mid = moe = None
    outs = {k: [] for k in ("p_conv", "p_k", "p_v", "p_C", "p_n", "p_m", "s_conv", "s_k", "s_v", "s_C", "s_n", "s_m")}
    for l in range(DEPTH):
        gif_hi, gif_lo = _split_hi_lo(_pad_lanes(w_in_t[l, MAIN_WIDTH:MAIN_WIDTH + GIF_WIDTH, :].T))
        gif_b = _pad_lanes(gate_bias[l][None, :])
        g_mix = norm_mix[l][None, :]
        if l == 0:
            h, gif = _norm_first(xp, xs, g_mix, gif_hi, gif_lo, gif_b)
            residual = (xp, xs, 0)
        else:
            x, h, gif = _combine_norm(*moe, x_mid, g_mix, gif_hi, gif_lo, gif_b)
            residual = (x, x, N_PROMPT_TILES)
        z = _project(h, w_in_t, l, 0, MAIN_WIDTH)
        zg = _project(h, w_in_t, l, MAIN_WIDTH + GIF_WIDTH, N_BRANCH * D_MODEL)

        gnorm = mlstm_norm[l][None, :]
        yb_p, yc_p, p_c, p_n, p_m = _mixers_prompt(z, _prompt_bias(rel_bias[l]), gif, gnorm)
        bias_past, bias_new = _step_bias(rel_bias[l], past)
        yb_s = _attn_step(z, cache_k, cache_v, bias_past, bias_new, l)
        yc_s, s_c, s_n, s_m = _mlstm_step(z, gif, gnorm, state_C, state_n5, state_m5, l)

        prev = state_conv[l]
        s1 = jnp.zeros((DEC_BATCH, DEC_SEQ, D_CONV), F32).at[:, 0].set(prev[:, 1]).reshape(N_SAMPLE, D_CONV)
        s2 = (jnp.zeros((DEC_BATCH, DEC_SEQ, D_CONV), F32).at[:, 0].set(prev[:, 0]).at[:, 1].set(prev[:, 1])
              .reshape(N_SAMPLE, D_CONV))
        x_mid, tails = _merge(z, zg, s1, s2, conv_w[l], yb_p, yb_s, yc_p, yc_s, *residual,
                              w_proj_conv[l].astype(BF16), w_proj_att[l].astype(BF16),
                              w_proj_mlstm[l].astype(BF16), w_out[l].astype(BF16))

        r_w = _pad_lanes(jnp.concatenate([router_group[l], router_expert[l]], axis=1))
        r_hi, r_lo = _split_hi_lo(r_w)
        r_b = _pad_lanes(jnp.concatenate([router_group_bias[l], router_expert_bias[l]])[None, :])
        xs_local, lp, gcnt = _route_dispatch(x_mid, norm_ffn[l][None, :], r_hi, r_lo, r_b)
        gcnt =gcnt[:, 0, N_GROUPS:N_GROUPS + N_EXPERTS].astype(jnp.int32)
        src, dst, chunk_expert, chunk_first, chunk_valid, chunk_next = _granule_lists(gcnt)
        ys = _expert_ffn(src, chunk_expert, chunk_first, chunk_valid, chunk_next, xs_local, w_gate, w_up, w_down, l)
        moe = (dst, ys, lp)

        keep = min(ATT_REACH, SEQ)
        k_rows = z[SEQ - keep:, COL_K * D_ATT:(COL_K + 1) * D_ATT].astype(F32)
        v_rows = z[SEQ - keep:, COL_V * D_ATT:(COL_V + 1) * D_ATT].astype(F32)
        outs["p_conv"].append(tails[N_PROMPT_TILES - 1, :, GROUPS_PER_TILE - 1][None])
        outs["p_k"].append(k_rows[:keep].reshape(1, keep, N_HEADS_ATT, HEAD_DIM_ATT))
        outs["p_v"].append(v_rows[:keep].reshape(1, keep, N_HEADS_ATT, HEAD_DIM_ATT))
        outs["p_C"].append(p_c[None])
        outs["p_n"].append(p_n[:, 0][None])
        outs["p_m"].append(p_m[:, 0, 0][None])
        outs["s_conv"].append(jnp.swapaxes(tails[N_PROMPT_TILES], 0, 1))
        outs["s_k"].append(k_rows[keep:].reshape(DEC_BATCH, DEC_SEQ, N_HEADS_ATT, HEAD_DIM_ATT))
        outs["s_v"].append(v_rows[keep:].reshape(DEC_BATCH, DEC_SEQ, N_HEADS_ATT, HEAD_DIM_ATT))
        outs["s_C"].append(s_c)
        outs["s_n"].append(s_n[:, :, 0])
        outs["s_m"].append(s_m[:, :, 0, 0])

    g_fin = norm_final[None, :]
    y_prompt = _combine_final(*moe, x_mid, g_fin, 0, N_PROMPT_TILES).reshape(x_prompt.shape)
    y_sample = _combine_final(*moe, x_mid, g_fin, N_PROMPT_TILES, 1).reshape(x_sample.shape)
    st = {k: jnp.stack(v) for k, v in outs.items()}
    return (y_prompt, y_sample, st["p_conv"], st["p_k"], st["p_v"], st["p_C"], st["p_n"], st["p_m"],
            st["s_conv"], st["s_k"], st["s_v"], st["s_C"], st["s_n"], st["s_m"])
```

```python
import functools
import math

import jax
import jax.numpy as jnp
from jax import lax
from jax.experimental import pallas as pl
from jax.experimental.pallas import tpu as pltpu

D_MODEL = 2048
SEQ = 8192
DEPTH = 2
DEC_BATCH = 8
DEC_SEQ = 32
N_SAMPLE = DEC_BATCH * DEC_SEQ
N_ROWS = SEQ + N_SAMPLE

CHUNK = 64
D_CONV = 1024
CONV_W = 3
N_HEADS_ATT = 8
HEAD_DIM_ATT = 128
D_ATT = N_HEADS_ATT * HEAD_DIM_ATT
BAND_CHUNKS = 8
ATT_REACH = BAND_CHUNKS * CHUNK
REL_CLIP = 128
N_HEADS_MLSTM = 4
HEAD_DIM_MLSTM = 256
D_MLSTM = N_HEADS_MLSTM * HEAD_DIM_MLSTM
N_BRANCH = 3
MAIN_WIDTH = 3 * D_CONV + 3 * D_ATT + 4 * D_MLSTM
GIF_WIDTH = 2 * N_HEADS_MLSTM
N_GROUPS = 4
EXPERTS_PER_GROUP = 4
N_EXPERTS = N_GROUPS * EXPERTS_PER_GROUP
D_EXPERT = 512
EPS = 1e-6

LANES = 128
BF16_ROWS = 16
ROW_TILE = 256
N_TILES = N_ROWS // ROW_TILE
N_PROMPT_TILES = SEQ // ROW_TILE
MM_ROWS = 1056
MM_COLS = 1024
ATT_Q = 256
ATT_K = ATT_Q + ATT_REACH
ATT_VARIANTS = ATT_REACH // ATT_Q + 1
MLSTM_L = 256
NEG = -1e30
VMEM_LIMIT = 56 * 1024 * 1024

GRAN = BF16_ROWS
XS_WIDTH = D_MODEL + LANES
MAX_RUN_PAD = N_EXPERTS * (GRAN - 1)
TILE_GRANS = (2 * ROW_TILE + MAX_RUN_PAD) // GRAN + 1
LOCAL_ROWS = TILE_GRANS * GRAN
CHUNK_GRANS = 16
CHUNK_ROWS = CHUNK_GRANS * GRAN
MAX_CHUNKS = (N_TILES * (TILE_GRANS - 1)) // CHUNK_GRANS + N_EXPERTS

COL_XA, COL_GB, COL_GC, COL_Q, COL_K, COL_V, COL_QM, COL_KM, COL_VM, COL_OM = range(10)

F32 = jnp.float32
BF16 = jnp.bfloat16


def _params(sem):
    return pltpu.CompilerParams(dimension_semantics=sem, vmem_limit_bytes=VMEM_LIMIT)


def _dot(a, b):
    return jnp.dot(a, b, preferred_element_type=F32)


def _dot_nt(a, b):
    return lax.dot_general(a, b, (((1,), (1,)), ((), ())), preferred_element_type=F32)


def _dot_tn(a, b):
    return lax.dot_general(a, b, (((0,), (0,)), ((), ())), preferred_element_type=F32)


def _split_hi_lo(w):
    hi = w.astype(BF16)
    lo = (w - hi.astype(F32)).astype(BF16)
    return hi, lo


def _sigmoid(x):
    return 1.0 / (1.0 + jnp.exp(-x))


def _log_sigmoid(x):
    return jnp.minimum(x, 0.0) - jnp.log1p(jnp.exp(-jnp.abs(x)))


def _rms(x, g):
    ms = jnp.mean(x * x, axis=-1, keepdims=True)
    return x * lax.rsqrt(ms + EPS) * g


def _small_proj(h, whi_ref, wlo_ref, b_ref):
    h_hi = h.astype(BF16)
    h_lo = (h - h_hi.astype(F32)).astype(BF16)
    whi = whi_ref[...]
    return _dot(h_hi, whi) + _dot(h_lo, whi) + _dot(h_hi, wlo_ref[...]) + b_ref[...]


def _norm_first_kernel(xp_ref, xs_ref, g_ref, whi_ref, wlo_ref, b_ref, h_out, s_out):
    i = pl.program_id(0)
    x = jnp.where(i < N_PROMPT_TILES, xp_ref[...], xs_ref[...])
    h = _rms(x, g_ref[...])
    h_out[...] = h.astype(BF16)
    s_out[...] = _small_proj(h, whi_ref, wlo_ref, b_ref)


def _route(logits):
    lane = lax.broadcasted_iota(jnp.int32, logits.shape, 1)
    is_g = lane < N_GROUPS
    gl = jnp.where(is_g, logits, NEG)
    gmax = jnp.max(gl, axis=1, keepdims=True)
    g_sel = jnp.min(jnp.where(is_g & (gl == gmax), lane, LANES), axis=1, keepdims=True)
    p_g = 1.0 / jnp.sum(jnp.where(is_g, jnp.exp(gl - gmax), 0.0), axis=1, keepdims=True)
    e_lane = lane - N_GROUPS
    in_g = (e_lane >= 0) & (e_lane < N_EXPERTS) & ((e_lane // EXPERTS_PER_GROUP) == g_sel)
    e1 = jnp.max(jnp.where(in_g, logits, NEG), axis=1, keepdims=True)
    i1 = jnp.min(jnp.where(in_g & (logits == e1), lane, LANES), axis=1, keepdims=True)
    rest = in_g & (lane != i1)
    e2 = jnp.max(jnp.where(rest, logits, NEG), axis=1, keepdims=True)
    i2 = jnp.min(jnp.where(rest & (logits == e2), lane, LANES), axis=1, keepdims=True)
    r = jnp.exp(e2 - e1)
    w1 = p_g / (1.0 + r)
    w2 = w1 * r
    return i1, i2, w1, w2


def _bf16_pieces(w):
    a = w.astype(BF16).astype(F32)
    b = (w - a).astype(BF16).astype(F32)
    return a, b, w - a - b


def _route_dispatch_rows(x, g_ref, whi_ref, wlo_ref, b_ref, xs_out, lp_out, cnt_out):
    h = _rms(x, g_ref[...])
    i1, i2, w1, w2 = _route(_small_proj(h, whi_ref, wlo_ref, b_ref))
    lane = lax.broadcasted_iota(jnp.int32, (ROW_TILE, LANES), 1)
    o1 = lane == i1
    o2 = lane == i2
    onehot = jnp.where(o1 | o2, 1.0, 0.0)
    t_idx = lax.broadcasted_iota(jnp.int32, (ROW_TILE, ROW_TILE), 0)
    s_idx = lax.broadcasted_iota(jnp.int32, (ROW_TILE, ROW_TILE), 1)
    earlier = jnp.where(s_idx < t_idx, 1.0, 0.0).astype(BF16)
    rank = _dot(earlier, onehot.astype(BF16))
    gcnt = jnp.floor((jnp.sum(onehot, axis=0, keepdims=True) + (GRAN - 1)) * (1.0 / GRAN))
    a_idx = lax.broadcasted_iota(jnp.int32, (LANES, LANES), 0)
    b_idx = lax.broadcasted_iota(jnp.int32, (LANES, LANES), 1)
    before = jnp.where(a_idx < b_idx, 1.0, 0.0).astype(BF16)
    gcnt8 = jnp.broadcast_to(gcnt, (8, LANES))
    run_start = _dot(gcnt8.astype(BF16), before)[0:1, :] * GRAN
    pos = run_start + rank
    lpos1 = jnp.sum(jnp.where(o1, pos, 0.0), axis=1, keepdims=True)
    lpos2 = jnp.sum(jnp.where(o2, pos, 0.0), axis=1, keepdims=True)
    lp = jnp.where(lane == 0, lpos1, jnp.where(lane == 1, lpos2, -1.0))
    lp_out[...] = lp
    cnt_out[...] = gcnt8

    meta = jnp.zeros((ROW_TILE, LANES), F32)
    fields = ((i1 - N_GROUPS).astype(F32),) + _bf16_pieces(w1) + ((i2 - N_GROUPS).astype(F32),) + _bf16_pieces(w2)
    for k, val in enumerate(fields):
        meta = jnp.where(lane == k, val, meta)
    rows = jnp.concatenate([h.astype(BF16), meta.astype(BF16)], axis=1)
    lp_t = lp.T
    p_idx = lax.broadcasted_iota(jnp.int32, (LOCAL_ROWS, ROW_TILE), 0).astype(F32)
    perm = jnp.where((p_idx == lp_t[0:1, :]) | (p_idx == lp_t[1:2, :]), 1.0, 0.0).astype(BF16)
    xs_out[...] = _dot(perm, rows).astype(BF16)


def _row_spec(width):
    return pl.BlockSpec((ROW_TILE, width), lambda i: (i, 0))


def _const_spec(shape):
    return pl.BlockSpec(shape, lambda i: (0,) * len(shape))


def _layer_spec(layer, shape, **kwargs):
    return pl.BlockSpec((None,) + tuple(shape), lambda *_: (layer,) + (0,) * len(shape), **kwargs)


def _norm_param_specs(layer):
    return [_layer_spec(layer, (1, D_MODEL)), _layer_spec(layer, (D_MODEL, LANES)),
            _layer_spec(layer, (D_MODEL, LANES)), _layer_spec(layer, (1, LANES))]


def _norm_first(xp, xs, g, whi, wlo, b, layer):
    return pl.pallas_call(
        _norm_first_kernel,
        grid=(N_TILES,),
        in_specs=[
            pl.BlockSpec((ROW_TILE, D_MODEL), lambda i: (jnp.minimum(i, N_PROMPT_TILES - 1), 0)),
            _const_spec((ROW_TILE, D_MODEL)),
        ] + _norm_param_specs(layer),
        out_specs=[_row_spec(D_MODEL), _row_spec(LANES)],
        out_shape=[
            jax.ShapeDtypeStruct((N_ROWS, D_MODEL), BF16),
            jax.ShapeDtypeStruct((N_ROWS, LANES), F32),
        ],
        compiler_params=_params(("parallel",)),
        name="norm_first",
    )(xp, xs, g, whi, wlo, b)


def _route_dispatch_kernel(x_ref, g_ref, whi_ref, wlo_ref, b_ref, xs_out, lp_out, cnt_out):
    _route_dispatch_rows(x_ref[...], g_ref, whi_ref, wlo_ref, b_ref, xs_out, lp_out, cnt_out)


def _route_dispatch(x, g, whi, wlo, b, layer):
    return pl.pallas_call(
        _route_dispatch_kernel,
        grid=(N_TILES,),
        in_specs=[_row_spec(D_MODEL)] + _norm_param_specs(layer),
        out_specs=[
            pl.BlockSpec((LOCAL_ROWS, XS_WIDTH), lambda i: (i, 0)),
            _row_spec(LANES),
            pl.BlockSpec((None, 8, LANES), lambda i: (i, 0, 0)),
        ],
        out_shape=[
            jax.ShapeDtypeStruct((N_TILES * LOCAL_ROWS, XS_WIDTH), BF16),
            jax.ShapeDtypeStruct((N_ROWS, LANES), F32),
            jax.ShapeDtypeStruct((N_TILES, 8, LANES), F32),
        ],
        compiler_params=_params(("parallel",)),
        name="route_dispatch",
    )(x, g, whi, wlo, b)


def _mm_kernel(h_ref, wt_ref, o_ref, wb_ref):
    @pl.when(pl.program_id(1) == 0)
    def _():
        wb_ref[...] = wt_ref[0].T.astype(BF16)

    o_ref[...] = _dot(h_ref[...], wb_ref[...]).astype(o_ref.dtype)


def _project(h, wt, layer, row0, n_cols):
    return pl.pallas_call(
        _mm_kernel,
        grid=(n_cols // MM_COLS, N_ROWS // MM_ROWS),
        in_specs=[
            pl.BlockSpec((MM_ROWS, D_MODEL), lambda j, i: (i, 0)),
            pl.BlockSpec((pl.Element(1), pl.Element(MM_COLS), pl.Element(D_MODEL)),
                         lambda j, i: (layer, pl.multiple_of(row0 + j * MM_COLS, 8), 0)),
        ],
        out_specs=pl.BlockSpec((MM_ROWS, MM_COLS), lambda j, i: (i, j)),
        out_shape=jax.ShapeDtypeStruct((N_ROWS, n_cols), BF16),
        scratch_shapes=[pltpu.VMEM((D_MODEL, MM_COLS), BF16)],
        compiler_params=_params(("parallel", "arbitrary")),
        name="project",
    )(h, wt)


def _attn_prompt_head(h, q_ref, k_refs, v_refs, bias_ref, o_ref):
    sl = slice(h * HEAD_DIM_ATT, (h + 1) * HEAD_DIM_ATT)
    q = q_ref[:, sl]
    kk = jnp.concatenate([r[:, sl] for r in k_refs], axis=0)
    vv = jnp.concatenate([r[:, sl] for r in v_refs], axis=0)
    s = _dot_nt(q, kk) * (HEAD_DIM_ATT ** -0.5) + bias_ref[h]
    mx = jnp.max(s, axis=1, keepdims=True)
    p = jnp.exp(s - mx).astype(BF16)
    od = _dot(p, jnp.concatenate([vv, jnp.ones((ATT_K, HEAD_DIM_ATT), BF16)], axis=1))
    o = od[:, :HEAD_DIM_ATT] * (1.0 / od[:, HEAD_DIM_ATT:HEAD_DIM_ATT + 1])
    o_ref[:, sl] = o.astype(o_ref.dtype)


def _attn_step_kernel(q_ref, k_ref, v_ref, ck_ref, cv_ref, bp_ref, bn_ref, o_ref):
    scale = HEAD_DIM_ATT ** -0.5
    ck = jnp.swapaxes(ck_ref[...], 0, 1).astype(BF16)
    cv = jnp.swapaxes(cv_ref[...], 0, 1).astype(BF16)
    for h in range(N_HEADS_ATT):
        sl = slice(h * HEAD_DIM_ATT, (h + 1) * HEAD_DIM_ATT)
        q = q_ref[:, sl]
        s_past = _dot_nt(q, ck[h]) * scale + bp_ref[h]
        s_new = _dot_nt(q, k_ref[:, sl]) * scale + bn_ref[h]
        mx = jnp.maximum(jnp.max(s_past, axis=1, keepdims=True), jnp.max(s_new, axis=1, keepdims=True))
        p_past = jnp.exp(s_past - mx)
        p_new = jnp.exp(s_new - mx)
        den = jnp.sum(p_past, axis=1, keepdims=True) + jnp.sum(p_new, axis=1, keepdims=True)
        o = _dot(p_past.astype(BF16), cv[h]) + _dot(p_new.astype(BF16), v_ref[:, sl])
        o_ref[:, sl] = (o / den).astype(o_ref.dtype)


def _attn_step(z, cache_k, cache_v, bias_past, bias_new, layer):
    first = SEQ // DEC_SEQ
    past = cache_k.shape[2]

    def z_spec(col):
        return pl.BlockSpec((DEC_SEQ, D_ATT), lambda b: (first + b, col))

    cache_spec = pl.BlockSpec((None, None, past, N_HEADS_ATT, HEAD_DIM_ATT), lambda b: (layer, b, 0, 0, 0))
    return pl.pallas_call(
        _attn_step_kernel,
        grid=(DEC_BATCH,),
        in_specs=[
            z_spec(COL_Q), z_spec(COL_K), z_spec(COL_V), cache_spec, cache_spec,
            _layer_spec(layer, (N_HEADS_ATT, DEC_SEQ, past)),
            _layer_spec(layer, (N_HEADS_ATT, DEC_SEQ, DEC_SEQ)),
        ],
        out_specs=pl.BlockSpec((DEC_SEQ, D_ATT), lambda b: (b, 0)),
        out_shape=jax.ShapeDtypeStruct((N_SAMPLE, D_ATT), BF16),
        compiler_params=_params(("parallel",)),
        name="attn_step",
    )(z, z, z, cache_k, cache_v, bias_past, bias_new)


def _mlstm_block(q, k, v, om, gates, gnorm, c0, n0, m0):
    li_col, lf_col, li_row, lf_row = gates
    L = q.shape[0]
    kscale = HEAD_DIM_MLSTM ** -0.5
    t_idx = lax.broadcasted_iota(jnp.int32, (L, L), 0)
    s_idx = lax.broadcasted_iota(jnp.int32, (L, L), 1)
    causal = s_idx <= t_idx
    b_col = jnp.sum(jnp.where(causal, lf_row, 0.0), axis=1, keepdims=True)
    b_row = jnp.sum(jnp.where(t_idx <= s_idx, lf_col, 0.0), axis=0, keepdims=True)
    d = jnp.where(causal, b_col - b_row + li_row, NEG)
    inter = b_col + m0
    m_col = jnp.maximum(inter, jnp.max(d, axis=1, keepdims=True))
    w = jnp.exp(d - (m_col - math.log(kscale)))
    sc = jnp.exp(inter - m_col)
    qk = _dot_nt(q, k) * w
    num = sc * _dot_nt(q, c0.astype(BF16)) + _dot(qk.astype(BF16), v)
    qn = _dot_nt(q, jnp.broadcast_to(n0, (BF16_ROWS, HEAD_DIM_MLSTM)).astype(BF16))[:, 0:1]
    den = sc * qn + jnp.sum(qk, axis=1, keepdims=True)
    hh = num * (1.0 / jnp.maximum(jnp.abs(den), jnp.exp(-m_col)))
    mu = jnp.mean(hh, axis=1, keepdims=True)
    cen = hh - mu
    var = jnp.mean(cen * cen, axis=1, keepdims=True)
    y = _sigmoid(om.astype(F32)) * (cen * lax.rsqrt(var + EPS) * gnorm)
    m_last = m_col[L - 1:L, :]
    b_last = b_col[L - 1:L, :]
    decay = jnp.exp(b_last + m0 - m_last)
    ws = jnp.exp(b_last - b_col + li_col - m_last) * kscale
    vs = (v.astype(F32) * ws).astype(BF16)
    c1 = decay * c0 + _dot_tn(vs, k)
    n1 = decay * n0 + jnp.sum(k.astype(F32) * ws, axis=0, keepdims=True)
    return y, c1, n1, m_last


def _log_gates(gif):
    lane = lax.broadcasted_iota(jnp.int32, gif.shape, 1)
    lg = jnp.where(lane < N_HEADS_MLSTM, gif, _log_sigmoid(gif))
    return lg, lg.T


def _gate_views(log_gates, head):
    lg, lg_t = log_gates
    f = N_HEADS_MLSTM + head
    return lg[:, head:head + 1], lg[:, f:f + 1], lg_t[head:head + 1, :], lg_t[f:f + 1, :]


def _mixers_prompt_kernel(q_ref, k0_ref, k1_ref, k2_ref, v0_ref, v1_ref, v2_ref, bias_ref,
                          qm_ref, km_ref, vm_ref, om_ref, gif_ref, gn_ref,
                          yb_ref, yc_ref, c_out, n_out, m_out, c_scr, n_scr, m_scr):
    step = pl.program_id(0)

    @pl.when(step == 0)
    def _():
        c_scr[...] = jnp.zeros_like(c_scr)
        n_scr[...] = jnp.zeros_like(n_scr)
        m_scr[...] = jnp.zeros_like(m_scr)

    log_gates = _log_gates(gif_ref[...])
    att_per_mlstm = N_HEADS_ATT // N_HEADS_MLSTM
    for h in range(N_HEADS_MLSTM):
        sl = slice(h * HEAD_DIM_MLSTM, (h + 1) * HEAD_DIM_MLSTM)
        y, c1, n1, m1 = _mlstm_block(qm_ref[:, sl], km_ref[:, sl], vm_ref[:, sl], om_ref[:, sl],
                                     _gate_views(log_gates, h), gn_ref[:, sl],
                                     c_scr[h], n_scr[h], m_scr[h][:, :1])
        yc_ref[:, sl] = y.astype(yc_ref.dtype)
        c_scr[h] = c1
        n_scr[h] = n1
        m_scr[h] = jnp.broadcast_to(m1, (1, LANES))
        for ha in range(h * att_per_mlstm, (h + 1) * att_per_mlstm):
            _attn_prompt_head(ha, q_ref, (k0_ref, k1_ref, k2_ref), (v0_ref, v1_ref, v2_ref), bias_ref, yb_ref)

    @pl.when(step == pl.num_programs(0) - 1)
    def _():
        c_out[...] = c_scr[...]
        n_out[...] = n_scr[...]
        m_out[...] = m_scr[...]


def _mixers_prompt(z, bias, gif, gnorm, layer):
    assert ATT_Q == MLSTM_L

    def z_spec(col):
        return pl.BlockSpec((ATT_Q, D_ATT), lambda j: (j, col))

    def kv_spec(col, back):
        return pl.BlockSpec((ATT_Q, D_ATT), lambda j: (jnp.maximum(j - back, 0), col))

    state_shapes = [
        jax.ShapeDtypeStruct((N_HEADS_MLSTM, HEAD_DIM_MLSTM, HEAD_DIM_MLSTM), F32),
        jax.ShapeDtypeStruct((N_HEADS_MLSTM, 1, HEAD_DIM_MLSTM), F32),
        jax.ShapeDtypeStruct((N_HEADS_MLSTM, 1, LANES), F32),
    ]
    return pl.pallas_call(
        _mixers_prompt_kernel,
        grid=(SEQ // ATT_Q,),
        in_specs=[
            z_spec(COL_Q),
            kv_spec(COL_K, 2), kv_spec(COL_K, 1), kv_spec(COL_K, 0),
            kv_spec(COL_V, 2), kv_spec(COL_V, 1), kv_spec(COL_V, 0),
            pl.BlockSpec((None, None, N_HEADS_ATT, ATT_Q, ATT_K),
                         lambda j: (layer, jnp.minimum(j, ATT_VARIANTS - 1), 0, 0, 0)),
            z_spec(COL_QM), z_spec(COL_KM), z_spec(COL_VM), z_spec(COL_OM),
            pl.BlockSpec((MLSTM_L, LANES), lambda j: (j, 0)),
            _layer_spec(layer, (1, D_MLSTM)),
        ],
        out_specs=[pl.BlockSpec((ATT_Q, D_ATT), lambda j: (j, 0)), pl.BlockSpec((MLSTM_L, D_MLSTM), lambda j: (j, 0))]
        + [_const_spec(s.shape) for s in state_shapes],
        out_shape=[jax.ShapeDtypeStruct((SEQ, D_ATT), BF16), jax.ShapeDtypeStruct((SEQ, D_MLSTM), BF16)]
        + state_shapes,
        scratch_shapes=[pltpu.VMEM(s.shape, F32) for s in state_shapes],
        compiler_params=_params(("arbitrary",)),
        name="mixers_prompt",
    )(z, z, z, z, z, z, z, bias, z, z, z, z, gif, gnorm)


def _mlstm_step_kernel(q_ref, k_ref, v_ref, om_ref, gif_ref, gn_ref, c_ref, n_ref, m_ref,
                       y_ref, c_out, n_out, m_out):
    log_gates = _log_gates(gif_ref[...])
    for h in range(N_HEADS_MLSTM):
        sl = slice(h * HEAD_DIM_MLSTM, (h + 1) * HEAD_DIM_MLSTM)
        y, c1, n1, m1 = _mlstm_block(q_ref[:, sl], k_ref[:, sl], v_ref[:, sl], om_ref[:, sl],
                                     _gate_views(log_gates, h), gn_ref[:, sl],
                                     c_ref[h], n_ref[h], m_ref[h][:, :1])
        y_ref[:, sl] = y.astype(y_ref.dtype)
        c_out[h] = c1
        n_out[h] = n1
        m_out[h] = jnp.broadcast_to(m1, (1, LANES))


def _mlstm_step(z, gif, gnorm, state_c, state_n, state_m, layer):
    first = SEQ // DEC_SEQ

    def z_spec(col):
        return pl.BlockSpec((DEC_SEQ, D_MLSTM), lambda b: (first + b, col))

    def st_in(shape):
        return pl.BlockSpec((None, None) + shape, lambda b: (layer, b) + (0,) * len(shape))

    def st_out(shape):
        return pl.BlockSpec((None,) + shape, lambda b: (b,) + (0,) * len(shape))

    shapes = [(N_HEADS_MLSTM, HEAD_DIM_MLSTM, HEAD_DIM_MLSTM), (N_HEADS_MLSTM, 1, HEAD_DIM_MLSTM),
              (N_HEADS_MLSTM, 1, LANES)]
    return pl.pallas_call(
        _mlstm_step_kernel,
        grid=(DEC_BATCH,),
        in_specs=[
            z_spec(COL_QM), z_spec(COL_KM), z_spec(COL_VM), z_spec(COL_OM),
            pl.BlockSpec((DEC_SEQ, LANES), lambda b: (first + b, 0)),
            _layer_spec(layer, (1, D_MLSTM)),
        ] + [st_in(s) for s in shapes],
        out_specs=[pl.BlockSpec((DEC_SEQ, D_MLSTM), lambda b: (b, 0))] + [st_out(s) for s in shapes],
        out_shape=[jax.ShapeDtypeStruct((N_SAMPLE, D_MLSTM), BF16)]
        + [jax.ShapeDtypeStruct((DEC_BATCH,) + s, F32) for s in shapes],
        compiler_params=_params(("parallel",)),
        name="mlstm_step",
    )(z, z, z, z, gif, gnorm, state_c, state_n, state_m)


HALO = 16
GROUPS_PER_TILE = ROW_TILE // DEC_SEQ


def _merge_kernel(xa_ref, gb_ref, gc_ref, xah_ref, gch_ref, s1_ref, s2_ref, cw_ref,
                  ybp_ref, ybs_ref, ycp_ref, ycs_ref, zg_ref, xp_ref, xs_ref,
                  wpc_ref, wpa_ref, wpm_ref, wout_ref, x_out, tail_out, u_scr):
    i = pl.program_id(0)
    is_s = i >= N_PROMPT_TILES
    row = lax.broadcasted_iota(jnp.int32, (ROW_TILE, 1), 0)
    pos = jnp.where(is_s, row % DEC_SEQ, row)
    u = gc_ref[...].astype(F32) * xa_ref[...].astype(F32)
    u_halo = gch_ref[...].astype(F32) * xah_ref[...].astype(F32)
    keep = jnp.logical_and(i > 0, jnp.logical_not(is_s))
    h1 = jnp.where(keep, u_halo[HALO - 1:HALO, :], 0.0)
    h2 = jnp.where(keep, u_halo[HALO - 2:HALO - 1, :], 0.0)
    f1 = jnp.where(is_s, s1_ref[...], h1)
    f2 = jnp.where(is_s, s2_ref[...], jnp.where(row == 0, h2, h1))
    u_m1 = jnp.where(pos >= 1, pltpu.roll(u, 1, 0), f1)
    u_m2 = jnp.where(pos >= 2, pltpu.roll(u, 2, 0), f2)
    cw = cw_ref[...]
    y = cw[0:1, :] * u_m2 + cw[1:2, :] * u_m1 + cw[2:3, :] * u
    ya = gb_ref[...].astype(F32) * y

    u_scr[...] = u
    for g in range(GROUPS_PER_TILE):
        for j in range(CONV_W - 1):
            src = (g + 1) * DEC_SEQ - (CONV_W - 1) + j
            tail_out[j, g:g + 1, :] = u_scr[src:src + 1, :]

    yb = jnp.where(is_s, ybs_ref[...], ybp_ref[...])
    yc = jnp.where(is_s, ycs_ref[...], ycp_ref[...])
    g = _sigmoid(zg_ref[...].astype(F32))
    merged = (g[:, 0:D_MODEL] * _dot(ya.astype(BF16), wpc_ref[...])
              + g[:, D_MODEL:2 * D_MODEL] * _dot(yb, wpa_ref[...])
              + g[:, 2 * D_MODEL:3 * D_MODEL] * _dot(yc, wpm_ref[...]))
    x = jnp.where(is_s, xs_ref[...], xp_ref[...])
    x_out[...] = x + _dot(merged.astype(BF16), wout_ref[...])


def _merge(z, zg, s1, s2, conv_w, yb_p, yb_s, yc_p, yc_s, x_p, x_s, x_s_block, wpc, wpa, wpm, wout, layer):
    halo_blocks = ROW_TILE // HALO

    def z_spec(col):
        return pl.BlockSpec((ROW_TILE, D_CONV), lambda i: (i, col))

    def halo_spec(col):
        return pl.BlockSpec((HALO, D_CONV), lambda i: (jnp.maximum(i * halo_blocks - 1, 0), col))

    def prompt_spec(width):
        return pl.BlockSpec((ROW_TILE, width), lambda i: (jnp.minimum(i, N_PROMPT_TILES - 1), 0))

    def weight_spec(shape):
        return _layer_spec(layer, shape, pipeline_mode=pl.Buffered(1))

    return pl.pallas_call(
        _merge_kernel,
        grid=(N_TILES,),
        in_specs=[
            z_spec(COL_XA), z_spec(COL_GB), z_spec(COL_GC), halo_spec(COL_XA), halo_spec(COL_GC),
            _layer_spec(layer, (ROW_TILE, D_CONV)), _layer_spec(layer, (ROW_TILE, D_CONV)),
            _layer_spec(layer, (CONV_W, D_CONV)),
            prompt_spec(D_ATT), _const_spec((ROW_TILE, D_ATT)),
            prompt_spec(D_MLSTM), _const_spec((ROW_TILE, D_MLSTM)),
            _row_spec(N_BRANCH * D_MODEL),
            prompt_spec(D_MODEL), pl.BlockSpec((ROW_TILE, D_MODEL), lambda i: (x_s_block, 0)),
            weight_spec((D_CONV, D_MODEL)), weight_spec((D_ATT, D_MODEL)), weight_spec((D_MLSTM, D_MODEL)),
            weight_spec((D_MODEL, D_MODEL)),
        ],
        out_specs=[
            _row_spec(D_MODEL),
            pl.BlockSpec((None, CONV_W - 1, GROUPS_PER_TILE, D_CONV), lambda i: (i, 0, 0, 0)),
        ],
        out_shape=[
            jax.ShapeDtypeStruct((N_ROWS, D_MODEL), F32),
            jax.ShapeDtypeStruct((N_TILES, CONV_W - 1, GROUPS_PER_TILE, D_CONV), F32),
        ],
        scratch_shapes=[pltpu.VMEM((ROW_TILE, D_CONV), F32)],
        compiler_params=_params(("parallel",)),
        name="merge",
    )(z, z, z, z, z, s1, s2, conv_w, yb_p, yb_s, yc_p, yc_s, zg, x_p, x_s, wpc, wpa, wpm, wout)


def _granule_copies(idx_ref, first, n, src_hbm, buf, sem, slot):
    return [
        pltpu.make_async_copy(
            src_hbm.at[pl.ds(pl.multiple_of(idx_ref[first + k] * GRAN, GRAN), GRAN), :],
            buf.at[slot, pl.ds(k * GRAN, GRAN), :],
            sem.at[slot])
        for k in range(n)
    ]


GATHER_AHEAD = 2
GATHER_SLOTS = GATHER_AHEAD + 1


def _prefetched_gather(idx_ref, n, src_hbm, buf, sem, first_group=0):
    step = pl.program_id(0)
    n_steps = pl.num_programs(0)

    def request(ahead):
        slot = (step + ahead) % GATHER_SLOTS
        for cp in _granule_copies(idx_ref, (first_group + step + ahead) * n, n, src_hbm, buf, sem, slot):
            cp.start()

    for ahead in range(GATHER_AHEAD):
        pl.when(jnp.logical_and(step == 0, ahead < n_steps))(functools.partial(request, ahead))
    pl.when(step + GATHER_AHEAD < n_steps)(functools.partial(request, GATHER_AHEAD))

    slot = step % GATHER_SLOTS
    for cp in _granule_copies(idx_ref, (first_group + step) * n, n, src_hbm, buf, sem, slot):
        cp.wait()
    return slot


def _expert_ffn_kernel(src_ref, expert_ref, first_ref, valid_ref, next_ref, xs_hbm, wg_hbm, wu_hbm, wd_hbm,
                       o_ref, buf, sem, wg_st, wu_st, wd_st, wsem, wg_bf, wu_bf, wd_bf, *, layer):
    c = pl.program_id(0)
    slot = _prefetched_gather(src_ref, CHUNK_GRANS, xs_hbm, buf, sem)

    def weight_copies(e):
        return [pltpu.make_async_copy(wg_hbm.at[layer, e], wg_st, wsem.at[0]),
                pltpu.make_async_copy(wu_hbm.at[layer, e], wu_st, wsem.at[1]),
                pltpu.make_async_copy(wd_hbm.at[layer, e], wd_st, wsem.at[2])]

    @pl.when(c == 0)
    def _():
        for cp in weight_copies(expert_ref[0]):
            cp.start()

    @pl.when(first_ref[c] == 1)
    def _():
        for cp in weight_copies(expert_ref[c]):
            cp.wait()
        wg_bf[...] = wg_st[...].astype(BF16)
        wu_bf[...] = wu_st[...].astype(BF16)
        wd_bf[...] = wd_st[...].astype(BF16)

        @pl.when(next_ref[c] >= 0)
        def _():
            for cp in weight_copies(next_ref[c]):
                cp.start()

    @pl.when(valid_ref[c] == 1)
    def _():
        rows = buf[slot]
        x = rows[:, :D_MODEL]
        meta = rows[:, D_MODEL:].astype(F32)
        w_first = meta[:, 1:2] + meta[:, 2:3] + meta[:, 3:4]
        w_second = meta[:, 5:6] + meta[:, 6:7] + meta[:, 7:8]
        w = jnp.where(meta[:, 0:1] == expert_ref[c].astype(F32), w_first, w_second)
        gate = _dot(x, wg_bf[...])
        a = gate * _sigmoid(gate) * _dot(x, wu_bf[...]) * w
        o_ref[...] = _dot(a.astype(BF16), wd_bf[...]).astype(o_ref.dtype)

    @pl.when(valid_ref[c] == 0)
    def _():
        o_ref[...] = jnp.zeros_like(o_ref)


def _expert_ffn(src, chunk_expert, chunk_first, chunk_valid, chunk_next, xs, w_gate, w_up, w_down, layer):
    any_spec = pl.BlockSpec(memory_space=pl.ANY)
    grid_spec = pltpu.PrefetchScalarGridSpec(
        num_scalar_prefetch=5,
        grid=(MAX_CHUNKS,),
        in_specs=[any_spec, any_spec, any_spec, any_spec],
        out_specs=pl.BlockSpec((CHUNK_ROWS, D_MODEL), lambda c, *_: (c, 0)),
        scratch_shapes=[
            pltpu.VMEM((GATHER_SLOTS, CHUNK_ROWS, XS_WIDTH), BF16),
            pltpu.SemaphoreType.DMA((GATHER_SLOTS,)),
            pltpu.VMEM((D_MODEL, D_EXPERT), F32),
            pltpu.VMEM((D_MODEL, D_EXPERT), F32),
            pltpu.VMEM((D_EXPERT, D_MODEL), F32),
            pltpu.SemaphoreType.DMA((3,)),
            pltpu.VMEM((D_MODEL, D_EXPERT), BF16),
            pltpu.VMEM((D_MODEL, D_EXPERT), BF16),
            pltpu.VMEM((D_EXPERT, D_MODEL), BF16),
        ],
    )
    return pl.pallas_call(
        functools.partial(_expert_ffn_kernel, layer=layer),
        grid_spec=grid_spec,
        out_shape=jax.ShapeDtypeStruct((MAX_CHUNKS * CHUNK_ROWS, D_MODEL), BF16),
        compiler_params=_params(("arbitrary",)),
        name="expert_ffn",
    )(src, chunk_expert, chunk_first, chunk_valid, chunk_next, xs, w_gate, w_up, w_down)


def _combine(dst_ref, ys_hbm, lp_ref, x_ref, buf, sem, first_tile=0):
    slot = _prefetched_gather(dst_ref, TILE_GRANS, ys_hbm, buf, sem, first_tile)
    lp = lp_ref[...]
    p_idx = lax.broadcasted_iota(jnp.int32, (ROW_TILE, LOCAL_ROWS), 1).astype(F32)
    pick = jnp.where((p_idx == lp[:, 0:1]) | (p_idx == lp[:, 1:2]), 1.0, 0.0).astype(BF16)
    return x_ref[...] + _dot(pick, buf[slot])


def _combine_norm_kernel(dst_ref, ys_hbm, lp_ref, x_ref, g_ref, whi_ref, wlo_ref, b_ref,
                         x_out, h_out, s_out, buf, sem):
    x = _combine(dst_ref, ys_hbm, lp_ref, x_ref, buf, sem)
    x_out[...] = x
    h = _rms(x, g_ref[...])
    h_out[...] = h.astype(BF16)
    s_out[...] = _small_proj(h, whi_ref, wlo_ref, b_ref)


def _combine_final_kernel(dst_ref, ys_hbm, lp_ref, x_ref, g_ref, o_ref, buf, sem, *, first_tile):
    o_ref[...] = _rms(_combine(dst_ref, ys_hbm, lp_ref, x_ref, buf, sem, first_tile), g_ref[...])


_COMBINE_SCRATCH = [pltpu.VMEM((GATHER_SLOTS, LOCAL_ROWS, D_MODEL), BF16),
                    pltpu.SemaphoreType.DMA((GATHER_SLOTS,))]


def _combine_norm(dst, ys, lp, x, g, whi, wlo, b, layer):
    def row(width):
        return pl.BlockSpec((ROW_TILE, width), lambda i, dst: (i, 0))

    grid_spec = pltpu.PrefetchScalarGridSpec(
        num_scalar_prefetch=1,
        grid=(N_TILES,),
        in_specs=[pl.BlockSpec(memory_space=pl.ANY), row(LANES), row(D_MODEL)] + _norm_param_specs(layer),
        out_specs=[row(D_MODEL), row(D_MODEL), row(LANES)],
        scratch_shapes=_COMBINE_SCRATCH,
    )
    return pl.pallas_call(
        _combine_norm_kernel,
        grid_spec=grid_spec,
        out_shape=[
            jax.ShapeDtypeStruct((N_ROWS, D_MODEL), F32),
            jax.ShapeDtypeStruct((N_ROWS, D_MODEL), BF16),
            jax.ShapeDtypeStruct((N_ROWS, LANES), F32),
        ],
        compiler_params=_params(("arbitrary",)),
        name="combine_norm",
    )(dst, ys, lp, x, g, whi, wlo, b)


def _combine_final(dst, ys, lp, x, g, first_tile, n_tiles):
    def row(width):
        return pl.BlockSpec((ROW_TILE, width), lambda i, dst: (i + first_tile, 0))

    grid_spec = pltpu.PrefetchScalarGridSpec(
        num_scalar_prefetch=1,
        grid=(n_tiles,),
        in_specs=[pl.BlockSpec(memory_space=pl.ANY), row(LANES), row(D_MODEL),
                  pl.BlockSpec((1, D_MODEL), lambda i, dst: (0, 0))],
        out_specs=pl.BlockSpec((ROW_TILE, D_MODEL), lambda i, dst: (i, 0)),
        scratch_shapes=_COMBINE_SCRATCH,
    )
    return pl.pallas_call(
        functools.partial(_combine_final_kernel, first_tile=first_tile),
        grid_spec=grid_spec,
        out_shape=jax.ShapeDtypeStruct((n_tiles * ROW_TILE, D_MODEL), F32),
        compiler_params=_params(("arbitrary",)),
        name="combine_final",
    )(dst, ys, lp, x, g)


N_SRC = MAX_CHUNKS * CHUNK_GRANS
N_DST = N_TILES * TILE_GRANS


def _granule_kernel(gcnt_ref, src_ref, dst_ref, expert_ref, first_ref, valid_ref, next_ref, local_ref):
    def fill(ref, n, value):
        def body(k, carry):
            ref[k] = value
            return carry
        lax.fori_loop(0, n, body, 0)

    fill(src_ref, N_SRC, TILE_GRANS - 1)
    fill(dst_ref, N_DST, 0)
    fill(expert_ref, MAX_CHUNKS, N_EXPERTS - 1)
    fill(first_ref, MAX_CHUNKS, 0)
    fill(valid_ref, MAX_CHUNKS, 0)
    fill(next_ref, MAX_CHUNKS + 1, -1)
    fill(local_ref, N_TILES, 0)

    def per_expert(e, carry):
        chunk, prev_first = carry
        slot0 = chunk * CHUNK_GRANS

        def per_tile(t, pos):
            n = gcnt_ref[t * N_EXPERTS + e]
            local0 = t * TILE_GRANS + local_ref[t]

            def per_granule(k, c2):
                src_ref[pos + k] = local0 + k
                dst_ref[local0 + k] = pos + k
                return c2

            lax.fori_loop(0, n, per_granule, 0)
            local_ref[t] = local_ref[t] + n
            return pos + n

        end = lax.fori_loop(0, N_TILES, per_tile, slot0)
        n_chunks = (end - slot0 + CHUNK_GRANS - 1) // CHUNK_GRANS

        def per_chunk(k, c2):
            expert_ref[chunk + k] = e
            valid_ref[chunk + k] = 1
            return c2

        lax.fori_loop(0, n_chunks, per_chunk, 0)
        owns = n_chunks > 0

        @pl.when(owns)
        def _():
            first_ref[chunk] = 1
            next_ref[prev_first] = e

        return chunk + n_chunks, jnp.where(owns, chunk, prev_first)

    lax.fori_loop(0, N_EXPERTS, per_expert, (0, MAX_CHUNKS))


def _granule_lists(gcnt):
    smem = pl.BlockSpec(memory_space=pltpu.SMEM)
    sizes = (N_SRC, N_DST, MAX_CHUNKS, MAX_CHUNKS, MAX_CHUNKS, MAX_CHUNKS + 1)
    return pl.pallas_call(
        _granule_kernel,
        in_specs=[smem],
        out_specs=[smem] * len(sizes),
        out_shape=[jax.ShapeDtypeStruct((n,), jnp.int32) for n in sizes],
        scratch_shapes=[pltpu.SMEM((N_TILES,), jnp.int32)],
        name="granule_lists",
    )(gcnt.reshape(N_TILES * N_EXPERTS))


def _pad_lanes(w):
    return jnp.pad(w, [(0, 0)] * (w.ndim - 1) + [(0, LANES - w.shape[-1])])


def _toeplitz_bias(table, n_q, n_k, reach):
    period = 1
    while period < n_q + n_k:
        period *= 2
    j = jnp.arange(period)
    d = jnp.where(j < n_k, j, j - period)
    u = table.reshape(-1, table.shape[-1])[:, jnp.clip(reach - d, -REL_CLIP, REL_CLIP) + REL_CLIP]
    flat = jnp.tile(u, (1, n_q))[:, :n_q * (period - 1)]
    return flat.reshape(table.shape[:-1] + (n_q, period - 1))[..., :n_k]


def _prompt_bias(table):
    bias = _toeplitz_bias(table, ATT_Q, ATT_K, ATT_REACH)
    qc = jnp.arange(ATT_Q)[:, None] // CHUNK
    m = jnp.arange(ATT_K)[None, :]
    kc = m // CHUNK
    band = (kc >= qc) & (kc <= qc + BAND_CHUNKS)
    first_key = ATT_REACH - ATT_Q * jnp.arange(ATT_VARIANTS)
    ok = band[None] & (m[None] >= first_key[:, None, None])
    return jnp.where(ok[None, :, None], bias[:, None], NEG)


def _step_bias(table, past):
    bias = _toeplitz_bias(table, DEC_SEQ, past + DEC_SEQ, past)
    return bias[..., :past], bias[..., past:]


def kernel(x_prompt, x_sample, state_conv, cache_k, cache_v, state_C, state_n, state_m, norm_mix, norm_ffn, w_in, conv_w, rel_bias, gate_bias, mlstm_norm, w_proj_conv, w_proj_att, w_proj_mlstm, w_out, router_group, router_group_bias, router_expert, router_expert_bias, w_gate, w_up, w_down, norm_final):
    past = cache_k.shape[2]
    xp = x_prompt.reshape(SEQ, D_MODEL)
    xs = x_sample.reshape(N_SAMPLE, D_MODEL)
    state_n5 = state_n.reshape(DEPTH, DEC_BATCH, N_HEADS_MLSTM, 1, HEAD_DIM_MLSTM)
    state_m5 = jnp.broadcast_to(state_m[..., None, None], (DEPTH, DEC_BATCH, N_HEADS_MLSTM, 1, LANES))

    w_in_t = jnp.swapaxes(w_in, 1, 2)

    g_mix = norm_mix[:, None, :]
    g_ffn = norm_ffn[:, None, :]
    gnorm = mlstm_norm[:, None, :]
    gif_w = jnp.swapaxes(w_in_t[:, MAIN_WIDTH:MAIN_WIDTH + GIF_WIDTH, :], 1, 2)
    gif_hi, gif_lo = _split_hi_lo(_pad_lanes(gif_w))
    gif_b = _pad_lanes(gate_bias[:, None, :])
    r_hi, r_lo = _split_hi_lo(_pad_lanes(jnp.concatenate([router_group, router_expert], axis=2)))
    r_b = _pad_lanes(jnp.concatenate([router_group_bias, router_expert_bias], axis=1)[:, None, :])
    bias_prompt = _prompt_bias(rel_bias)
    bias_past, bias_new = _step_bias(rel_bias, past)
    hist = jnp.zeros((DEPTH, DEC_BATCH, DEC_SEQ, D_CONV), F32)
    s1 = hist.at[:, :, 0].set(state_conv[:, :, 1]).reshape(DEPTH, N_SAMPLE, D_CONV)
    s2 = (hist.at[:, :, 0].set(state_conv[:, :, 0]).at[:, :, 1].set(state_conv[:, :, 1])
          .reshape(DEPTH, N_SAMPLE, D_CONV))
    merge_w = (w_proj_conv.astype(BF16), w_proj_att.astype(BF16), w_proj_mlstm.astype(BF16), w_out.astype(BF16))

    x_mid = moe = None
    outs = {k: [] for k in ("p_conv", "p_k", "p_v", "p_C", "p_n", "p_m", "s_conv", "s_k", "s_v", "s_C", "s_n", "s_m")}
    for l in range(DEPTH):
        if l == 0:
            h, gif = _norm_first(xp, xs, g_mix, gif_hi, gif_lo, gif_b, l)
            residual = (xp, xs, 0)
        else:
            x, h, gif = _combine_norm(*moe, x_mid, g_mix, gif_hi, gif_lo, gif_b, l)
            residual = (x, x, N_PROMPT_TILES)
        z = _project(h, w_in_t, l, 0, MAIN_WIDTH)
        zg = _project(h, w_in_t, l, MAIN_WIDTH + GIF_WIDTH, N_BRANCH * D_MODEL)

        yb_p, yc_p, p_c, p_n, p_m = _mixers_prompt(z, bias_prompt, gif, gnorm, l)
        yb_s = _attn_step(z, cache_k, cache_v, bias_past, bias_new, l)
        yc_s, s_c, s_n, s_m = _mlstm_step(z, gif, gnorm, state_C, state_n5, state_m5, l)

        x_mid, tails = _merge(z, zg, s1, s2, conv_w, yb_p, yb_s, yc_p, yc_s, *residual, *merge_w, l)

        xs_local, lp, gcnt = _route_dispatch(x_mid, g_ffn, r_hi, r_lo, r_b, l)
        gcnt = gcnt[:, 0, N_GROUPS:N_GROUPS + N_EXPERTS].astype(jnp.int32)
        src, dst, chunk_expert, chunk_first, chunk_valid, chunk_next = _granule_lists(gcnt)
        ys = _expert_ffn(src, chunk_expert, chunk_first, chunk_valid, chunk_next, xs_local, w_gate, w_up, w_down, l)
        moe = (dst, ys, lp)

        keep = min(ATT_REACH, SEQ)
        k_rows = z[SEQ - keep:, COL_K * D_ATT:(COL_K + 1) * D_ATT].astype(F32)
        v_rows = z[SEQ - keep:, COL_V * D_ATT:(COL_V + 1) * D_ATT].astype(F32)
        outs["p_conv"].append(tails[N_PROMPT_TILES - 1, :, GROUPS_PER_TILE - 1][None])
        outs["p_k"].append(k_rows[:keep].reshape(1, keep, N_HEADS_ATT, HEAD_DIM_ATT))
        outs["p_v"].append(v_rows[:keep].reshape(1, keep, N_HEADS_ATT, HEAD_DIM_ATT))
        outs["p_C"].append(p_c[None])
        outs["p_n"].append(p_n[:, 0][None])
        outs["p_m"].append(p_m[:, 0, 0][None])
        outs["s_conv"].append(jnp.swapaxes(tails[N_PROMPT_TILES], 0, 1))
        outs["s_k"].append(k_rows[keep:].reshape(DEC_BATCH, DEC_SEQ, N_HEADS_ATT, HEAD_DIM_ATT))
        outs["s_v"].append(v_rows[keep:].reshape(DEC_BATCH, DEC_SEQ, N_HEADS_ATT, HEAD_DIM_ATT))
        outs["s_C"].append(s_c)
        outs["s_n"].append(s_n[:, :, 0])
        outs["s_m"].append(s_m[:, :, 0, 0])

    g_fin = norm_final[None, :]
    y_prompt = _combine_final(*moe, x_mid, g_fin, 0, N_PROMPT_TILES).reshape(x_prompt.shape)
    y_sample = _combine_final(*moe, x_mid, g_fin, N_PROMPT_TILES, 1).reshape(x_sample.shape)
    st = {k: jnp.stack(v) for k, v in outs.items()}
    return (y_prompt, y_sample, st["p_conv"], st["p_k"], st["p_v"], st["p_C"], st["p_n"], st["p_m"],
            st["s_conv"], st["s_k"], st["s_v"], st["s_C"], st["s_n"], st["s_m"])
```

```python
import functools
import math

import jax
import jax.numpy as jnp
from jax import lax
from jax.experimental import pallas as pl
from jax.experimental.pallas import tpu as pltpu

D_MODEL = 2048
SEQ = 8192
DEPTH = 2
DEC_BATCH = 8
DEC_SEQ = 32
N_SAMPLE = DEC_BATCH * DEC_SEQ
N_ROWS = SEQ + N_SAMPLE

CHUNK = 64
D_CONV = 1024
CONV_W = 3
N_HEADS_ATT = 8
HEAD_DIM_ATT = 128
D_ATT = N_HEADS_ATT * HEAD_DIM_ATT
BAND_CHUNKS = 8
ATT_REACH = BAND_CHUNKS * CHUNK
REL_CLIP = 128
N_HEADS_MLSTM = 4
HEAD_DIM_MLSTM = 256
D_MLSTM = N_HEADS_MLSTM * HEAD_DIM_MLSTM
N_BRANCH = 3
MAIN_WIDTH = 3 * D_CONV + 3 * D_ATT + 4 * D_MLSTM
GIF_WIDTH = 2 * N_HEADS_MLSTM
N_GROUPS = 4
EXPERTS_PER_GROUP = 4
N_EXPERTS = N_GROUPS * EXPERTS_PER_GROUP
D_EXPERT = 512
EPS = 1e-6

LANES = 128
BF16_ROWS = 16
ROW_TILE = 256
N_TILES = N_ROWS // ROW_TILE
N_PROMPT_TILES = SEQ // ROW_TILE
MM_ROWS = 1056
MM_COLS = 1024
ATT_Q = 256
ATT_K = ATT_Q + ATT_REACH
ATT_VARIANTS = ATT_REACH // ATT_Q + 1
MLSTM_L = 256
NEG = -1e30
VMEM_LIMIT = 56 * 1024 * 1024

GRAN = BF16_ROWS
XS_WIDTH = D_MODEL + LANES
MAX_RUN_PAD = N_EXPERTS * (GRAN - 1)
TILE_GRANS = (2 * ROW_TILE + MAX_RUN_PAD) // GRAN + 1
LOCAL_ROWS = TILE_GRANS * GRAN
CHUNK_GRANS = 16
CHUNK_ROWS = CHUNK_GRANS * GRAN
MAX_CHUNKS = (N_TILES * (TILE_GRANS - 1)) // CHUNK_GRANS + N_EXPERTS

COL_XA, COL_GB, COL_GC, COL_Q, COL_K, COL_V, COL_QM, COL_KM, COL_VM, COL_OM = range(10)

F32 = jnp.float32
BF16 = jnp.bfloat16


def _params(sem):
    return pltpu.CompilerParams(dimension_semantics=sem, vmem_limit_bytes=VMEM_LIMIT)


def _dot(a, b):
    return jnp.dot(a, b, preferred_element_type=F32)


def _dot_nt(a, b):
    return lax.dot_general(a, b, (((1,), (1,)), ((), ())), preferred_element_type=F32)


def _dot_tn(a, b):
    return lax.dot_general(a, b, (((0,), (0,)), ((), ())), preferred_element_type=F32)


def _split_hi_lo(w):
    hi = w.astype(BF16)
    lo = (w - hi.astype(F32)).astype(BF16)
    return hi, lo


def _sigmoid(x):
    return 1.0 / (1.0 + jnp.exp(-x))


def _log_sigmoid(x):
    return jnp.minimum(x, 0.0) - jnp.log1p(jnp.exp(-jnp.abs(x)))


def _rms(x, g):
    ms = jnp.mean(x * x, axis=-1, keepdims=True)
    return x * lax.rsqrt(ms + EPS) * g


def _small_proj(h, whi_ref, wlo_ref, b_ref):
    h_hi = h.astype(BF16)
    h_lo = (h - h_hi.astype(F32)).astype(BF16)
    whi = whi_ref[...]
    return _dot(h_hi, whi) + _dot(h_lo, whi) + _dot(h_hi, wlo_ref[...]) + b_ref[...]


def _norm_first_kernel(xp_ref, xs_ref, g_ref, whi_ref, wlo_ref, b_ref, h_out, s_out):
    i = pl.program_id(0)
    x = jnp.where(i < N_PROMPT_TILES, xp_ref[...], xs_ref[...])
    h = _rms(x, g_ref[...])
    h_out[...] = h.astype(BF16)
    s_out[...] = _small_proj(h, whi_ref, wlo_ref, b_ref)


def _route(logits):
    lane = lax.broadcasted_iota(jnp.int32, logits.shape, 1)
    is_g = lane < N_GROUPS
    gl = jnp.where(is_g, logits, NEG)
    gmax = jnp.max(gl, axis=1, keepdims=True)
    g_sel = jnp.min(jnp.where(is_g & (gl == gmax), lane, LANES), axis=1, keepdims=True)
    p_g = 1.0 / jnp.sum(jnp.where(is_g, jnp.exp(gl - gmax), 0.0), axis=1, keepdims=True)
    e_lane = lane - N_GROUPS
    in_g = (e_lane >= 0) & (e_lane < N_EXPERTS) & ((e_lane // EXPERTS_PER_GROUP) == g_sel)
    e1 = jnp.max(jnp.where(in_g, logits, NEG), axis=1, keepdims=True)
    i1 = jnp.min(jnp.where(in_g & (logits == e1), lane, LANES), axis=1, keepdims=True)
    rest = in_g & (lane != i1)
    e2 = jnp.max(jnp.where(rest, logits, NEG), axis=1, keepdims=True)
    i2 = jnp.min(jnp.where(rest & (logits == e2), lane, LANES), axis=1, keepdims=True)
    r = jnp.exp(e2 - e1)
    w1 = p_g / (1.0 + r)
    w2 = w1 * r
    return i1, i2, w1, w2


def _bf16_pieces(w):
    a = w.astype(BF16).astype(F32)
    b = (w - a).astype(BF16).astype(F32)
    return a, b, w - a - b


def _route_dispatch_rows(x, g_ref, whi_ref, wlo_ref, b_ref, xs_out, lp_out, cnt_out):
    h = _rms(x, g_ref[...])
    i1, i2, w1, w2 = _route(_small_proj(h, whi_ref, wlo_ref, b_ref))
    lane = lax.broadcasted_iota(jnp.int32, (ROW_TILE, LANES), 1)
    o1 = lane == i1
    o2 = lane == i2
    onehot = jnp.where(o1 | o2, 1.0, 0.0)
    t_idx = lax.broadcasted_iota(jnp.int32, (ROW_TILE, ROW_TILE), 0)
    s_idx = lax.broadcasted_iota(jnp.int32, (ROW_TILE, ROW_TILE), 1)
    earlier = jnp.where(s_idx < t_idx, 1.0, 0.0).astype(BF16)
    rank = _dot(earlier, onehot.astype(BF16))
    gcnt = jnp.floor((jnp.sum(onehot, axis=0, keepdims=True) + (GRAN - 1)) * (1.0 / GRAN))
    a_idx = lax.broadcasted_iota(jnp.int32, (LANES, LANES), 0)
    b_idx = lax.broadcasted_iota(jnp.int32, (LANES, LANES), 1)
    before = jnp.where(a_idx < b_idx, 1.0, 0.0).astype(BF16)
    gcnt8 = jnp.broadcast_to(gcnt, (8, LANES))
    run_start = _dot(gcnt8.astype(BF16), before)[0:1, :] * GRAN
    pos = run_start + rank
    lpos1 = jnp.sum(jnp.where(o1, pos, 0.0), axis=1, keepdims=True)
    lpos2 = jnp.sum(jnp.where(o2, pos, 0.0), axis=1, keepdims=True)
    lp = jnp.where(lane == 0, lpos1, jnp.where(lane == 1, lpos2, -1.0))
    lp_out[...] = lp
    cnt_out[...] = gcnt8

    meta = jnp.zeros((ROW_TILE, LANES), F32)
    fields = ((i1 - N_GROUPS).astype(F32),) + _bf16_pieces(w1) + ((i2 - N_GROUPS).astype(F32),) + _bf16_pieces(w2)
    for k, val in enumerate(fields):
        meta = jnp.where(lane == k, val, meta)
    rows = jnp.concatenate([h.astype(BF16), meta.astype(BF16)], axis=1)
    lp_t = lp.T
    p_idx = lax.broadcasted_iota(jnp.int32, (LOCAL_ROWS, ROW_TILE), 0).astype(F32)
    perm = jnp.where((p_idx == lp_t[0:1, :]) | (p_idx == lp_t[1:2, :]), 1.0, 0.0).astype(BF16)
    xs_out[...] = _dot(perm, rows).astype(BF16)


def _row_spec(width):
    return pl.BlockSpec((ROW_TILE, width), lambda i: (i, 0))


def _const_spec(shape):
    return pl.BlockSpec(shape, lambda i: (0,) * len(shape))


def _layer_spec(layer, shape, **kwargs):
    return pl.BlockSpec((None,) + tuple(shape), lambda *_: (layer,) + (0,) * len(shape), **kwargs)


def _norm_param_specs(layer):
    return [_layer_spec(layer, (1, D_MODEL)), _layer_spec(layer, (D_MODEL, LANES)),
            _layer_spec(layer, (D_MODEL, LANES)), _layer_spec(layer, (1, LANES))]


def _norm_first(xp, xs, g, whi, wlo, b, layer):
    return pl.pallas_call(
        _norm_first_kernel,
        grid=(N_TILES,),
        in_specs=[
            pl.BlockSpec((ROW_TILE, D_MODEL), lambda i: (jnp.minimum(i, N_PROMPT_TILES - 1), 0)),
            _const_spec((ROW_TILE, D_MODEL)),
        ] + _norm_param_specs(layer),
        out_specs=[_row_spec(D_MODEL), _row_spec(LANES)],
        out_shape=[
            jax.ShapeDtypeStruct((N_ROWS, D_MODEL), BF16),
            jax.ShapeDtypeStruct((N_ROWS, LANES), F32),
        ],
        compiler_params=_params(("parallel",)),
        name="norm_first",
    )(xp, xs, g, whi, wlo, b)


def _route_dispatch_kernel(x_ref, g_ref, whi_ref, wlo_ref, b_ref, xs_out, lp_out, cnt_out):
    _route_dispatch_rows(x_ref[...], g_ref, whi_ref, wlo_ref, b_ref, xs_out, lp_out, cnt_out)


def _route_dispatch(x, g, whi, wlo, b, layer):
    return pl.pallas_call(
        _route_dispatch_kernel,
        grid=(N_TILES,),
        in_specs=[_row_spec(D_MODEL)] + _norm_param_specs(layer),
        out_specs=[
            pl.BlockSpec((LOCAL_ROWS, XS_WIDTH), lambda i: (i, 0)),
            _row_spec(LANES),
            pl.BlockSpec((None, 8, LANES), lambda i: (i, 0, 0)),
        ],
        out_shape=[
            jax.ShapeDtypeStruct((N_TILES * LOCAL_ROWS, XS_WIDTH), BF16),
            jax.ShapeDtypeStruct((N_ROWS, LANES), F32),
            jax.ShapeDtypeStruct((N_TILES, 8, LANES), F32),
        ],
        compiler_params=_params(("parallel",)),
        name="route_dispatch",
    )(x, g, whi, wlo, b)


def _mm_kernel(h_ref, wt_ref, o_ref, wb_ref):
    @pl.when(pl.program_id(1) == 0)
    def _():
        wb_ref[...] = wt_ref[0].T.astype(BF16)

    o_ref[...] = _dot(h_ref[...], wb_ref[...]).astype(o_ref.dtype)


def _project(h, wt, layer, row0, n_cols):
    return pl.pallas_call(
        _mm_kernel,
        grid=(n_cols // MM_COLS, N_ROWS // MM_ROWS),
        in_specs=[
            pl.BlockSpec((MM_ROWS, D_MODEL), lambda j, i: (i, 0)),
            pl.BlockSpec((pl.Element(1), pl.Element(MM_COLS), pl.Element(D_MODEL)),
                         lambda j, i: (layer, pl.multiple_of(row0 + j * MM_COLS, 8), 0)),
        ],
        out_specs=pl.BlockSpec((MM_ROWS, MM_COLS), lambda j, i: (i, j)),
        out_shape=jax.ShapeDtypeStruct((N_ROWS, n_cols), BF16),
        scratch_shapes=[pltpu.VMEM((D_MODEL, MM_COLS), BF16)],
        compiler_params=_params(("parallel", "arbitrary")),
        name="project",
    )(h, wt)


def _attn_prompt_head(h, q_ref, k_refs, v_refs, bias_ref, o_ref):
    sl = slice(h * HEAD_DIM_ATT, (h + 1) * HEAD_DIM_ATT)
    q = q_ref[:, sl]
    kk = jnp.concatenate([r[:, sl] for r in k_refs], axis=0)
    vv = jnp.concatenate([r[:, sl] for r in v_refs], axis=0)
    s = _dot_nt(q, kk) * (HEAD_DIM_ATT ** -0.5) + bias_ref[h]
    mx = jnp.max(s, axis=1, keepdims=True)
    p = jnp.exp(s - mx).astype(BF16)
    od = _dot(p, jnp.concatenate([vv, jnp.ones((ATT_K, HEAD_DIM_ATT), BF16)], axis=1))
    o = od[:, :HEAD_DIM_ATT] * (1.0 / od[:, HEAD_DIM_ATT:HEAD_DIM_ATT + 1])
    o_ref[:, sl] = o.astype(o_ref.dtype)


def _attn_step_kernel(q_ref, k_ref, v_ref, ck_ref, cv_ref, bp_ref, bn_ref, o_ref):
    scale = HEAD_DIM_ATT ** -0.5
    ck = jnp.swapaxes(ck_ref[...], 0, 1).astype(BF16)
    cv = jnp.swapaxes(cv_ref[...], 0, 1).astype(BF16)
    for h in range(N_HEADS_ATT):
        sl = slice(h * HEAD_DIM_ATT, (h + 1) * HEAD_DIM_ATT)
        q = q_ref[:, sl]
        s_past = _dot_nt(q, ck[h]) * scale + bp_ref[h]
        s_new = _dot_nt(q, k_ref[:, sl]) * scale + bn_ref[h]
        mx = jnp.maximum(jnp.max(s_past, axis=1, keepdims=True), jnp.max(s_new, axis=1, keepdims=True))
        p_past = jnp.exp(s_past - mx)
        p_new = jnp.exp(s_new - mx)
        den = jnp.sum(p_past, axis=1, keepdims=True) + jnp.sum(p_new, axis=1, keepdims=True)
        o = _dot(p_past.astype(BF16), cv[h]) + _dot(p_new.astype(BF16), v_ref[:, sl])
        o_ref[:, sl] = (o / den).astype(o_ref.dtype)


def _attn_step(z, cache_k, cache_v, bias_past, bias_new, layer):
    first = SEQ // DEC_SEQ
    past = cache_k.shape[2]

    def z_spec(col):
        return pl.BlockSpec((DEC_SEQ, D_ATT), lambda b: (first + b, col))

    cache_spec = pl.BlockSpec((None, None, past, N_HEADS_ATT, HEAD_DIM_ATT), lambda b: (layer, b, 0, 0, 0))
    return pl.pallas_call(
        _attn_step_kernel,
        grid=(DEC_BATCH,),
        in_specs=[
            z_spec(COL_Q), z_spec(COL_K), z_spec(COL_V), cache_spec, cache_spec,
            _layer_spec(0, (N_HEADS_ATT, DEC_SEQ, past)),
            _layer_spec(0, (N_HEADS_ATT, DEC_SEQ, DEC_SEQ)),
        ],
        out_specs=pl.BlockSpec((DEC_SEQ, D_ATT), lambda b: (b, 0)),
        out_shape=jax.ShapeDtypeStruct((N_SAMPLE, D_ATT), BF16),
        compiler_params=_params(("parallel",)),
        name="attn_step",
    )(z, z, z, cache_k, cache_v, bias_past, bias_new)


def _mlstm_block(q, k, v, om, gates, gnorm, c0, n0, m0):
    li_col, lf_col, li_row, lf_row = gates
    L = q.shape[0]
    kscale = HEAD_DIM_MLSTM ** -0.5
    t_idx = lax.broadcasted_iota(jnp.int32, (L, L), 0)
    s_idx = lax.broadcasted_iota(jnp.int32, (L, L), 1)
    causal = s_idx <= t_idx
    b_col = jnp.sum(jnp.where(causal, lf_row, 0.0), axis=1, keepdims=True)
    b_row = jnp.sum(jnp.where(t_idx <= s_idx, lf_col, 0.0), axis=0, keepdims=True)
    d = jnp.where(causal, b_col - b_row + li_row, NEG)
    inter = b_col + m0
    m_col = jnp.maximum(inter, jnp.max(d, axis=1, keepdims=True))
    w = jnp.exp(d - (m_col - math.log(kscale)))
    sc = jnp.exp(inter - m_col)
    qk = _dot_nt(q, k) * w
    num = sc * _dot_nt(q, c0.astype(BF16)) + _dot(qk.astype(BF16), v)
    qn = _dot_nt(q, jnp.broadcast_to(n0, (BF16_ROWS, HEAD_DIM_MLSTM)).astype(BF16))[:, 0:1]
    den = sc * qn + jnp.sum(qk, axis=1, keepdims=True)
    hh = num * (1.0 / jnp.maximum(jnp.abs(den), jnp.exp(-m_col)))
    mu = jnp.mean(hh, axis=1, keepdims=True)
    cen = hh - mu
    var = jnp.mean(cen * cen, axis=1, keepdims=True)
    y = _sigmoid(om.astype(F32)) * (cen * lax.rsqrt(var + EPS) * gnorm)
    m_last = m_col[L - 1:L, :]
    b_last = b_col[L - 1:L, :]
    decay = jnp.exp(b_last + m0 - m_last)
    ws = jnp.exp(b_last - b_col + li_col - m_last) * kscale
    vs = (v.astype(F32) * ws).astype(BF16)
    c1 = decay * c0 + _dot_tn(vs, k)
    n1 = decay * n0 + jnp.sum(k.astype(F32) * ws, axis=0, keepdims=True)
    return y, c1, n1, m_last


def _log_gates(gif):
    lane = lax.broadcasted_iota(jnp.int32, gif.shape, 1)
    lg = jnp.where(lane < N_HEADS_MLSTM, gif, _log_sigmoid(gif))
    return lg, lg.T


def _gate_views(log_gates, head):
    lg, lg_t = log_gates
    f = N_HEADS_MLSTM + head
    return lg[:, head:head + 1], lg[:, f:f + 1], lg_t[head:head + 1, :], lg_t[f:f + 1, :]


def _mixers_prompt_kernel(q_ref, k0_ref, k1_ref, k2_ref, v0_ref, v1_ref, v2_ref, bias_ref,
                          qm_ref, km_ref, vm_ref, om_ref, gif_ref, gn_ref,
                          yb_ref, yc_ref, c_out, n_out, m_out, c_scr, n_scr, m_scr):
    step = pl.program_id(0)

    @pl.when(step == 0)
    def _():
        c_scr[...] = jnp.zeros_like(c_scr)
        n_scr[...] = jnp.zeros_like(n_scr)
        m_scr[...] = jnp.zeros_like(m_scr)

    log_gates = _log_gates(gif_ref[...])
    att_per_mlstm = N_HEADS_ATT // N_HEADS_MLSTM
    for h in range(N_HEADS_MLSTM):
        sl = slice(h * HEAD_DIM_MLSTM, (h + 1) * HEAD_DIM_MLSTM)
        y, c1, n1, m1 = _mlstm_block(qm_ref[:, sl], km_ref[:, sl], vm_ref[:, sl], om_ref[:, sl],
                                     _gate_views(log_gates, h), gn_ref[:, sl],
                                     c_scr[h], n_scr[h], m_scr[h][:, :1])
        yc_ref[:, sl] = y.astype(yc_ref.dtype)
        c_scr[h] = c1
        n_scr[h] = n1
        m_scr[h] = jnp.broadcast_to(m1, (1, LANES))
        for ha in range(h * att_per_mlstm, (h + 1) * att_per_mlstm):
            _attn_prompt_head(ha, q_ref, (k0_ref, k1_ref, k2_ref), (v0_ref, v1_ref, v2_ref), bias_ref, yb_ref)

    @pl.when(step == pl.num_programs(0) - 1)
    def _():
        c_out[...] = c_scr[...]
        n_out[...] = n_scr[...]
        m_out[...] = m_scr[...]


def _mixers_prompt(z, bias, gif, gnorm, layer):
    assert ATT_Q == MLSTM_L

    def z_spec(col):
        return pl.BlockSpec((ATT_Q, D_ATT), lambda j: (j, col))

    def kv_spec(col, back):
        return pl.BlockSpec((ATT_Q, D_ATT), lambda j: (jnp.maximum(j - back, 0), col))

    state_shapes = [
        jax.ShapeDtypeStruct((N_HEADS_MLSTM, HEAD_DIM_MLSTM, HEAD_DIM_MLSTM), F32),
        jax.ShapeDtypeStruct((N_HEADS_MLSTM, 1, HEAD_DIM_MLSTM), F32),
        jax.ShapeDtypeStruct((N_HEADS_MLSTM, 1, LANES), F32),
    ]
    return pl.pallas_call(
        _mixers_prompt_kernel,
        grid=(SEQ // ATT_Q,),
        in_specs=[
            z_spec(COL_Q),
            kv_spec(COL_K, 2), kv_spec(COL_K, 1), kv_spec(COL_K, 0),
            kv_spec(COL_V, 2), kv_spec(COL_V, 1), kv_spec(COL_V, 0),
            pl.BlockSpec((None, None, N_HEADS_ATT, ATT_Q, ATT_K),
                         lambda j: (0, jnp.minimum(j, ATT_VARIANTS - 1), 0, 0, 0)),
            z_spec(COL_QM), z_spec(COL_KM), z_spec(COL_VM), z_spec(COL_OM),
            pl.BlockSpec((MLSTM_L, LANES), lambda j: (j, 0)),
            _layer_spec(layer, (1, D_MLSTM)),
        ],
        out_specs=[pl.BlockSpec((ATT_Q, D_ATT), lambda j: (j, 0)), pl.BlockSpec((MLSTM_L, D_MLSTM), lambda j: (j, 0))]
        + [_const_spec(s.shape) for s in state_shapes],
        out_shape=[jax.ShapeDtypeStruct((SEQ, D_ATT), BF16), jax.ShapeDtypeStruct((SEQ, D_MLSTM), BF16)]
        + state_shapes,
        scratch_shapes=[pltpu.VMEM(s.shape, F32) for s in state_shapes],
        compiler_params=_params(("arbitrary",)),
        name="mixers_prompt",
    )(z, z, z, z, z, z, z, bias, z, z, z, z, gif, gnorm)


def _mlstm_step_kernel(q_ref, k_ref, v_ref, om_ref, gif_ref, gn_ref, c_ref, n_ref, m_ref,
                       y_ref, c_out, n_out, m_out):
    log_gates = _log_gates(gif_ref[...])
    for h in range(N_HEADS_MLSTM):
        sl = slice(h * HEAD_DIM_MLSTM, (h + 1) * HEAD_DIM_MLSTM)
        y, c1, n1, m1 = _mlstm_block(q_ref[:, sl], k_ref[:, sl], v_ref[:, sl], om_ref[:, sl],
                                     _gate_views(log_gates, h), gn_ref[:, sl],
                                     c_ref[h], n_ref[h], m_ref[h][:, :1])
        y_ref[:, sl] = y.astype(y_ref.dtype)
        c_out[h] = c1
        n_out[h] = n1
        m_out[h] = jnp.broadcast_to(m1, (1, LANES))


def _mlstm_step(z, gif, gnorm, state_c, state_n, state_m, layer):
    first = SEQ // DEC_SEQ

    def z_spec(col):
        return pl.BlockSpec((DEC_SEQ, D_MLSTM), lambda b: (first + b, col))

    def st_in(shape):
        return pl.BlockSpec((None, None) + shape, lambda b: (layer, b) + (0,) * len(shape))

    def st_out(shape):
        return pl.BlockSpec((None,) + shape, lambda b: (b,) + (0,) * len(shape))

    shapes = [(N_HEADS_MLSTM, HEAD_DIM_MLSTM, HEAD_DIM_MLSTM), (N_HEADS_MLSTM, 1, HEAD_DIM_MLSTM),
              (N_HEADS_MLSTM, 1, LANES)]
    return pl.pallas_call(
        _mlstm_step_kernel,
        grid=(DEC_BATCH,),
        in_specs=[
            z_spec(COL_QM), z_spec(COL_KM), z_spec(COL_VM), z_spec(COL_OM),
            pl.BlockSpec((DEC_SEQ, LANES), lambda b: (first + b, 0)),
            _layer_spec(layer, (1, D_MLSTM)),
        ] + [st_in(s) for s in shapes],
        out_specs=[pl.BlockSpec((DEC_SEQ, D_MLSTM), lambda b: (b, 0))] + [st_out(s) for s in shapes],
        out_shape=[jax.ShapeDtypeStruct((N_SAMPLE, D_MLSTM), BF16)]
        + [jax.ShapeDtypeStruct((DEC_BATCH,) + s, F32) for s in shapes],
        compiler_params=_params(("parallel",)),
        name="mlstm_step",
    )(z, z, z, z, gif, gnorm, state_c, state_n, state_m)


HALO = 16
GROUPS_PER_TILE = ROW_TILE // DEC_SEQ


def _merge_kernel(xa_ref, gb_ref, gc_ref, xah_ref, gch_ref, s1_ref, s2_ref, cw_ref,
                  ybp_ref, ybs_ref, ycp_ref, ycs_ref, zg_ref, xp_ref, xs_ref,
                  wpc_ref, wpa_ref, wpm_ref, wout_ref, x_out, tail_out, u_scr):
    i = pl.program_id(0)
    is_s = i >= N_PROMPT_TILES
    row = lax.broadcasted_iota(jnp.int32, (ROW_TILE, 1), 0)
    pos = jnp.where(is_s, row % DEC_SEQ, row)
    u = gc_ref[...].astype(F32) * xa_ref[...].astype(F32)
    u_halo = gch_ref[...].astype(F32) * xah_ref[...].astype(F32)
    keep = jnp.logical_and(i > 0, jnp.logical_not(is_s))
    h1 = jnp.where(keep, u_halo[HALO - 1:HALO, :], 0.0)
    h2 = jnp.where(keep, u_halo[HALO - 2:HALO - 1, :], 0.0)
    f1 = jnp.where(is_s, s1_ref[...], h1)
    f2 = jnp.where(is_s, s2_ref[...], jnp.where(row == 0, h2, h1))
    u_m1 = jnp.where(pos >= 1, pltpu.roll(u, 1, 0), f1)
    u_m2 = jnp.where(pos >= 2, pltpu.roll(u, 2, 0), f2)
    cw = cw_ref[...]
    y = cw[0:1, :] * u_m2 + cw[1:2, :] * u_m1 + cw[2:3, :] * u
    ya = gb_ref[...].astype(F32) * y

    u_scr[...] = u
    for g in range(GROUPS_PER_TILE):
        for j in range(CONV_W - 1):
            src = (g + 1) * DEC_SEQ - (CONV_W - 1) + j
            tail_out[j, g:g + 1, :] = u_scr[src:src + 1, :]

    yb = jnp.where(is_s, ybs_ref[...], ybp_ref[...])
    yc = jnp.where(is_s, ycs_ref[...], ycp_ref[...])
    g = _sigmoid(zg_ref[...].astype(F32))
    merged = (g[:, 0:D_MODEL] * _dot(ya.astype(BF16), wpc_ref[...])
              + g[:, D_MODEL:2 * D_MODEL] * _dot(yb, wpa_ref[...])
              + g[:, 2 * D_MODEL:3 * D_MODEL] * _dot(yc, wpm_ref[...]))
    x = jnp.where(is_s, xs_ref[...], xp_ref[...])
    x_out[...] = x + _dot(merged.astype(BF16), wout_ref[...])


def _merge(z, zg, s1, s2, conv_w, yb_p, yb_s, yc_p, yc_s, x_p, x_s, x_s_block, wpc, wpa, wpm, wout, layer):
    halo_blocks = ROW_TILE // HALO

    def z_spec(col):
        return pl.BlockSpec((ROW_TILE, D_CONV), lambda i: (i, col))

    def halo_spec(col):
        return pl.BlockSpec((HALO, D_CONV), lambda i: (jnp.maximum(i * halo_blocks - 1, 0), col))

    def prompt_spec(width):
        return pl.BlockSpec((ROW_TILE, width), lambda i: (jnp.minimum(i, N_PROMPT_TILES - 1), 0))

    def weight_spec(shape):
        return _layer_spec(layer, shape, pipeline_mode=pl.Buffered(1))

    return pl.pallas_call(
        _merge_kernel,
        grid=(N_TILES,),
        in_specs=[
            z_spec(COL_XA), z_spec(COL_GB), z_spec(COL_GC), halo_spec(COL_XA), halo_spec(COL_GC),
            _layer_spec(layer, (ROW_TILE, D_CONV)), _layer_spec(layer, (ROW_TILE, D_CONV)),
            _layer_spec(layer, (CONV_W, D_CONV)),
            prompt_spec(D_ATT), _const_spec((ROW_TILE, D_ATT)),
            prompt_spec(D_MLSTM), _const_spec((ROW_TILE, D_MLSTM)),
            _row_spec(N_BRANCH * D_MODEL),
            prompt_spec(D_MODEL), pl.BlockSpec((ROW_TILE, D_MODEL), lambda i: (x_s_block, 0)),
            weight_spec((D_CONV, D_MODEL)), weight_spec((D_ATT, D_MODEL)), weight_spec((D_MLSTM, D_MODEL)),
            weight_spec((D_MODEL, D_MODEL)),
        ],
        out_specs=[
            _row_spec(D_MODEL),
            pl.BlockSpec((None, CONV_W - 1, GROUPS_PER_TILE, D_CONV), lambda i: (i, 0, 0, 0)),
        ],
        out_shape=[
            jax.ShapeDtypeStruct((N_ROWS, D_MODEL), F32),
            jax.ShapeDtypeStruct((N_TILES, CONV_W - 1, GROUPS_PER_TILE, D_CONV), F32),
        ],
        scratch_shapes=[pltpu.VMEM((ROW_TILE, D_CONV), F32)],
        compiler_params=_params(("parallel",)),
        name="merge",
    )(z, z, z, z, z, s1, s2, conv_w, yb_p, yb_s, yc_p, yc_s, zg, x_p, x_s, wpc, wpa, wpm, wout)


def _granule_copies(idx_ref, first, n, src_hbm, buf, sem, slot):
    return [
        pltpu.make_async_copy(
            src_hbm.at[pl.ds(pl.multiple_of(idx_ref[first + k] * GRAN, GRAN), GRAN), :],
            buf.at[slot, pl.ds(k * GRAN, GRAN), :],
            sem.at[slot])
        for k in range(n)
    ]


GATHER_AHEAD = 2
GATHER_SLOTS = GATHER_AHEAD + 1


def _prefetched_gather(idx_ref, n, src_hbm, buf, sem, first_group=0, used_ref=None):
    step = pl.program_id(0)
    n_steps = pl.num_programs(0)

    def used(s):
        return True if used_ref is None else used_ref[jnp.minimum(s, n_steps - 1)] == 1

    def request(ahead):
        slot = (step + ahead) % GATHER_SLOTS
        for cp in _granule_copies(idx_ref, (first_group + step + ahead) * n, n, src_hbm, buf, sem, slot):
            cp.start()

    for ahead in range(GATHER_AHEAD):
        pl.when((step == 0) & (ahead < n_steps) & used(ahead))(functools.partial(request, ahead))
    pl.when((step + GATHER_AHEAD < n_steps) & used(step + GATHER_AHEAD))(functools.partial(request, GATHER_AHEAD))

    slot = step % GATHER_SLOTS

    @pl.when(used(step))
    def _():
        for cp in _granule_copies(idx_ref, (first_group + step) * n, n, src_hbm, buf, sem, slot):
            cp.wait()

    return slot


def _expert_ffn_kernel(src_ref, expert_ref, first_ref, valid_ref, next_ref, xs_hbm, wg_hbm, wu_hbm, wd_hbm,
                       o_ref, buf, sem, wg_st, wu_st, wd_st, wsem, wg_bf, wu_bf, wd_bf, *, layer):
    c = pl.program_id(0)
    slot = _prefetched_gather(src_ref, CHUNK_GRANS, xs_hbm, buf, sem, used_ref=valid_ref)

    def weight_copies(e):
        return [pltpu.make_async_copy(wg_hbm.at[layer, e], wg_st, wsem.at[0]),
                pltpu.make_async_copy(wu_hbm.at[layer, e], wu_st, wsem.at[1]),
                pltpu.make_async_copy(wd_hbm.at[layer, e], wd_st, wsem.at[2])]

    @pl.when(c == 0)
    def _():
        for cp in weight_copies(expert_ref[0]):
            cp.start()

    @pl.when(first_ref[c] == 1)
    def _():
        for cp in weight_copies(expert_ref[c]):
            cp.wait()
        wg_bf[...] = wg_st[...].astype(BF16)
        wu_bf[...] = wu_st[...].astype(BF16)
        wd_bf[...] = wd_st[...].astype(BF16)

        @pl.when(next_ref[c] >= 0)
        def _():
            for cp in weight_copies(next_ref[c]):
                cp.start()

    @pl.when(valid_ref[c] == 1)
    def _():
        rows = buf[slot]
        x = rows[:, :D_MODEL]
        meta = rows[:, D_MODEL:].astype(F32)
        w_first = meta[:, 1:2] + meta[:, 2:3] + meta[:, 3:4]
        w_second = meta[:, 5:6] + meta[:, 6:7] + meta[:, 7:8]
        w = jnp.where(meta[:, 0:1] == expert_ref[c].astype(F32), w_first, w_second)
        gate = _dot(x, wg_bf[...])
        a = gate * _sigmoid(gate) * _dot(x, wu_bf[...]) * w
        o_ref[...] = _dot(a.astype(BF16), wd_bf[...]).astype(o_ref.dtype)

    @pl.when(valid_ref[c] == 0)
    def _():
        o_ref[...] = jnp.zeros_like(o_ref)


def _expert_ffn(src, chunk_expert, chunk_first, chunk_valid, chunk_next, xs, w_gate, w_up, w_down, layer):
    any_spec = pl.BlockSpec(memory_space=pl.ANY)
    grid_spec = pltpu.PrefetchScalarGridSpec(
        num_scalar_prefetch=5,
        grid=(MAX_CHUNKS,),
        in_specs=[any_spec, any_spec, any_spec, any_spec],
        out_specs=pl.BlockSpec((CHUNK_ROWS, D_MODEL), lambda c, *_: (c, 0)),
        scratch_shapes=[
            pltpu.VMEM((GATHER_SLOTS, CHUNK_ROWS, XS_WIDTH), BF16),
            pltpu.SemaphoreType.DMA((GATHER_SLOTS,)),
            pltpu.VMEM((D_MODEL, D_EXPERT), F32),
            pltpu.VMEM((D_MODEL, D_EXPERT), F32),
            pltpu.VMEM((D_EXPERT, D_MODEL), F32),
            pltpu.SemaphoreType.DMA((3,)),
            pltpu.VMEM((D_MODEL, D_EXPERT), BF16),
            pltpu.VMEM((D_MODEL, D_EXPERT), BF16),
            pltpu.VMEM((D_EXPERT, D_MODEL), BF16),
        ],
    )
    return pl.pallas_call(
        functools.partial(_expert_ffn_kernel, layer=layer),
        grid_spec=grid_spec,
        out_shape=jax.ShapeDtypeStruct((MAX_CHUNKS * CHUNK_ROWS, D_MODEL), BF16),
        compiler_params=_params(("arbitrary",)),
        name="expert_ffn",
    )(src, chunk_expert, chunk_first, chunk_valid, chunk_next, xs, w_gate, w_up, w_down)


def _combine(dst_ref, ys_hbm, lp_ref, x_ref, buf, sem, first_tile=0):
    slot = _prefetched_gather(dst_ref, TILE_GRANS, ys_hbm, buf, sem, first_tile)
    lp = lp_ref[...]
    p_idx = lax.broadcasted_iota(jnp.int32, (ROW_TILE, LOCAL_ROWS), 1).astype(F32)
    pick = jnp.where((p_idx == lp[:, 0:1]) | (p_idx == lp[:, 1:2]), 1.0, 0.0).astype(BF16)
    return x_ref[...] + _dot(pick, buf[slot])


def _combine_norm_kernel(dst_ref, ys_hbm, lp_ref, x_ref, g_ref, whi_ref, wlo_ref, b_ref,
                         x_out, h_out, s_out, buf, sem):
    x = _combine(dst_ref, ys_hbm, lp_ref, x_ref, buf, sem)
    x_out[...] = x
    h = _rms(x, g_ref[...])
    h_out[...] = h.astype(BF16)
    s_out[...] = _small_proj(h, whi_ref, wlo_ref, b_ref)


def _combine_final_kernel(dst_ref, ys_hbm, lp_ref, x_ref, g_ref, o_ref, buf, sem, *, first_tile):
    o_ref[...] = _rms(_combine(dst_ref, ys_hbm, lp_ref, x_ref, buf, sem, first_tile), g_ref[...])


_COMBINE_SCRATCH = [pltpu.VMEM((GATHER_SLOTS, LOCAL_ROWS, D_MODEL), BF16),
                    pltpu.SemaphoreType.DMA((GATHER_SLOTS,))]


def _combine_norm(dst, ys, lp, x, g, whi, wlo, b, layer):
    def row(width):
        return pl.BlockSpec((ROW_TILE, width), lambda i, dst: (i, 0))

    grid_spec = pltpu.PrefetchScalarGridSpec(
        num_scalar_prefetch=1,
        grid=(N_TILES,),
        in_specs=[pl.BlockSpec(memory_space=pl.ANY), row(LANES), row(D_MODEL)] + _norm_param_specs(layer),
        out_specs=[row(D_MODEL), row(D_MODEL), row(LANES)],
        scratch_shapes=_COMBINE_SCRATCH,
    )
    return pl.pallas_call(
        _combine_norm_kernel,
        grid_spec=grid_spec,
        out_shape=[
            jax.ShapeDtypeStruct((N_ROWS, D_MODEL), F32),
            jax.ShapeDtypeStruct((N_ROWS, D_MODEL), BF16),
            jax.ShapeDtypeStruct((N_ROWS, LANES), F32),
        ],
        compiler_params=_params(("arbitrary",)),
        name="combine_norm",
    )(dst, ys, lp, x, g, whi, wlo, b)


def _combine_final(dst, ys, lp, x, g, first_tile, n_tiles):
    def row(width):
        return pl.BlockSpec((ROW_TILE, width), lambda i, dst: (i + first_tile, 0))

    grid_spec = pltpu.PrefetchScalarGridSpec(
        num_scalar_prefetch=1,
        grid=(n_tiles,),
        in_specs=[pl.BlockSpec(memory_space=pl.ANY), row(LANES), row(D_MODEL),
                  pl.BlockSpec((1, D_MODEL), lambda i, dst: (0, 0))],
        out_specs=pl.BlockSpec((ROW_TILE, D_MODEL), lambda i, dst: (i, 0)),
        scratch_shapes=_COMBINE_SCRATCH,
    )
    return pl.pallas_call(
        functools.partial(_combine_final_kernel, first_tile=first_tile),
        grid_spec=grid_spec,
        out_shape=jax.ShapeDtypeStruct((n_tiles * ROW_TILE, D_MODEL), F32),
        compiler_params=_params(("arbitrary",)),
        name="combine_final",
    )(dst, ys, lp, x, g)


N_SRC = MAX_CHUNKS * CHUNK_GRANS
N_DST = N_TILES * TILE_GRANS


def _granule_kernel(gcnt_ref, src_ref, dst_ref, expert_ref, first_ref, valid_ref, next_ref, local_ref):
    def fill(ref, n, value):
        def body(k, carry):
            ref[k] = value
            return carry
        lax.fori_loop(0, n, body, 0, unroll=8)

    fill(src_ref, N_SRC, TILE_GRANS - 1)
    fill(dst_ref, N_DST, 0)
    fill(expert_ref, MAX_CHUNKS, N_EXPERTS - 1)
    fill(first_ref, MAX_CHUNKS, 0)
    fill(valid_ref, MAX_CHUNKS, 0)
    fill(next_ref, MAX_CHUNKS + 1, -1)
    fill(local_ref, N_TILES, 0)

    def per_expert(e, carry):
        chunk, prev_first = carry
        slot0 = chunk * CHUNK_GRANS

        def per_tile(t, pos):
            n = gcnt_ref[t * N_EXPERTS + e]
            local0 = t * TILE_GRANS + local_ref[t]

            def per_granule(k, c2):
                src_ref[pos + k] = local0 + k
                dst_ref[local0 + k] = pos + k
                return c2

            lax.fori_loop(0, n, per_granule, 0)
            local_ref[t] = local_ref[t] + n
            return pos + n

        end = lax.fori_loop(0, N_TILES, per_tile, slot0)
        n_chunks = (end - slot0 + CHUNK_GRANS - 1) // CHUNK_GRANS

        def per_chunk(k, c2):
            expert_ref[chunk + k] = e
            valid_ref[chunk + k] = 1
            return c2

        lax.fori_loop(0, n_chunks, per_chunk, 0)
        owns = n_chunks > 0

        @pl.when(owns)
        def _():
            first_ref[chunk] = 1
            next_ref[prev_first] = e

        return chunk + n_chunks, jnp.where(owns, chunk, prev_first)

    lax.fori_loop(0, N_EXPERTS, per_expert, (0, MAX_CHUNKS))


def _granule_lists(gcnt):
    smem = pl.BlockSpec(memory_space=pltpu.SMEM)
    sizes = (N_SRC, N_DST, MAX_CHUNKS, MAX_CHUNKS, MAX_CHUNKS, MAX_CHUNKS + 1)
    return pl.pallas_call(
        _granule_kernel,
        in_specs=[smem],
        out_specs=[smem] * len(sizes),
        out_shape=[jax.ShapeDtypeStruct((n,), jnp.int32) for n in sizes],
        scratch_shapes=[pltpu.SMEM((N_TILES,), jnp.int32)],
        name="granule_lists",
    )(gcnt.reshape(N_TILES * N_EXPERTS))


def _pad_lanes(w):
    return jnp.pad(w, [(0, 0)] * (w.ndim - 1) + [(0, LANES - w.shape[-1])])


def _toeplitz_bias(table, n_q, n_k, reach):
    period = 1
    while period < n_q + n_k:
        period *= 2
    j = jnp.arange(period)
    d = jnp.where(j < n_k, j, j - period)
    u = table.reshape(-1, table.shape[-1])[:, jnp.clip(reach - d, -REL_CLIP, REL_CLIP) + REL_CLIP]
    flat = jnp.tile(u, (1, n_q))[:, :n_q * (period - 1)]
    return flat.reshape(table.shape[:-1] + (n_q, period - 1))[..., :n_k]


def _prompt_bias(table):
    bias = _toeplitz_bias(table, ATT_Q, ATT_K, ATT_REACH)
    qc = jnp.arange(ATT_Q)[:, None] // CHUNK
    m = jnp.arange(ATT_K)[None, :]
    kc = m // CHUNK
    band = (kc >= qc) & (kc <= qc + BAND_CHUNKS)
    first_key = ATT_REACH - ATT_Q * jnp.arange(ATT_VARIANTS)
    ok = band[None] & (m[None] >= first_key[:, None, None])
    return jnp.where(ok[None, :, None], bias[:, None], NEG)


def _step_bias(table, past):
    bias = _toeplitz_bias(table, DEC_SEQ, past + DEC_SEQ, past)
    return bias[..., :past], bias[..., past:]


def kernel(x_prompt, x_sample, state_conv, cache_k, cache_v, state_C, state_n, state_m, norm_mix, norm_ffn, w_in, conv_w, rel_bias, gate_bias, mlstm_norm, w_proj_conv, w_proj_att, w_proj_mlstm, w_out, router_group, router_group_bias, router_expert, router_expert_bias, w_gate, w_up, w_down, norm_final):
    past = cache_k.shape[2]
    xp = x_prompt.reshape(SEQ, D_MODEL)
    xs = x_sample.reshape(N_SAMPLE, D_MODEL)
    state_n5 = state_n.reshape(DEPTH, DEC_BATCH, N_HEADS_MLSTM, 1, HEAD_DIM_MLSTM)
    state_m5 = jnp.broadcast_to(state_m[..., None, None], (DEPTH, DEC_BATCH, N_HEADS_MLSTM, 1, LANES))

    w_in_t = jnp.swapaxes(w_in, 1, 2)

    g_mix = norm_mix[:, None, :]
    g_ffn = norm_ffn[:, None, :]
    gnorm = mlstm_norm[:, None, :]
    gif_w = jnp.swapaxes(w_in_t[:, MAIN_WIDTH:MAIN_WIDTH + GIF_WIDTH, :], 1, 2)
    gif_hi, gif_lo = _split_hi_lo(_pad_lanes(gif_w))
    gif_b = _pad_lanes(gate_bias[:, None, :])
    r_hi, r_lo = _split_hi_lo(_pad_lanes(jnp.concatenate([router_group, router_expert], axis=2)))
    r_b = _pad_lanes(jnp.concatenate([router_group_bias, router_expert_bias], axis=1)[:, None, :])
    hist = jnp.zeros((DEPTH, DEC_BATCH, DEC_SEQ, D_CONV), F32)
    s1 = hist.at[:, :, 0].set(state_conv[:, :, 1]).reshape(DEPTH, N_SAMPLE, D_CONV)
    s2 = (hist.at[:, :, 0].set(state_conv[:, :, 0]).at[:, :, 1].set(state_conv[:, :, 1])
          .reshape(DEPTH, N_SAMPLE, D_CONV))
    merge_w = (w_proj_conv.astype(BF16), w_proj_att.astype(BF16), w_proj_mlstm.astype(BF16), w_out.astype(BF16))

    x_mid = moe = None
    outs = {k: [] for k in ("p_conv", "p_k", "p_v", "p_C", "p_n", "p_m", "s_conv", "s_k", "s_v", "s_C", "s_n", "s_m")}
    for l in range(DEPTH):
        if l == 0:
            h, gif = _norm_first(xp, xs, g_mix, gif_hi, gif_lo, gif_b, l)
            residual = (xp, xs, 0)
        else:
            x, h, gif = _combine_norm(*moe, x_mid, g_mix, gif_hi, gif_lo, gif_b, l)
            residual = (x, x, N_PROMPT_TILES)
        z = _project(h, w_in_t, l, 0, MAIN_WIDTH)
        zg = _project(h, w_in_t, l, MAIN_WIDTH + GIF_WIDTH, N_BRANCH * D_MODEL)

        yb_p, yc_p, p_c, p_n, p_m = _mixers_prompt(z, _prompt_bias(rel_bias[l:l + 1]), gif, gnorm, l)
        bias_past, bias_new = _step_bias(rel_bias[l:l + 1], past)
        yb_s = _attn_step(z, cache_k, cache_v, bias_past, bias_new, l)
        yc_s, s_c, s_n, s_m = _mlstm_step(z, gif, gnorm, state_C, state_n5, state_m5, l)

        x_mid, tails = _merge(z, zg, s1, s2, conv_w, yb_p, yb_s, yc_p, yc_s, *residual, *merge_w, l)

        xs_local, lp, gcnt = _route_dispatch(x_mid, g_ffn, r_hi, r_lo, r_b, l)
        gcnt = gcnt[:, 0, N_GROUPS:N_GROUPS + N_EXPERTS].astype(jnp.int32)
        src, dst, chunk_expert, chunk_first, chunk_valid, chunk_next = _granule_lists(gcnt)
        ys = _expert_ffn(src, chunk_expert, chunk_first, chunk_valid, chunk_next, xs_local, w_gate, w_up, w_down, l)
        moe = (dst, ys, lp)

        keep = min(ATT_REACH, SEQ)
        k_rows = z[SEQ - keep:, COL_K * D_ATT:(COL_K + 1) * D_ATT].astype(F32)
        v_rows = z[SEQ - keep:, COL_V * D_ATT:(COL_V + 1) * D_ATT].astype(F32)
        outs["p_conv"].append(tails[N_PROMPT_TILES - 1, :, GROUPS_PER_TILE - 1][None])
        outs["p_k"].append(k_rows[:keep].reshape(1, keep, N_HEADS_ATT, HEAD_DIM_ATT))
        outs["p_v"].append(v_rows[:keep].reshape(1, keep, N_HEADS_ATT, HEAD_DIM_ATT))
        outs["p_C"].append(p_c[None])
        outs["p_n"].append(p_n[:, 0][None])
        outs["p_m"].append(p_m[:, 0, 0][None])
        outs["s_conv"].append(jnp.swapaxes(tails[N_PROMPT_TILES], 0, 1))
        outs["s_k"].append(k_rows[keep:].reshape(DEC_BATCH, DEC_SEQ, N_HEADS_ATT, HEAD_DIM_ATT))
        outs["s_v"].append(v_rows[keep:].reshape(DEC_BATCH, DEC_SEQ, N_HEADS_ATT, HEAD_DIM_ATT))
        outs["s_C"].append(s_c)
        outs["s_n"].append(s_n[:, :, 0])
        outs["s_m"].append(s_m[:, :, 0, 0])

    g_fin = norm_final[None, :]
    y_prompt = _combine_final(*moe, x_mid, g_fin, 0, N_PROMPT_TILES).reshape(x_prompt.shape)
    y_sample = _combine_final(*moe, x_mid, g_fin, N_PROMPT_TILES, 1).reshape(x_sample.shape)
    st = {k: jnp.stack(v) for k, v in outs.items()}
    return (y_prompt, y_sample, st["p_conv"], st["p_k"], st["p_v"], st["p_C"], st["p_n"], st["p_m"],
            st["s_conv"], st["s_k"], st["s_v"], st["s_C"], st["s_n"], st["s_m"])
```

```python
import functools
import math

import jax
import jax.numpy as jnp
from jax import lax
from jax.experimental import pallas as pl
from jax.experimental.pallas import tpu as pltpu

D_MODEL = 2048
SEQ = 8192
DEPTH = 2
DEC_BATCH = 8
DEC_SEQ = 32
N_SAMPLE = DEC_BATCH * DEC_SEQ
N_ROWS = SEQ + N_SAMPLE

CHUNK = 64
D_CONV = 1024
CONV_W = 3
N_HEADS_ATT = 8
HEAD_DIM_ATT = 128
D_ATT = N_HEADS_ATT * HEAD_DIM_ATT
BAND_CHUNKS = 8
ATT_REACH = BAND_CHUNKS * CHUNK
REL_CLIP = 128
N_HEADS_MLSTM = 4
HEAD_DIM_MLSTM = 256
D_MLSTM = N_HEADS_MLSTM * HEAD_DIM_MLSTM
N_BRANCH = 3
MAIN_WIDTH = 3 * D_CONV + 3 * D_ATT + 4 * D_MLSTM
GIF_WIDTH = 2 * N_HEADS_MLSTM
N_GROUPS = 4
EXPERTS_PER_GROUP = 4
N_EXPERTS = N_GROUPS * EXPERTS_PER_GROUP
D_EXPERT = 512
EPS = 1e-6

LANES = 128
BF16_ROWS = 16
ROW_TILE = 256
N_TILES = N_ROWS // ROW_TILE
N_PROMPT_TILES = SEQ // ROW_TILE
MM_ROWS = 1056
MM_COLS = 1024
ATT_Q = 256
ATT_K = ATT_Q + ATT_REACH
ATT_VARIANTS = ATT_REACH // ATT_Q + 1
MLSTM_L = 256
NEG = -1e30
VMEM_LIMIT = 56 * 1024 * 1024

GRAN = BF16_ROWS
XS_WIDTH = D_MODEL + LANES
MAX_RUN_PAD = N_EXPERTS * (GRAN - 1)
TILE_GRANS = (2 * ROW_TILE + MAX_RUN_PAD) // GRAN + 1
LOCAL_ROWS = TILE_GRANS * GRAN
CHUNK_GRANS = 16
CHUNK_ROWS = CHUNK_GRANS * GRAN
MAX_CHUNKS = (N_TILES * (TILE_GRANS - 1)) // CHUNK_GRANS + N_EXPERTS

COL_XA, COL_GB, COL_GC, COL_Q, COL_K, COL_V, COL_QM, COL_KM, COL_VM, COL_OM = range(10)

F32 = jnp.float32
BF16 = jnp.bfloat16


def _params(sem):
    return pltpu.CompilerParams(dimension_semantics=sem, vmem_limit_bytes=VMEM_LIMIT)


def _dot(a, b):
    return jnp.dot(a, b, preferred_element_type=F32)


def _dot_nt(a, b):
    return lax.dot_general(a, b, (((1,), (1,)), ((), ())), preferred_element_type=F32)


def _dot_tn(a, b):
    return lax.dot_general(a, b, (((0,), (0,)), ((), ())), preferred_element_type=F32)


def _split_hi_lo(w):
    hi = w.astype(BF16)
    lo = (w - hi.astype(F32)).astype(BF16)
    return hi, lo


def _sigmoid(x):
    return 1.0 / (1.0 + jnp.exp(-x))


def _log_sigmoid(x):
    return jnp.minimum(x, 0.0) - jnp.log1p(jnp.exp(-jnp.abs(x)))


def _rms(x, g):
    ms = jnp.mean(x * x, axis=-1, keepdims=True)
    return x * lax.rsqrt(ms + EPS) * g


def _small_proj(h, whi_ref, wlo_ref, b_ref):
    h_hi = h.astype(BF16)
    h_lo = (h - h_hi.astype(F32)).astype(BF16)
    whi = whi_ref[...]
    return _dot(h_hi, whi) + _dot(h_lo, whi) + _dot(h_hi, wlo_ref[...]) + b_ref[...]


def _norm_first_kernel(xp_ref, xs_ref, g_ref, whi_ref, wlo_ref, b_ref, h_out, s_out):
    i = pl.program_id(0)
    x = jnp.where(i < N_PROMPT_TILES, xp_ref[...], xs_ref[...])
    h = _rms(x, g_ref[...])
    h_out[...] = h.astype(BF16)
    s_out[...] = _small_proj(h, whi_ref, wlo_ref, b_ref)


def _route(logits):
    lane = lax.broadcasted_iota(jnp.int32, logits.shape, 1)
    is_g = lane < N_GROUPS
    gl = jnp.where(is_g, logits, NEG)
    gmax = jnp.max(gl, axis=1, keepdims=True)
    g_sel = jnp.min(jnp.where(is_g & (gl == gmax), lane, LANES), axis=1, keepdims=True)
    p_g = 1.0 / jnp.sum(jnp.where(is_g, jnp.exp(gl - gmax), 0.0), axis=1, keepdims=True)
    e_lane = lane - N_GROUPS
    in_g = (e_lane >= 0) & (e_lane < N_EXPERTS) & ((e_lane // EXPERTS_PER_GROUP) == g_sel)
    e1 = jnp.max(jnp.where(in_g, logits, NEG), axis=1, keepdims=True)
    i1 = jnp.min(jnp.where(in_g & (logits == e1), lane, LANES), axis=1, keepdims=True)
    rest = in_g & (lane != i1)
    e2 = jnp.max(jnp.where(rest, logits, NEG), axis=1, keepdims=True)
    i2 = jnp.min(jnp.where(rest & (logits == e2), lane, LANES), axis=1, keepdims=True)
    r = jnp.exp(e2 - e1)
    w1 = p_g / (1.0 + r)
    w2 = w1 * r
    return i1, i2, w1, w2


def _bf16_pieces(w):
    a = w.astype(BF16).astype(F32)
    b = (w - a).astype(BF16).astype(F32)
    return a, b, w - a - b


def _route_dispatch_tiles_lockstep(xs, g_ref, whi_ref, wlo_ref, b_ref, outs):
    tiles = range(len(xs))
    lane = lax.broadcasted_iota(jnp.int32, (ROW_TILE, LANES), 1)
    t_idx = lax.broadcasted_iota(jnp.int32, (ROW_TILE, ROW_TILE), 0)
    s_idx = lax.broadcasted_iota(jnp.int32, (ROW_TILE, ROW_TILE), 1)
    earlier = jnp.where(s_idx < t_idx, 1.0, 0.0).astype(BF16)
    a_idx = lax.broadcasted_iota(jnp.int32, (LANES, LANES), 0)
    b_idx = lax.broadcasted_iota(jnp.int32, (LANES, LANES), 1)
    before = jnp.where(a_idx < b_idx, 1.0, 0.0).astype(BF16)
    p_idx = lax.broadcasted_iota(jnp.int32, (LOCAL_ROWS, ROW_TILE), 0).astype(F32)

    hs = [_rms(xs[t], g_ref[...]) for t in tiles]
    logits = [_small_proj(hs[t], whi_ref, wlo_ref, b_ref) for t in tiles]
    routes = [_route(logits[t]) for t in tiles]

    lps = []
    for t in tiles:
        i1, i2, _, _ = routes[t]
        o1 = lane == i1
        o2 = lane == i2
        onehot = jnp.where(o1 | o2, 1.0, 0.0)
        rank = _dot(earlier, onehot.astype(BF16))
        gcnt = jnp.floor((jnp.sum(onehot, axis=0, keepdims=True) + (GRAN - 1)) * (1.0 / GRAN))
        gcnt8 = jnp.broadcast_to(gcnt, (8, LANES))
        run_start = _dot(gcnt8.astype(BF16), before)[0:1, :] * GRAN
        pos = run_start + rank
        lpos1 = jnp.sum(jnp.where(o1, pos, 0.0), axis=1, keepdims=True)
        lpos2 = jnp.sum(jnp.where(o2, pos, 0.0), axis=1, keepdims=True)
        lp = jnp.where(lane == 0, lpos1, jnp.where(lane == 1, lpos2, -1.0))
        outs[t][1][...] = lp
        outs[t][2][...] = gcnt8
        lps.append(lp)

    rows = []
    for t in tiles:
        i1, i2, w1, w2 = routes[t]
        meta = jnp.zeros((ROW_TILE, LANES), F32)
        fields = (((i1 - N_GROUPS).astype(F32),) + _bf16_pieces(w1)
                  + ((i2 - N_GROUPS).astype(F32),) + _bf16_pieces(w2))
        for k, val in enumerate(fields):
            meta = jnp.where(lane == k, val, meta)
        rows.append(jnp.concatenate([hs[t].astype(BF16), meta.astype(BF16)], axis=1))

    perms = []
    for t in tiles:
        lp_t = lps[t].T
        perms.append(jnp.where((p_idx == lp_t[0:1, :]) | (p_idx == lp_t[1:2, :]), 1.0, 0.0).astype(BF16))
    for t in tiles:
        outs[t][0][...] = _dot(perms[t], rows[t]).astype(BF16)


def _row_spec(width):
    return pl.BlockSpec((ROW_TILE, width), lambda i: (i, 0))


def _const_spec(shape):
    return pl.BlockSpec(shape, lambda i: (0,) * len(shape))


def _layer_spec(layer, shape, **kwargs):
    return pl.BlockSpec((None,) + tuple(shape), lambda *_: (layer,) + (0,) * len(shape), **kwargs)


def _norm_param_specs(layer):
    return [_layer_spec(layer, (1, D_MODEL)), _layer_spec(layer, (D_MODEL, LANES)),
            _layer_spec(layer, (D_MODEL, LANES)), _layer_spec(layer, (1, LANES))]


def _norm_first(xp, xs, g, whi, wlo, b, layer):
    return pl.pallas_call(
        _norm_first_kernel,
        grid=(N_TILES,),
        in_specs=[
            pl.BlockSpec((ROW_TILE, D_MODEL), lambda i: (jnp.minimum(i, N_PROMPT_TILES - 1), 0)),
            _const_spec((ROW_TILE, D_MODEL)),
        ] + _norm_param_specs(layer),
        out_specs=[_row_spec(D_MODEL), _row_spec(LANES)],
        out_shape=[
            jax.ShapeDtypeStruct((N_ROWS, D_MODEL), BF16),
            jax.ShapeDtypeStruct((N_ROWS, LANES), F32),
        ],
        compiler_params=_params(("parallel",)),
        name="norm_first",
    )(xp, xs, g, whi, wlo, b)


DISPATCH_TILES = 3


def _route_dispatch_kernel(x_ref, g_ref, whi_ref, wlo_ref, b_ref, xs_out, lp_out, cnt_out):
    tiles = range(DISPATCH_TILES)
    _route_dispatch_tiles_lockstep(
        [x_ref[t * ROW_TILE:(t + 1) * ROW_TILE, :] for t in tiles], g_ref, whi_ref, wlo_ref, b_ref,
        [(xs_out.at[pl.ds(t * LOCAL_ROWS, LOCAL_ROWS), :], lp_out.at[pl.ds(t * ROW_TILE, ROW_TILE), :], cnt_out.at[t])
         for t in tiles])


def _route_dispatch(x, g, whi, wlo, b, layer):
    def spec(rows, width):
        return pl.BlockSpec((DISPATCH_TILES * rows, width), lambda i: (i, 0))

    return pl.pallas_call(
        _route_dispatch_kernel,
        grid=(N_TILES // DISPATCH_TILES,),
        in_specs=[spec(ROW_TILE, D_MODEL)] + _norm_param_specs(layer),
        out_specs=[
            spec(LOCAL_ROWS, XS_WIDTH),
            spec(ROW_TILE, LANES),
            pl.BlockSpec((DISPATCH_TILES, 8, LANES), lambda i: (i, 0, 0)),
        ],
        out_shape=[
            jax.ShapeDtypeStruct((N_TILES * LOCAL_ROWS, XS_WIDTH), BF16),
            jax.ShapeDtypeStruct((N_ROWS, LANES), F32),
            jax.ShapeDtypeStruct((N_TILES, 8, LANES), F32),
        ],
        compiler_params=_params(("parallel",)),
        name="route_dispatch",
    )(x, g, whi, wlo, b)


def _mm_kernel(h_ref, wt_ref, o_ref, wb_ref):
    @pl.when(pl.program_id(1) == 0)
    def _():
        wb_ref[...] = wt_ref[0].T.astype(BF16)

    o_ref[...] = _dot(h_ref[...], wb_ref[...]).astype(o_ref.dtype)


def _project(h, wt, layer, row0, n_cols):
    return pl.pallas_call(
        _mm_kernel,
        grid=(n_cols // MM_COLS, N_ROWS // MM_ROWS),
        in_specs=[
            pl.BlockSpec((MM_ROWS, D_MODEL), lambda j, i: (i, 0)),
            pl.BlockSpec((pl.Element(1), pl.Element(MM_COLS), pl.Element(D_MODEL)),
                         lambda j, i: (layer, pl.multiple_of(row0 + j * MM_COLS, 8), 0)),
        ],
        out_specs=pl.BlockSpec((MM_ROWS, MM_COLS), lambda j, i: (i, j)),
        out_shape=jax.ShapeDtypeStruct((N_ROWS, n_cols), BF16),
        scratch_shapes=[pltpu.VMEM((D_MODEL, MM_COLS), BF16)],
        compiler_params=_params(("parallel", "arbitrary")),
        name="project",
    )(h, wt)


def _attn_prompt_head(h, q_ref, k_refs, v_refs, bias_ref, o_ref):
    sl = slice(h * HEAD_DIM_ATT, (h + 1) * HEAD_DIM_ATT)
    q = q_ref[:, sl]
    kk = jnp.concatenate([r[:, sl] for r in k_refs], axis=0)
    vv = jnp.concatenate([r[:, sl] for r in v_refs], axis=0)
    s = _dot_nt(q, kk) * (HEAD_DIM_ATT ** -0.5) + bias_ref[h]
    mx = jnp.max(s, axis=1, keepdims=True)
    p = jnp.exp(s - mx).astype(BF16)
    od = _dot(p, jnp.concatenate([vv, jnp.ones((ATT_K, HEAD_DIM_ATT), BF16)], axis=1))
    o = od[:, :HEAD_DIM_ATT] * (1.0 / od[:, HEAD_DIM_ATT:HEAD_DIM_ATT + 1])
    o_ref[:, sl] = o.astype(o_ref.dtype)


def _attn_step_kernel(q_ref, k_ref, v_ref, ck_ref, cv_ref, bp_ref, bn_ref, o_ref):
    scale = HEAD_DIM_ATT ** -0.5
    ck = jnp.swapaxes(ck_ref[...], 0, 1).astype(BF16)
    cv = jnp.swapaxes(cv_ref[...], 0, 1).astype(BF16)
    for h in range(N_HEADS_ATT):
        sl = slice(h * HEAD_DIM_ATT, (h + 1) * HEAD_DIM_ATT)
        q = q_ref[:, sl]
        s_past = _dot_nt(q, ck[h]) * scale + bp_ref[h]
        s_new = _dot_nt(q, k_ref[:, sl]) * scale + bn_ref[h]
        mx = jnp.maximum(jnp.max(s_past, axis=1, keepdims=True), jnp.max(s_new, axis=1, keepdims=True))
        p_past = jnp.exp(s_past - mx)
        p_new = jnp.exp(s_new - mx)
        den = jnp.sum(p_past, axis=1, keepdims=True) + jnp.sum(p_new, axis=1, keepdims=True)
        o = _dot(p_past.astype(BF16), cv[h]) + _dot(p_new.astype(BF16), v_ref[:, sl])
        o_ref[:, sl] = (o / den).astype(o_ref.dtype)


def _attn_step(z, cache_k, cache_v, bias_past, bias_new, layer):
    first = SEQ // DEC_SEQ
    past = cache_k.shape[2]

    def z_spec(col):
        return pl.BlockSpec((DEC_SEQ, D_ATT), lambda b: (first + b, col))

    cache_spec = pl.BlockSpec((None, None, past, N_HEADS_ATT, HEAD_DIM_ATT), lambda b: (layer, b, 0, 0, 0))
    return pl.pallas_call(
        _attn_step_kernel,
        grid=(DEC_BATCH,),
        in_specs=[
            z_spec(COL_Q), z_spec(COL_K), z_spec(COL_V), cache_spec, cache_spec,
            _layer_spec(0, (N_HEADS_ATT, DEC_SEQ, past)),
            _layer_spec(0, (N_HEADS_ATT, DEC_SEQ, DEC_SEQ)),
        ],
        out_specs=pl.BlockSpec((DEC_SEQ, D_ATT), lambda b: (b, 0)),
        out_shape=jax.ShapeDtypeStruct((N_SAMPLE, D_ATT), BF16),
        compiler_params=_params(("parallel",)),
        name="attn_step",
    )(z, z, z, cache_k, cache_v, bias_past, bias_new)


def _mlstm_block(q, k, v, om, gates, gnorm, c0, n0, m0):
    li_col, lf_col, li_row, lf_row = gates
    L = q.shape[0]
    kscale = HEAD_DIM_MLSTM ** -0.5
    t_idx = lax.broadcasted_iota(jnp.int32, (L, L), 0)
    s_idx = lax.broadcasted_iota(jnp.int32, (L, L), 1)
    causal = s_idx <= t_idx
    b_col = jnp.sum(jnp.where(causal, lf_row, 0.0), axis=1, keepdims=True)
    b_row = jnp.sum(jnp.where(t_idx <= s_idx, lf_col, 0.0), axis=0, keepdims=True)
    d = jnp.where(causal, b_col - b_row + li_row, NEG)
    inter = b_col + m0
    m_col = jnp.maximum(inter, jnp.max(d, axis=1, keepdims=True))
    w = jnp.exp(d - (m_col - math.log(kscale)))
    sc = jnp.exp(inter - m_col)
    qk = _dot_nt(q, k) * w
    num = sc * _dot_nt(q, c0.astype(BF16)) + _dot(qk.astype(BF16), v)
    qn = _dot_nt(q, jnp.broadcast_to(n0, (BF16_ROWS, HEAD_DIM_MLSTM)).astype(BF16))[:, 0:1]
    den = sc * qn + jnp.sum(qk, axis=1, keepdims=True)
    hh = num * (1.0 / jnp.maximum(jnp.abs(den), jnp.exp(-m_col)))
    mu = jnp.mean(hh, axis=1, keepdims=True)
    cen = hh - mu
    var = jnp.mean(cen * cen, axis=1, keepdims=True)
    y = _sigmoid(om.astype(F32)) * (cen * lax.rsqrt(var + EPS) * gnorm)
    m_last = m_col[L - 1:L, :]
    b_last = b_col[L - 1:L, :]
    decay = jnp.exp(b_last + m0 - m_last)
    ws = jnp.exp(b_last - b_col + li_col - m_last) * kscale
    vs = (v.astype(F32) * ws).astype(BF16)
    c1 = decay * c0 + _dot_tn(vs, k)
    n1 = decay * n0 + jnp.sum(k.astype(F32) * ws, axis=0, keepdims=True)
    return y, c1, n1, m_last


def _log_gates(gif):
    lane = lax.broadcasted_iota(jnp.int32, gif.shape, 1)
    lg = jnp.where(lane < N_HEADS_MLSTM, gif, _log_sigmoid(gif))
    return lg, lg.T


def _gate_views(log_gates, head):
    lg, lg_t = log_gates
    f = N_HEADS_MLSTM + head
    return lg[:, head:head + 1], lg[:, f:f + 1], lg_t[head:head + 1, :], lg_t[f:f + 1, :]


def _mixers_prompt_kernel(q_ref, k0_ref, k1_ref, k2_ref, v0_ref, v1_ref, v2_ref, bias_ref,
                          qm_ref, km_ref, vm_ref, om_ref, gif_ref, gn_ref,
                          yb_ref, yc_ref, c_out, n_out, m_out, c_scr, n_scr, m_scr):
    step = pl.program_id(0)

    @pl.when(step == 0)
    def _():
        c_scr[...] = jnp.zeros_like(c_scr)
        n_scr[...] = jnp.zeros_like(n_scr)
        m_scr[...] = jnp.zeros_like(m_scr)

    log_gates = _log_gates(gif_ref[...])
    att_per_mlstm = N_HEADS_ATT // N_HEADS_MLSTM
    for h in range(N_HEADS_MLSTM):
        sl = slice(h * HEAD_DIM_MLSTM, (h + 1) * HEAD_DIM_MLSTM)
        y, c1, n1, m1 = _mlstm_block(qm_ref[:, sl], km_ref[:, sl], vm_ref[:, sl], om_ref[:, sl],
                                     _gate_views(log_gates, h), gn_ref[:, sl],
                                     c_scr[h], n_scr[h], m_scr[h][:, :1])
        yc_ref[:, sl] = y.astype(yc_ref.dtype)
        c_scr[h] = c1
        n_scr[h] = n1
        m_scr[h] = jnp.broadcast_to(m1, (1, LANES))
        for ha in range(h * att_per_mlstm, (h + 1) * att_per_mlstm):
            _attn_prompt_head(ha, q_ref, (k0_ref, k1_ref, k2_ref), (v0_ref, v1_ref, v2_ref), bias_ref, yb_ref)

    @pl.when(step == pl.num_programs(0) - 1)
    def _():
        c_out[...] = c_scr[...]
        n_out[...] = n_scr[...]
        m_out[...] = m_scr[...]


def _mixers_prompt(z, bias, gif, gnorm, layer):
    assert ATT_Q == MLSTM_L

    def z_spec(col):
        return pl.BlockSpec((ATT_Q, D_ATT), lambda j: (j, col))

    def kv_spec(col, back):
        return pl.BlockSpec((ATT_Q, D_ATT), lambda j: (jnp.maximum(j - back, 0), col))

    state_shapes = [
        jax.ShapeDtypeStruct((N_HEADS_MLSTM, HEAD_DIM_MLSTM, HEAD_DIM_MLSTM), F32),
        jax.ShapeDtypeStruct((N_HEADS_MLSTM, 1, HEAD_DIM_MLSTM), F32),
        jax.ShapeDtypeStruct((N_HEADS_MLSTM, 1, LANES), F32),
    ]
    return pl.pallas_call(
        _mixers_prompt_kernel,
        grid=(SEQ // ATT_Q,),
        in_specs=[
            z_spec(COL_Q),
            kv_spec(COL_K, 2), kv_spec(COL_K, 1), kv_spec(COL_K, 0),
            kv_spec(COL_V, 2), kv_spec(COL_V, 1), kv_spec(COL_V, 0),
            pl.BlockSpec((None, None, N_HEADS_ATT, ATT_Q, ATT_K),
                         lambda j: (0, jnp.minimum(j, ATT_VARIANTS - 1), 0, 0, 0)),
            z_spec(COL_QM), z_spec(COL_KM), z_spec(COL_VM), z_spec(COL_OM),
            pl.BlockSpec((MLSTM_L, LANES), lambda j: (j, 0)),
            _layer_spec(layer, (1, D_MLSTM)),
        ],
        out_specs=[pl.BlockSpec((ATT_Q, D_ATT), lambda j: (j, 0)), pl.BlockSpec((MLSTM_L, D_MLSTM), lambda j: (j, 0))]
        + [_const_spec(s.shape) for s in state_shapes],
        out_shape=[jax.ShapeDtypeStruct((SEQ, D_ATT), BF16), jax.ShapeDtypeStruct((SEQ, D_MLSTM), BF16)]
        + state_shapes,
        scratch_shapes=[pltpu.VMEM(s.shape, F32) for s in state_shapes],
        compiler_params=_params(("arbitrary",)),
        name="mixers_prompt",
    )(z, z, z, z, z, z, z, bias, z, z, z, z, gif, gnorm)


def _mlstm_step_kernel(q_ref, k_ref, v_ref, om_ref, gif_ref, gn_ref, c_ref, n_ref, m_ref,
                       y_ref, c_out, n_out, m_out):
    log_gates = _log_gates(gif_ref[...])
    for h in range(N_HEADS_MLSTM):
        sl = slice(h * HEAD_DIM_MLSTM, (h + 1) * HEAD_DIM_MLSTM)
        y, c1, n1, m1 = _mlstm_block(q_ref[:, sl], k_ref[:, sl], v_ref[:, sl], om_ref[:, sl],
                                     _gate_views(log_gates, h), gn_ref[:, sl],
                                     c_ref[h], n_ref[h], m_ref[h][:, :1])
        y_ref[:, sl] = y.astype(y_ref.dtype)
        c_out[h] = c1
        n_out[h] = n1
        m_out[h] = jnp.broadcast_to(m1, (1, LANES))


def _mlstm_step(z, gif, gnorm, state_c, state_n, state_m, layer):
    first = SEQ // DEC_SEQ

    def z_spec(col):
        return pl.BlockSpec((DEC_SEQ, D_MLSTM), lambda b: (first + b, col))

    def st_in(shape):
        return pl.BlockSpec((None, None) + shape, lambda b: (layer, b) + (0,) * len(shape))

    def st_out(shape):
        return pl.BlockSpec((None,) + shape, lambda b: (b,) + (0,) * len(shape))

    shapes = [(N_HEADS_MLSTM, HEAD_DIM_MLSTM, HEAD_DIM_MLSTM), (N_HEADS_MLSTM, 1, HEAD_DIM_MLSTM),
              (N_HEADS_MLSTM, 1, LANES)]
    return pl.pallas_call(
        _mlstm_step_kernel,
        grid=(DEC_BATCH,),
        in_specs=[
            z_spec(COL_QM), z_spec(COL_KM), z_spec(COL_VM), z_spec(COL_OM),
            pl.BlockSpec((DEC_SEQ, LANES), lambda b: (first + b, 0)),
            _layer_spec(layer, (1, D_MLSTM)),
        ] + [st_in(s) for s in shapes],
        out_specs=[pl.BlockSpec((DEC_SEQ, D_MLSTM), lambda b: (b, 0))] + [st_out(s) for s in shapes],
        out_shape=[jax.ShapeDtypeStruct((N_SAMPLE, D_MLSTM), BF16)]
        + [jax.ShapeDtypeStruct((DEC_BATCH,) + s, F32) for s in shapes],
        compiler_params=_params(("parallel",)),
        name="mlstm_step",
    )(z, z, z, z, gif, gnorm, state_c, state_n, state_m)


HALO = 16
GROUPS_PER_TILE = ROW_TILE // DEC_SEQ


def _merge_kernel(xa_ref, gb_ref, gc_ref, xah_ref, gch_ref, s1_ref, s2_ref, cw_ref,
                  ybp_ref, ybs_ref, ycp_ref, ycs_ref, zg_ref, xp_ref, xs_ref,
                  wpc_ref, wpa_ref, wpm_ref, wout_ref, x_out, tail_out, u_scr):
    i = pl.program_id(0)
    is_s = i >= N_PROMPT_TILES
    row = lax.broadcasted_iota(jnp.int32, (ROW_TILE, 1), 0)
    pos = jnp.where(is_s, row % DEC_SEQ, row)
    u = gc_ref[...].astype(F32) * xa_ref[...].astype(F32)
    u_halo = gch_ref[...].astype(F32) * xah_ref[...].astype(F32)
    keep = jnp.logical_and(i > 0, jnp.logical_not(is_s))
    h1 = jnp.where(keep, u_halo[HALO - 1:HALO, :], 0.0)
    h2 = jnp.where(keep, u_halo[HALO - 2:HALO - 1, :], 0.0)
    f1 = jnp.where(is_s, s1_ref[...], h1)
    f2 = jnp.where(is_s, s2_ref[...], jnp.where(row == 0, h2, h1))
    u_m1 = jnp.where(pos >= 1, pltpu.roll(u, 1, 0), f1)
    u_m2 = jnp.where(pos >= 2, pltpu.roll(u, 2, 0), f2)
    cw = cw_ref[...]
    y = cw[0:1, :] * u_m2 + cw[1:2, :] * u_m1 + cw[2:3, :] * u
    ya = gb_ref[...].astype(F32) * y

    u_scr[...] = u
    for g in range(GROUPS_PER_TILE):
        for j in range(CONV_W - 1):
            src = (g + 1) * DEC_SEQ - (CONV_W - 1) + j
            tail_out[j, g:g + 1, :] = u_scr[src:src + 1, :]

    yb = jnp.where(is_s, ybs_ref[...], ybp_ref[...])
    yc = jnp.where(is_s, ycs_ref[...], ycp_ref[...])
    g = _sigmoid(zg_ref[...].astype(F32))
    merged = (g[:, 0:D_MODEL] * _dot(ya.astype(BF16), wpc_ref[...])
              + g[:, D_MODEL:2 * D_MODEL] * _dot(yb, wpa_ref[...])
              + g[:, 2 * D_MODEL:3 * D_MODEL] * _dot(yc, wpm_ref[...]))
    x = jnp.where(is_s, xs_ref[...], xp_ref[...])
    x_out[...] = x + _dot(merged.astype(BF16), wout_ref[...])


def _merge(z, zg, s1, s2, conv_w, yb_p, yb_s, yc_p, yc_s, x_p, x_s, x_s_block, wpc, wpa, wpm, wout, layer):
    halo_blocks = ROW_TILE // HALO

    def z_spec(col):
        return pl.BlockSpec((ROW_TILE, D_CONV), lambda i: (i, col))

    def halo_spec(col):
        return pl.BlockSpec((HALO, D_CONV), lambda i: (jnp.maximum(i * halo_blocks - 1, 0), col))

    def prompt_spec(width):
        return pl.BlockSpec((ROW_TILE, width), lambda i: (jnp.minimum(i, N_PROMPT_TILES - 1), 0))

    def weight_spec(shape):
        return _layer_spec(layer, shape, pipeline_mode=pl.Buffered(1))

    return pl.pallas_call(
        _merge_kernel,
        grid=(N_TILES,),
        in_specs=[
            z_spec(COL_XA), z_spec(COL_GB), z_spec(COL_GC), halo_spec(COL_XA), halo_spec(COL_GC),
            _layer_spec(layer, (ROW_TILE, D_CONV)), _layer_spec(layer, (ROW_TILE, D_CONV)),
            _layer_spec(layer, (CONV_W, D_CONV)),
            prompt_spec(D_ATT), _const_spec((ROW_TILE, D_ATT)),
            prompt_spec(D_MLSTM), _const_spec((ROW_TILE, D_MLSTM)),
            _row_spec(N_BRANCH * D_MODEL),
            prompt_spec(D_MODEL), pl.BlockSpec((ROW_TILE, D_MODEL), lambda i: (x_s_block, 0)),
            weight_spec((D_CONV, D_MODEL)), weight_spec((D_ATT, D_MODEL)), weight_spec((D_MLSTM, D_MODEL)),
            weight_spec((D_MODEL, D_MODEL)),
        ],
        out_specs=[
            _row_spec(D_MODEL),
            pl.BlockSpec((None, CONV_W - 1, GROUPS_PER_TILE, D_CONV), lambda i: (i, 0, 0, 0)),
        ],
        out_shape=[
            jax.ShapeDtypeStruct((N_ROWS, D_MODEL), F32),
            jax.ShapeDtypeStruct((N_TILES, CONV_W - 1, GROUPS_PER_TILE, D_CONV), F32),
        ],
        scratch_shapes=[pltpu.VMEM((ROW_TILE, D_CONV), F32)],
        compiler_params=_params(("parallel",)),
        name="merge",
    )(z, z, z, z, z, s1, s2, conv_w, yb_p, yb_s, yc_p, yc_s, zg, x_p, x_s, wpc, wpa, wpm, wout)


def _granule_copies(idx_ref, first, n, src_hbm, buf, sem, slot):
    return [
        pltpu.make_async_copy(
            src_hbm.at[pl.ds(pl.multiple_of(idx_ref[first + k] * GRAN, GRAN), GRAN), :],
            buf.at[slot, pl.ds(k * GRAN, GRAN), :],
            sem.at[slot])
        for k in range(n)
    ]


GATHER_AHEAD = 2
GATHER_SLOTS = GATHER_AHEAD + 1


def _prefetched_gather(idx_ref, n, src_hbm, buf, sem, first_group=0, used_ref=None):
    step = pl.program_id(0)
    n_steps = pl.num_programs(0)

    def used(s):
        return True if used_ref is None else used_ref[jnp.minimum(s, n_steps - 1)] == 1

    def request(ahead):
        slot = (step + ahead) % GATHER_SLOTS
        for cp in _granule_copies(idx_ref, (first_group + step + ahead) * n, n, src_hbm, buf, sem, slot):
            cp.start()

    for ahead in range(GATHER_AHEAD):
        pl.when((step == 0) & (ahead < n_steps) & used(ahead))(functools.partial(request, ahead))
    pl.when((step + GATHER_AHEAD < n_steps) & used(step + GATHER_AHEAD))(functools.partial(request, GATHER_AHEAD))

    slot = step % GATHER_SLOTS

    @pl.when(used(step))
    def _():
        for cp in _granule_copies(idx_ref, (first_group + step) * n, n, src_hbm, buf, sem, slot):
            cp.wait()

    return slot


def _expert_ffn_kernel(src_ref, expert_ref, first_ref, valid_ref, next_ref, xs_hbm, wg_hbm, wu_hbm, wd_hbm,
                       o_ref, buf, sem, wg_st, wu_st, wd_st, wsem, wg_bf, wu_bf, wd_bf, *, layer):
    c = pl.program_id(0)
    slot = _prefetched_gather(src_ref, CHUNK_GRANS, xs_hbm, buf, sem, used_ref=valid_ref)

    def weight_copies(e):
        return [pltpu.make_async_copy(wg_hbm.at[layer, e], wg_st, wsem.at[0]),
                pltpu.make_async_copy(wu_hbm.at[layer, e], wu_st, wsem.at[1]),
                pltpu.make_async_copy(wd_hbm.at[layer, e], wd_st, wsem.at[2])]

    @pl.when(c == 0)
    def _():
        for cp in weight_copies(expert_ref[0]):
            cp.start()

    @pl.when(first_ref[c] == 1)
    def _():
        for cp in weight_copies(expert_ref[c]):
            cp.wait()
        wg_bf[...] = wg_st[...].astype(BF16)
        wu_bf[...] = wu_st[...].astype(BF16)
        wd_bf[...] = wd_st[...].astype(BF16)

        @pl.when(next_ref[c] >= 0)
        def _():
            for cp in weight_copies(next_ref[c]):
                cp.start()

    @pl.when(valid_ref[c] == 1)
    def _():
        rows = buf[slot]
        x = rows[:, :D_MODEL]
        meta = rows[:, D_MODEL:].astype(F32)
        w_first = meta[:, 1:2] + meta[:, 2:3] + meta[:, 3:4]
        w_second = meta[:, 5:6] + meta[:, 6:7] + meta[:, 7:8]
        w = jnp.where(meta[:, 0:1] == expert_ref[c].astype(F32), w_first, w_second)
        gate = _dot(x, wg_bf[...])
        a = gate * _sigmoid(gate) * _dot(x, wu_bf[...]) * w
        o_ref[...] = _dot(a.astype(BF16), wd_bf[...]).astype(o_ref.dtype)

    @pl.when(valid_ref[c] == 0)
    def _():
        o_ref[...] = jnp.zeros_like(o_ref)


def _expert_ffn(src, chunk_expert, chunk_first, chunk_valid, chunk_next, xs, w_gate, w_up, w_down, layer):
    any_spec = pl.BlockSpec(memory_space=pl.ANY)
    grid_spec = pltpu.PrefetchScalarGridSpec(
        num_scalar_prefetch=5,
        grid=(MAX_CHUNKS,),
        in_specs=[any_spec, any_spec, any_spec, any_spec],
        out_specs=pl.BlockSpec((CHUNK_ROWS, D_MODEL), lambda c, *_: (c, 0)),
        scratch_shapes=[
            pltpu.VMEM((GATHER_SLOTS, CHUNK_ROWS, XS_WIDTH), BF16),
            pltpu.SemaphoreType.DMA((GATHER_SLOTS,)),
            pltpu.VMEM((D_MODEL, D_EXPERT), F32),
            pltpu.VMEM((D_MODEL, D_EXPERT), F32),
            pltpu.VMEM((D_EXPERT, D_MODEL), F32),
            pltpu.SemaphoreType.DMA((3,)),
            pltpu.VMEM((D_MODEL, D_EXPERT), BF16),
            pltpu.VMEM((D_MODEL, D_EXPERT), BF16),
            pltpu.VMEM((D_EXPERT, D_MODEL), BF16),
        ],
    )
    return pl.pallas_call(
        functools.partial(_expert_ffn_kernel, layer=layer),
        grid_spec=grid_spec,
        out_shape=jax.ShapeDtypeStruct((MAX_CHUNKS * CHUNK_ROWS, D_MODEL), BF16),
        compiler_params=_params(("arbitrary",)),
        name="expert_ffn",
    )(src, chunk_expert, chunk_first, chunk_valid, chunk_next, xs, w_gate, w_up, w_down)


def _combine(dst_ref, ys_hbm, lp_ref, x_ref, buf, sem, first_tile=0):
    slot = _prefetched_gather(dst_ref, TILE_GRANS, ys_hbm, buf, sem, first_tile)
    lp = lp_ref[...]
    p_idx = lax.broadcasted_iota(jnp.int32, (ROW_TILE, LOCAL_ROWS), 1).astype(F32)
    pick = jnp.where((p_idx == lp[:, 0:1]) | (p_idx == lp[:, 1:2]), 1.0, 0.0).astype(BF16)
    return x_ref[...] + _dot(pick, buf[slot])


def _combine_norm_kernel(dst_ref, ys_hbm, lp_ref, x_ref, g_ref, whi_ref, wlo_ref, b_ref,
                         x_out, h_out, s_out, buf, sem):
    x = _combine(dst_ref, ys_hbm, lp_ref, x_ref, buf, sem)
    x_out[...] = x
    h = _rms(x, g_ref[...])
    h_out[...] = h.astype(BF16)
    s_out[...] = _small_proj(h, whi_ref, wlo_ref, b_ref)


def _combine_final_kernel(dst_ref, ys_hbm, lp_ref, x_ref, g_ref, o_ref, buf, sem, *, first_tile):
    o_ref[...] = _rms(_combine(dst_ref, ys_hbm, lp_ref, x_ref, buf, sem, first_tile), g_ref[...])


_COMBINE_SCRATCH = [pltpu.VMEM((GATHER_SLOTS, LOCAL_ROWS, D_MODEL), BF16),
                    pltpu.SemaphoreType.DMA((GATHER_SLOTS,))]


def _combine_norm(dst, ys, lp, x, g, whi, wlo, b, layer):
    def row(width):
        return pl.BlockSpec((ROW_TILE, width), lambda i, dst: (i, 0))

    grid_spec = pltpu.PrefetchScalarGridSpec(
        num_scalar_prefetch=1,
        grid=(N_TILES,),
        in_specs=[pl.BlockSpec(memory_space=pl.ANY), row(LANES), row(D_MODEL)] + _norm_param_specs(layer),
        out_specs=[row(D_MODEL), row(D_MODEL), row(LANES)],
        scratch_shapes=_COMBINE_SCRATCH,
    )
    return pl.pallas_call(
        _combine_norm_kernel,
        grid_spec=grid_spec,
        out_shape=[
            jax.ShapeDtypeStruct((N_ROWS, D_MODEL), F32),
            jax.ShapeDtypeStruct((N_ROWS, D_MODEL), BF16),
            jax.ShapeDtypeStruct((N_ROWS, LANES), F32),
        ],
        compiler_params=_params(("arbitrary",)),
        name="combine_norm",
    )(dst, ys, lp, x, g, whi, wlo, b)


def _combine_final(dst, ys, lp, x, g, first_tile, n_tiles):
    def row(width):
        return pl.BlockSpec((ROW_TILE, width), lambda i, dst: (i + first_tile, 0))

    grid_spec = pltpu.PrefetchScalarGridSpec(
        num_scalar_prefetch=1,
        grid=(n_tiles,),
        in_specs=[pl.BlockSpec(memory_space=pl.ANY), row(LANES), row(D_MODEL),
                  pl.BlockSpec((1, D_MODEL), lambda i, dst: (0, 0))],
        out_specs=pl.BlockSpec((ROW_TILE, D_MODEL), lambda i, dst: (i, 0)),
        scratch_shapes=_COMBINE_SCRATCH,
    )
    return pl.pallas_call(
        functools.partial(_combine_final_kernel, first_tile=first_tile),
        grid_spec=grid_spec,
        out_shape=jax.ShapeDtypeStruct((n_tiles * ROW_TILE, D_MODEL), F32),
        compiler_params=_params(("arbitrary",)),
        name="combine_final",
    )(dst, ys, lp, x, g)


N_SRC = MAX_CHUNKS * CHUNK_GRANS
N_DST = N_TILES * TILE_GRANS


def _granule_kernel(gcnt_ref, src_ref, dst_ref, expert_ref, first_ref, valid_ref, next_ref, local_ref):
    def fill(ref, n, value):
        def body(k, carry):
            ref[k] = value
            return carry
        lax.fori_loop(0, n, body, 0, unroll=8)

    fill(src_ref, N_SRC, TILE_GRANS - 1)
    fill(dst_ref, N_DST, 0)
    fill(expert_ref, MAX_CHUNKS, N_EXPERTS - 1)
    fill(first_ref, MAX_CHUNKS, 0)
    fill(valid_ref, MAX_CHUNKS, 0)
    fill(next_ref, MAX_CHUNKS + 1, -1)
    fill(local_ref, N_TILES, 0)

    def per_expert(e, carry):
        chunk, prev_first = carry
        slot0 = chunk * CHUNK_GRANS

        def per_tile(t, pos):
            n = gcnt_ref[t * N_EXPERTS + e]
            local0 = t * TILE_GRANS + local_ref[t]

            def per_granule(k, c2):
                src_ref[pos + k] = local0 + k
                dst_ref[local0 + k] = pos + k
                return c2

            lax.fori_loop(0, n, per_granule, 0)
            local_ref[t] = local_ref[t] + n
            return pos + n

        end = lax.fori_loop(0, N_TILES, per_tile, slot0)
        n_chunks = (end - slot0 + CHUNK_GRANS - 1) // CHUNK_GRANS

        def per_chunk(k, c2):
            expert_ref[chunk + k] = e
            valid_ref[chunk + k] = 1
            return c2

        lax.fori_loop(0, n_chunks, per_chunk, 0)
        owns = n_chunks > 0

        @pl.when(owns)
        def _():
            first_ref[chunk] = 1
            next_ref[prev_first] = e

        return chunk + n_chunks, jnp.where(owns, chunk, prev_first)

    lax.fori_loop(0, N_EXPERTS, per_expert, (0, MAX_CHUNKS))


def _granule_lists(gcnt):
    smem = pl.BlockSpec(memory_space=pltpu.SMEM)
    sizes = (N_SRC, N_DST, MAX_CHUNKS, MAX_CHUNKS, MAX_CHUNKS, MAX_CHUNKS + 1)
    return pl.pallas_call(
        _granule_kernel,
        in_specs=[smem],
        out_specs=[smem] * len(sizes),
        out_shape=[jax.ShapeDtypeStruct((n,), jnp.int32) for n in sizes],
        scratch_shapes=[pltpu.SMEM((N_TILES,), jnp.int32)],
        name="granule_lists",
    )(gcnt.reshape(N_TILES * N_EXPERTS))


def _pad_lanes(w):
    return jnp.pad(w, [(0, 0)] * (w.ndim - 1) + [(0, LANES - w.shape[-1])])


def _toeplitz_bias(table, n_q, n_k, reach):
    period = 1
    while period < n_q + n_k:
        period *= 2
    j = jnp.arange(period)
    d = jnp.where(j < n_k, j, j - period)
    u = table.reshape(-1, table.shape[-1])[:, jnp.clip(reach - d, -REL_CLIP, REL_CLIP) + REL_CLIP]
    flat = jnp.tile(u, (1, n_q))[:, :n_q * (period - 1)]
    return flat.reshape(table.shape[:-1] + (n_q, period - 1))[..., :n_k]


def _prompt_bias(table):
    bias = _toeplitz_bias(table, ATT_Q, ATT_K, ATT_REACH)
    qc = jnp.arange(ATT_Q)[:, None] // CHUNK
    m = jnp.arange(ATT_K)[None, :]
    kc = m // CHUNK
    band = (kc >= qc) & (kc <= qc + BAND_CHUNKS)
    first_key = ATT_REACH - ATT_Q * jnp.arange(ATT_VARIANTS)
    ok = band[None] & (m[None] >= first_key[:, None, None])
    return jnp.where(ok[None, :, None], bias[:, None], NEG)


def _step_bias(table, past):
    bias = _toeplitz_bias(table, DEC_SEQ, past + DEC_SEQ, past)
    return bias[..., :past], bias[..., past:]


def kernel(x_prompt, x_sample, state_conv, cache_k, cache_v, state_C, state_n, state_m, norm_mix, norm_ffn, w_in, conv_w, rel_bias, gate_bias, mlstm_norm, w_proj_conv, w_proj_att, w_proj_mlstm, w_out, router_group, router_group_bias, router_expert, router_expert_bias, w_gate, w_up, w_down, norm_final):
    past = cache_k.shape[2]
    xp = x_prompt.reshape(SEQ, D_MODEL)
    xs = x_sample.reshape(N_SAMPLE, D_MODEL)
    state_n5 = state_n.reshape(DEPTH, DEC_BATCH, N_HEADS_MLSTM, 1, HEAD_DIM_MLSTM)
    state_m5 = jnp.broadcast_to(state_m[..., None, None], (DEPTH, DEC_BATCH, N_HEADS_MLSTM, 1, LANES))

    w_in_t = jnp.swapaxes(w_in, 1, 2)

    g_mix = norm_mix[:, None, :]
    g_ffn = norm_ffn[:, None, :]
    gnorm = mlstm_norm[:, None, :]
    gif_w = jnp.swapaxes(w_in_t[:, MAIN_WIDTH:MAIN_WIDTH + GIF_WIDTH, :], 1, 2)
    gif_hi, gif_lo = _split_hi_lo(_pad_lanes(gif_w))
    gif_b = _pad_lanes(gate_bias[:, None, :])
    r_hi, r_lo = _split_hi_lo(_pad_lanes(jnp.concatenate([router_group, router_expert], axis=2)))
    r_b = _pad_lanes(jnp.concatenate([router_group_bias, router_expert_bias], axis=1)[:, None, :])
    hist = jnp.zeros((DEPTH, DEC_BATCH, DEC_SEQ, D_CONV), F32)
    s1 = hist.at[:, :, 0].set(state_conv[:, :, 1]).reshape(DEPTH, N_SAMPLE, D_CONV)
    s2 = (hist.at[:, :, 0].set(state_conv[:, :, 0]).at[:, :, 1].set(state_conv[:, :, 1])
          .reshape(DEPTH, N_SAMPLE, D_CONV))
    merge_w = (w_proj_conv.astype(BF16), w_proj_att.astype(BF16), w_proj_mlstm.astype(BF16), w_out.astype(BF16))

    x_mid = moe = None
    outs = {k: [] for k in ("p_conv", "p_k", "p_v", "p_C", "p_n", "p_m", "s_conv", "s_k", "s_v", "s_C", "s_n", "s_m")}
    for l in range(DEPTH):
        if l == 0:
            h, gif = _norm_first(xp, xs, g_mix, gif_hi, gif_lo, gif_b, l)
            residual = (xp, xs, 0)
        else:
            x, h, gif = _combine_norm(*moe, x_mid, g_mix, gif_hi, gif_lo, gif_b, l)
            residual = (x, x, N_PROMPT_TILES)
        z = _project(h, w_in_t, l, 0, MAIN_WIDTH)
        zg = _project(h, w_in_t, l, MAIN_WIDTH + GIF_WIDTH, N_BRANCH * D_MODEL)

        yb_p, yc_p, p_c, p_n, p_m = _mixers_prompt(z, _prompt_bias(rel_bias[l:l + 1]), gif, gnorm, l)
        bias_past, bias_new = _step_bias(rel_bias[l:l + 1], past)
        yb_s = _attn_step(z, cache_k, cache_v, bias_past, bias_new, l)
        yc_s, s_c, s_n, s_m = _mlstm_step(z, gif, gnorm, state_C, state_n5, state_m5, l)

        x_mid, tails = _merge(z, zg, s1, s2, conv_w, yb_p, yb_s, yc_p, yc_s, *residual, *merge_w, l)

        xs_local, lp, gcnt = _route_dispatch(x_mid, g_ffn, r_hi, r_lo, r_b, l)
        gcnt = gcnt[:, 0, N_GROUPS:N_GROUPS + N_EXPERTS].astype(jnp.int32)
        src, dst, chunk_expert, chunk_first, chunk_valid, chunk_next = _granule_lists(gcnt)
        ys = _expert_ffn(src, chunk_expert, chunk_first, chunk_valid, chunk_next, xs_local, w_gate, w_up, w_down, l)
        moe = (dst, ys, lp)

        keep = min(ATT_REACH, SEQ)
        k_rows = z[SEQ - keep:, COL_K * D_ATT:(COL_K + 1) * D_ATT].astype(F32)
        v_rows = z[SEQ - keep:, COL_V * D_ATT:(COL_V + 1) * D_ATT].astype(F32)
        outs["p_conv"].append(tails[N_PROMPT_TILES - 1, :, GROUPS_PER_TILE - 1][None])
        outs["p_k"].append(k_rows[:keep].reshape(1, keep, N_HEADS_ATT, HEAD_DIM_ATT))
        outs["p_v"].append(v_rows[:keep].reshape(1, keep, N_HEADS_ATT, HEAD_DIM_ATT))
        outs["p_C"].append(p_c[None])
        outs["p_n"].append(p_n[:, 0][None])
        outs["p_m"].append(p_m[:, 0, 0][None])
        outs["s_conv"].append(jnp.swapaxes(tails[N_PROMPT_TILES], 0, 1))
        outs["s_k"].append(k_rows[keep:].reshape(DEC_BATCH, DEC_SEQ, N_HEADS_ATT, HEAD_DIM_ATT))
        outs["s_v"].append(v_rows[keep:].reshape(DEC_BATCH, DEC_SEQ, N_HEADS_ATT, HEAD_DIM_ATT))
        outs["s_C"].append(s_c)
        outs["s_n"].append(s_n[:, :, 0])
        outs["s_m"].append(s_m[:, :, 0, 0])

    g_fin = norm_final[None, :]
    y_prompt = _combine_final(*moe, x_mid, g_fin, 0, N_PROMPT_TILES).reshape(x_prompt.shape)
    y_sample = _combine_final(*moe, x_mid, g_fin, N_PROMPT_TILES, 1).reshape(x_sample.shape)
    st = {k: jnp.stack(v) for k, v in outs.items()}
    return (y_prompt, y_sample, st["p_conv"], st["p_k"], st["p_v"], st["p_C"], st["p_n"], st["p_m"],
            st["s_conv"], st["s_k"], st["s_v"], st["s_C"], st["s_n"], st["s_m"])
```

```python
import functools
import math

import jax
import jax.numpy as jnp
from jax import lax
from jax.experimental import pallas as pl
from jax.experimental.pallas import tpu as pltpu

D_MODEL = 2048
SEQ = 8192
DEPTH = 2
DEC_BATCH = 8
DEC_SEQ = 32
N_SAMPLE = DEC_BATCH * DEC_SEQ
N_ROWS = SEQ + N_SAMPLE

CHUNK = 64
D_CONV = 1024
CONV_W = 3
N_HEADS_ATT = 8
HEAD_DIM_ATT = 128
D_ATT = N_HEADS_ATT * HEAD_DIM_ATT
BAND_CHUNKS = 8
ATT_REACH = BAND_CHUNKS * CHUNK
REL_CLIP = 128
N_HEADS_MLSTM = 4
HEAD_DIM_MLSTM = 256
D_MLSTM = N_HEADS_MLSTM * HEAD_DIM_MLSTM
N_BRANCH = 3
MAIN_WIDTH = 3 * D_CONV + 3 * D_ATT + 4 * D_MLSTM
GIF_WIDTH = 2 * N_HEADS_MLSTM
N_GROUPS = 4
EXPERTS_PER_GROUP = 4
N_EXPERTS = N_GROUPS * EXPERTS_PER_GROUP
D_EXPERT = 512
EPS = 1e-6

LANES = 128
BF16_ROWS = 16
ROW_TILE = 256
N_TILES = N_ROWS // ROW_TILE
N_PROMPT_TILES = SEQ // ROW_TILE
MM_ROWS = 1056
MM_COLS = 1024
ATT_Q = 256
ATT_K = ATT_Q + ATT_REACH
ATT_VARIANTS = ATT_REACH // ATT_Q + 1
MLSTM_L = 256
NEG = -1e30
VMEM_LIMIT = 56 * 1024 * 1024

GRAN = BF16_ROWS
XS_WIDTH = D_MODEL + LANES
MAX_RUN_PAD = N_EXPERTS * (GRAN - 1)
TILE_GRANS = (2 * ROW_TILE + MAX_RUN_PAD) // GRAN + 1
LOCAL_ROWS = TILE_GRANS * GRAN
CHUNK_GRANS = 16
CHUNK_ROWS = CHUNK_GRANS * GRAN
MAX_CHUNKS = (N_TILES * (TILE_GRANS - 1)) // CHUNK_GRANS + N_EXPERTS

COL_XA, COL_GB, COL_GC, COL_Q, COL_K, COL_V, COL_QM, COL_KM, COL_VM, COL_OM = range(10)

F32 = jnp.float32
BF16 = jnp.bfloat16


def _params(sem):
    return pltpu.CompilerParams(dimension_semantics=sem, vmem_limit_bytes=VMEM_LIMIT)


def _dot(a, b):
    return jnp.dot(a, b, preferred_element_type=F32)


def _dot_nt(a, b):
    return lax.dot_general(a, b, (((1,), (1,)), ((), ())), preferred_element_type=F32)


def _dot_tn(a, b):
    return lax.dot_general(a, b, (((0,), (0,)), ((), ())), preferred_element_type=F32)


def _split_hi_lo(w):
    hi = w.astype(BF16)
    lo = (w - hi.astype(F32)).astype(BF16)
    return hi, lo


def _sigmoid(x):
    return 1.0 / (1.0 + jnp.exp(-x))


def _log_sigmoid(x):
    return jnp.minimum(x, 0.0) - jnp.log1p(jnp.exp(-jnp.abs(x)))


def _rms(x, g):
    ms = jnp.mean(x * x, axis=-1, keepdims=True)
    return x * lax.rsqrt(ms + EPS) * g


def _small_proj(h, whi_ref, wlo_ref, b_ref):
    h_hi = h.astype(BF16)
    h_lo = (h - h_hi.astype(F32)).astype(BF16)
    whi = whi_ref[...]
    return _dot(h_hi, whi) + _dot(h_lo, whi) + _dot(h_hi, wlo_ref[...]) + b_ref[...]


def _norm_first_kernel(xp_ref, xs_ref, g_ref, whi_ref, wlo_ref, b_ref, h_out, s_out):
    i = pl.program_id(0)
    x = jnp.where(i < N_PROMPT_TILES, xp_ref[...], xs_ref[...])
    h = _rms(x, g_ref[...])
    h_out[...] = h.astype(BF16)
    s_out[...] = _small_proj(h, whi_ref, wlo_ref, b_ref)


def _route(logits):
    lane = lax.broadcasted_iota(jnp.int32, logits.shape, 1)
    is_g = lane < N_GROUPS
    gl = jnp.where(is_g, logits, NEG)
    gmax = jnp.max(gl, axis=1, keepdims=True)
    g_sel = jnp.min(jnp.where(is_g & (gl == gmax), lane, LANES), axis=1, keepdims=True)
    p_g = 1.0 / jnp.sum(jnp.where(is_g, jnp.exp(gl - gmax), 0.0), axis=1, keepdims=True)
    e_lane = lane - N_GROUPS
    in_g = (e_lane >= 0) & (e_lane < N_EXPERTS) & ((e_lane // EXPERTS_PER_GROUP) == g_sel)
    e1 = jnp.max(jnp.where(in_g, logits, NEG), axis=1, keepdims=True)
    i1 = jnp.min(jnp.where(in_g & (logits == e1), lane, LANES), axis=1, keepdims=True)
    rest = in_g & (lane != i1)
    e2 = jnp.max(jnp.where(rest, logits, NEG), axis=1, keepdims=True)
    i2 = jnp.min(jnp.where(rest & (logits == e2), lane, LANES), axis=1, keepdims=True)
    r = jnp.exp(e2 - e1)
    w1 = p_g / (1.0 + r)
    w2 = w1 * r
    return i1, i2, w1, w2


def _bf16_pieces(w):
    a = w.astype(BF16).astype(F32)
    b = (w - a).astype(BF16).astype(F32)
    return a, b, w - a - b


def _route_dispatch_tiles_lockstep(xs, g_ref, whi_ref, wlo_ref, b_ref, outs):
    tiles = range(len(xs))
    lane = lax.broadcasted_iota(jnp.int32, (ROW_TILE, LANES), 1)
    t_idx = lax.broadcasted_iota(jnp.int32, (ROW_TILE, ROW_TILE), 0)
    s_idx = lax.broadcasted_iota(jnp.int32, (ROW_TILE, ROW_TILE), 1)
    earlier = jnp.where(s_idx < t_idx, 1.0, 0.0).astype(BF16)
    a_idx = lax.broadcasted_iota(jnp.int32, (LANES, LANES), 0)
    b_idx = lax.broadcasted_iota(jnp.int32, (LANES, LANES), 1)
    before = jnp.where(a_idx < b_idx, 1.0, 0.0).astype(BF16)
    p_idx = lax.broadcasted_iota(jnp.int32, (LOCAL_ROWS, ROW_TILE), 0).astype(F32)

    hs = [_rms(xs[t], g_ref[...]) for t in tiles]
    logits = [_small_proj(hs[t], whi_ref, wlo_ref, b_ref) for t in tiles]
    routes = [_route(logits[t]) for t in tiles]

    lps = []
    for t in tiles:
        i1, i2, _, _ = routes[t]
        o1 = lane == i1
        o2 = lane == i2
        onehot = jnp.where(o1 | o2, 1.0, 0.0)
        rank = _dot(earlier, onehot.astype(BF16))
        gcnt = jnp.floor((jnp.sum(onehot, axis=0, keepdims=True) + (GRAN - 1)) * (1.0 / GRAN))
        gcnt8 = jnp.broadcast_to(gcnt, (8, LANES))
        run_start = _dot(gcnt8.astype(BF16), before)[0:1, :] * GRAN
        pos = run_start + rank
        lpos1 = jnp.sum(jnp.where(o1, pos, 0.0), axis=1, keepdims=True)
        lpos2 = jnp.sum(jnp.where(o2, pos, 0.0), axis=1, keepdims=True)
        lp = jnp.where(lane == 0, lpos1, jnp.where(lane == 1, lpos2, -1.0))
        outs[t][1][...] = lp
        outs[t][2][...] = gcnt8
        lps.append(lp)

    rows = []
    for t in tiles:
        i1, i2, w1, w2 = routes[t]
        meta = jnp.zeros((ROW_TILE, LANES), F32)
        fields = (((i1 - N_GROUPS).astype(F32),) + _bf16_pieces(w1)
                  + ((i2 - N_GROUPS).astype(F32),) + _bf16_pieces(w2))
        for k, val in enumerate(fields):
            meta = jnp.where(lane == k, val, meta)
        rows.append(jnp.concatenate([hs[t].astype(BF16), meta.astype(BF16)], axis=1))

    perms = []
    for t in tiles:
        lp_t = lps[t].T
        perms.append(jnp.where((p_idx == lp_t[0:1, :]) | (p_idx == lp_t[1:2, :]), 1.0, 0.0).astype(BF16))
    for t in tiles:
        outs[t][0][...] = _dot(perms[t], rows[t]).astype(BF16)


def _row_spec(width):
    return pl.BlockSpec((ROW_TILE, width), lambda i: (i, 0))


def _const_spec(shape):
    return pl.BlockSpec(shape, lambda i: (0,) * len(shape))


def _layer_spec(layer, shape, **kwargs):
    return pl.BlockSpec((None,) + tuple(shape), lambda *_: (layer,) + (0,) * len(shape), **kwargs)


def _norm_param_specs(layer):
    return [_layer_spec(layer, (1, D_MODEL)), _layer_spec(layer, (D_MODEL, LANES)),
            _layer_spec(layer, (D_MODEL, LANES)), _layer_spec(layer, (1, LANES))]


def _norm_first(xp, xs, g, whi, wlo, b, layer):
    return pl.pallas_call(
        _norm_first_kernel,
        grid=(N_TILES,),
        in_specs=[
            pl.BlockSpec((ROW_TILE, D_MODEL), lambda i: (jnp.minimum(i, N_PROMPT_TILES - 1), 0)),
            _const_spec((ROW_TILE, D_MODEL)),
        ] + _norm_param_specs(layer),
        out_specs=[_row_spec(D_MODEL), _row_spec(LANES)],
        out_shape=[
            jax.ShapeDtypeStruct((N_ROWS, D_MODEL), BF16),
            jax.ShapeDtypeStruct((N_ROWS, LANES), F32),
        ],
        compiler_params=_params(("parallel",)),
        name="norm_first",
    )(xp, xs, g, whi, wlo, b)


DISPATCH_TILES = 3


def _route_dispatch_kernel(x_ref, g_ref, whi_ref, wlo_ref, b_ref, xs_out, lp_out, cnt_out):
    tiles = range(DISPATCH_TILES)
    _route_dispatch_tiles_lockstep(
        [x_ref[t * ROW_TILE:(t + 1) * ROW_TILE, :] for t in tiles], g_ref, whi_ref, wlo_ref, b_ref,
        [(xs_out.at[pl.ds(t * LOCAL_ROWS, LOCAL_ROWS), :], lp_out.at[pl.ds(t * ROW_TILE, ROW_TILE), :], cnt_out.at[t])
         for t in tiles])


def _route_dispatch(x, g, whi, wlo, b, layer):
    def spec(rows, width):
        return pl.BlockSpec((DISPATCH_TILES * rows, width), lambda i: (i, 0))

    return pl.pallas_call(
        _route_dispatch_kernel,
        grid=(N_TILES // DISPATCH_TILES,),
        in_specs=[spec(ROW_TILE, D_MODEL)] + _norm_param_specs(layer),
        out_specs=[
            spec(LOCAL_ROWS, XS_WIDTH),
            spec(ROW_TILE, LANES),
            pl.BlockSpec((DISPATCH_TILES, 8, LANES), lambda i: (i, 0, 0)),
        ],
        out_shape=[
            jax.ShapeDtypeStruct((N_TILES * LOCAL_ROWS, XS_WIDTH), BF16),
            jax.ShapeDtypeStruct((N_ROWS, LANES), F32),
            jax.ShapeDtypeStruct((N_TILES, 8, LANES), F32),
        ],
        compiler_params=_params(("parallel",)),
        name="route_dispatch",
    )(x, g, whi, wlo, b)


def _mm_kernel(h_ref, wt_ref, o_ref, wb_ref):
    @pl.when(pl.program_id(1) == 0)
    def _():
        wb_ref[...] = wt_ref[0].T.astype(BF16)

    o_ref[...] = _dot(h_ref[...], wb_ref[...]).astype(o_ref.dtype)


def _project(h, wt, layer, row0, n_cols):
    return pl.pallas_call(
        _mm_kernel,
        grid=(n_cols // MM_COLS, N_ROWS // MM_ROWS),
        in_specs=[
            pl.BlockSpec((MM_ROWS, D_MODEL), lambda j, i: (i, 0)),
            pl.BlockSpec((pl.Element(1), pl.Element(MM_COLS), pl.Element(D_MODEL)),
                         lambda j, i: (layer, pl.multiple_of(row0 + j * MM_COLS, 8), 0)),
        ],
        out_specs=pl.BlockSpec((MM_ROWS, MM_COLS), lambda j, i: (i, j)),
        out_shape=jax.ShapeDtypeStruct((N_ROWS, n_cols), BF16),
        scratch_shapes=[pltpu.VMEM((D_MODEL, MM_COLS), BF16)],
        compiler_params=_params(("parallel", "arbitrary")),
        name="project",
    )(h, wt)


def _attn_prompt_head(h, q_ref, k_refs, v_refs, bias_ref, o_ref):
    sl = slice(h * HEAD_DIM_ATT, (h + 1) * HEAD_DIM_ATT)
    q = q_ref[:, sl]
    kk = jnp.concatenate([r[:, sl] for r in k_refs], axis=0)
    vv = jnp.concatenate([r[:, sl] for r in v_refs], axis=0)
    s = _dot_nt(q, kk) * (HEAD_DIM_ATT ** -0.5) + bias_ref[h]
    mx = jnp.max(s, axis=1, keepdims=True)
    p = jnp.exp(s - mx).astype(BF16)
    od = _dot(p, jnp.concatenate([vv, jnp.ones((ATT_K, HEAD_DIM_ATT), BF16)], axis=1))
    o = od[:, :HEAD_DIM_ATT] * (1.0 / od[:, HEAD_DIM_ATT:HEAD_DIM_ATT + 1])
    o_ref[:, sl] = o.astype(o_ref.dtype)


def _attn_step_kernel(q_ref, k_ref, v_ref, ck_ref, cv_ref, bp_ref, bn_ref, o_ref):
    scale = HEAD_DIM_ATT ** -0.5
    ck = jnp.swapaxes(ck_ref[...], 0, 1).astype(BF16)
    cv = jnp.swapaxes(cv_ref[...], 0, 1).astype(BF16)
    for h in range(N_HEADS_ATT):
        sl = slice(h * HEAD_DIM_ATT, (h + 1) * HEAD_DIM_ATT)
        q = q_ref[:, sl]
        s_past = _dot_nt(q, ck[h]) * scale + bp_ref[h]
        s_new = _dot_nt(q, k_ref[:, sl]) * scale + bn_ref[h]
        mx = jnp.maximum(jnp.max(s_past, axis=1, keepdims=True), jnp.max(s_new, axis=1, keepdims=True))
        p_past = jnp.exp(s_past - mx)
        p_new = jnp.exp(s_new - mx)
        den = jnp.sum(p_past, axis=1, keepdims=True) + jnp.sum(p_new, axis=1, keepdims=True)
        o = _dot(p_past.astype(BF16), cv[h]) + _dot(p_new.astype(BF16), v_ref[:, sl])
        o_ref[:, sl] = (o / den).astype(o_ref.dtype)


def _attn_step(z, cache_k, cache_v, bias_past, bias_new, layer):
    first = SEQ // DEC_SEQ
    past = cache_k.shape[2]

    def z_spec(col):
        return pl.BlockSpec((DEC_SEQ, D_ATT), lambda b: (first + b, col))

    cache_spec = pl.BlockSpec((None, None, past, N_HEADS_ATT, HEAD_DIM_ATT), lambda b: (layer, b, 0, 0, 0))
    return pl.pallas_call(
        _attn_step_kernel,
        grid=(DEC_BATCH,),
        in_specs=[
            z_spec(COL_Q), z_spec(COL_K), z_spec(COL_V), cache_spec, cache_spec,
            _layer_spec(0, (N_HEADS_ATT, DEC_SEQ, past)),
            _layer_spec(0, (N_HEADS_ATT, DEC_SEQ, DEC_SEQ)),
        ],
        out_specs=pl.BlockSpec((DEC_SEQ, D_ATT), lambda b: (b, 0)),
        out_shape=jax.ShapeDtypeStruct((N_SAMPLE, D_ATT), BF16),
        compiler_params=_params(("parallel",)),
        name="attn_step",
    )(z, z, z, cache_k, cache_v, bias_past, bias_new)


def _mlstm_block(q, k, v, om, gates, gnorm, c0, n0, m0):
    li_col, lf_col, li_row, lf_row = gates
    L = q.shape[0]
    kscale = HEAD_DIM_MLSTM ** -0.5
    t_idx = lax.broadcasted_iota(jnp.int32, (L, L), 0)
    s_idx = lax.broadcasted_iota(jnp.int32, (L, L), 1)
    causal = s_idx <= t_idx
    b_col = jnp.sum(jnp.where(causal, lf_row, 0.0), axis=1, keepdims=True)
    b_row = jnp.sum(jnp.where(t_idx <= s_idx, lf_col, 0.0), axis=0, keepdims=True)
    d = jnp.where(causal, b_col - b_row + li_row, NEG)
    inter = b_col + m0
    m_col = jnp.maximum(inter, jnp.max(d, axis=1, keepdims=True))
    w = jnp.exp(d - (m_col - math.log(kscale)))
    sc = jnp.exp(inter - m_col)
    qk = _dot_nt(q, k) * w
    num = sc * _dot_nt(q, c0.astype(BF16)) + _dot(qk.astype(BF16), v)
    qn = _dot_nt(q, jnp.broadcast_to(n0, (BF16_ROWS, HEAD_DIM_MLSTM)).astype(BF16))[:, 0:1]
    den = sc * qn + jnp.sum(qk, axis=1, keepdims=True)
    hh = num * (1.0 / jnp.maximum(jnp.abs(den), jnp.exp(-m_col)))
    mu = jnp.mean(hh, axis=1, keepdims=True)
    cen = hh - mu
    var = jnp.mean(cen * cen, axis=1, keepdims=True)
    y = _sigmoid(om.astype(F32)) * (cen * lax.rsqrt(var + EPS) * gnorm)
    m_last = m_col[L - 1:L, :]
    b_last = b_col[L - 1:L, :]
    decay = jnp.exp(b_last + m0 - m_last)
    ws = jnp.exp(b_last - b_col + li_col - m_last) * kscale
    vs = (v.astype(F32) * ws).astype(BF16)
    c1 = decay * c0 + _dot_tn(vs, k)
    n1 = decay * n0 + jnp.sum(k.astype(F32) * ws, axis=0, keepdims=True)
    return y, c1, n1, m_last


def _log_gates(gif):
    lane = lax.broadcasted_iota(jnp.int32, gif.shape, 1)
    lg = jnp.where(lane < N_HEADS_MLSTM, gif, _log_sigmoid(gif))
    return lg, lg.T


def _gate_views(log_gates, head):
    lg, lg_t = log_gates
    f = N_HEADS_MLSTM + head
    return lg[:, head:head + 1], lg[:, f:f + 1], lg_t[head:head + 1, :], lg_t[f:f + 1, :]


def _mixers_prompt_kernel(q_ref, k0_ref, k1_ref, k2_ref, v0_ref, v1_ref, v2_ref, bias_ref,
                          qm_ref, km_ref, vm_ref, om_ref, gif_ref, gn_ref,
                          yb_ref, yc_ref, c_out, n_out, m_out, c_scr, n_scr, m_scr):
    step = pl.program_id(0)

    @pl.when(step == 0)
    def _():
        c_scr[...] = jnp.zeros_like(c_scr)
        n_scr[...] = jnp.zeros_like(n_scr)
        m_scr[...] = jnp.zeros_like(m_scr)

    log_gates = _log_gates(gif_ref[...])
    att_per_mlstm = N_HEADS_ATT // N_HEADS_MLSTM
    for h in range(N_HEADS_MLSTM):
        _attn_prompt_head(h * att_per_mlstm, q_ref, (k0_ref, k1_ref, k2_ref), (v0_ref, v1_ref, v2_ref),
                          bias_ref, yb_ref)
        sl = slice(h * HEAD_DIM_MLSTM, (h + 1) * HEAD_DIM_MLSTM)
        y, c1, n1, m1 = _mlstm_block(qm_ref[:, sl], km_ref[:, sl], vm_ref[:, sl], om_ref[:, sl],
                                     _gate_views(log_gates, h), gn_ref[:, sl],
                                     c_scr[h], n_scr[h], m_scr[h][:, :1])
        yc_ref[:, sl] = y.astype(yc_ref.dtype)
        c_scr[h] = c1
        n_scr[h] = n1
        m_scr[h] = jnp.broadcast_to(m1, (1, LANES))
        for ha in range(h * att_per_mlstm + 1, (h + 1) * att_per_mlstm):
            _attn_prompt_head(ha, q_ref, (k0_ref, k1_ref, k2_ref), (v0_ref, v1_ref, v2_ref), bias_ref, yb_ref)

    @pl.when(step == pl.num_programs(0) - 1)
    def _():
        c_out[...] = c_scr[...]
        n_out[...] = n_scr[...]
        m_out[...] = m_scr[...]


def _mixers_prompt(z, bias, gif, gnorm, layer):
    assert ATT_Q == MLSTM_L

    def z_spec(col):
        return pl.BlockSpec((ATT_Q, D_ATT), lambda j: (j, col))

    def kv_spec(col, back):
        return pl.BlockSpec((ATT_Q, D_ATT), lambda j: (jnp.maximum(j - back, 0), col))

    state_shapes = [
        jax.ShapeDtypeStruct((N_HEADS_MLSTM, HEAD_DIM_MLSTM, HEAD_DIM_MLSTM), F32),
        jax.ShapeDtypeStruct((N_HEADS_MLSTM, 1, HEAD_DIM_MLSTM), F32),
        jax.ShapeDtypeStruct((N_HEADS_MLSTM, 1, LANES), F32),
    ]
    return pl.pallas_call(
        _mixers_prompt_kernel,
        grid=(SEQ // ATT_Q,),
        in_specs=[
            z_spec(COL_Q),
            kv_spec(COL_K, 2), kv_spec(COL_K, 1), kv_spec(COL_K, 0),
            kv_spec(COL_V, 2), kv_spec(COL_V, 1), kv_spec(COL_V, 0),
            pl.BlockSpec((None, None, N_HEADS_ATT, ATT_Q, ATT_K),
                         lambda j: (0, jnp.minimum(j, ATT_VARIANTS - 1), 0, 0, 0)),
            z_spec(COL_QM), z_spec(COL_KM), z_spec(COL_VM), z_spec(COL_OM),
            pl.BlockSpec((MLSTM_L, LANES), lambda j: (j, 0)),
            _layer_spec(layer, (1, D_MLSTM)),
        ],
        out_specs=[pl.BlockSpec((ATT_Q, D_ATT), lambda j: (j, 0)), pl.BlockSpec((MLSTM_L, D_MLSTM), lambda j: (j, 0))]
        + [_const_spec(s.shape) for s in state_shapes],
        out_shape=[jax.ShapeDtypeStruct((SEQ, D_ATT), BF16), jax.ShapeDtypeStruct((SEQ, D_MLSTM), BF16)]
        + state_shapes,
        scratch_shapes=[pltpu.VMEM(s.shape, F32) for s in state_shapes],
        compiler_params=_params(("arbitrary",)),
        name="mixers_prompt",
    )(z, z, z, z, z, z, z, bias, z, z, z, z, gif, gnorm)


def _mlstm_step_kernel(q_ref, k_ref, v_ref, om_ref, gif_ref, gn_ref, c_ref, n_ref, m_ref,
                       y_ref, c_out, n_out, m_out):
    log_gates = _log_gates(gif_ref[...])
    for h in range(N_HEADS_MLSTM):
        sl = slice(h * HEAD_DIM_MLSTM, (h + 1) * HEAD_DIM_MLSTM)
        y, c1, n1, m1 = _mlstm_block(q_ref[:, sl], k_ref[:, sl], v_ref[:, sl], om_ref[:, sl],
                                     _gate_views(log_gates, h), gn_ref[:, sl],
                                     c_ref[h], n_ref[h], m_ref[h][:, :1])
        y_ref[:, sl] = y.astype(y_ref.dtype)
        c_out[h] = c1
        n_out[h] = n1
        m_out[h] = jnp.broadcast_to(m1, (1, LANES))


def _mlstm_step(z, gif, gnorm, state_c, state_n, state_m, layer):
    first = SEQ // DEC_SEQ

    def z_spec(col):
        return pl.BlockSpec((DEC_SEQ, D_MLSTM), lambda b: (first + b, col))

    def st_in(shape):
        return pl.BlockSpec((None, None) + shape, lambda b: (layer, b) + (0,) * len(shape))

    def st_out(shape):
        return pl.BlockSpec((None,) + shape, lambda b: (b,) + (0,) * len(shape))

    shapes = [(N_HEADS_MLSTM, HEAD_DIM_MLSTM, HEAD_DIM_MLSTM), (N_HEADS_MLSTM, 1, HEAD_DIM_MLSTM),
              (N_HEADS_MLSTM, 1, LANES)]
    return pl.pallas_call(
        _mlstm_step_kernel,
        grid=(DEC_BATCH,),
        in_specs=[
            z_spec(COL_QM), z_spec(COL_KM), z_spec(COL_VM), z_spec(COL_OM),
            pl.BlockSpec((DEC_SEQ, LANES), lambda b: (first + b, 0)),
            _layer_spec(layer, (1, D_MLSTM)),
        ] + [st_in(s) for s in shapes],
        out_specs=[pl.BlockSpec((DEC_SEQ, D_MLSTM), lambda b: (b, 0))] + [st_out(s) for s in shapes],
        out_shape=[jax.ShapeDtypeStruct((N_SAMPLE, D_MLSTM), BF16)]
        + [jax.ShapeDtypeStruct((DEC_BATCH,) + s, F32) for s in shapes],
        compiler_params=_params(("parallel",)),
        name="mlstm_step",
    )(z, z, z, z, gif, gnorm, state_c, state_n, state_m)


HALO = 16
GROUPS_PER_TILE = ROW_TILE // DEC_SEQ


def _merge_kernel(xa_ref, gb_ref, gc_ref, xah_ref, gch_ref, s1_ref, s2_ref, cw_ref,
                  ybp_ref, ybs_ref, ycp_ref, ycs_ref, zg_ref, xp_ref, xs_ref,
                  wpc_ref, wpa_ref, wpm_ref, wout_ref, x_out, tail_out, u_scr):
    i = pl.program_id(0)
    is_s = i >= N_PROMPT_TILES
    row = lax.broadcasted_iota(jnp.int32, (ROW_TILE, 1), 0)
    pos = jnp.where(is_s, row % DEC_SEQ, row)
    u = gc_ref[...].astype(F32) * xa_ref[...].astype(F32)
    u_halo = gch_ref[...].astype(F32) * xah_ref[...].astype(F32)
    keep = jnp.logical_and(i > 0, jnp.logical_not(is_s))
    h1 = jnp.where(keep, u_halo[HALO - 1:HALO, :], 0.0)
    h2 = jnp.where(keep, u_halo[HALO - 2:HALO - 1, :], 0.0)
    f1 = jnp.where(is_s, s1_ref[...], h1)
    f2 = jnp.where(is_s, s2_ref[...], jnp.where(row == 0, h2, h1))
    u_m1 = jnp.where(pos >= 1, pltpu.roll(u, 1, 0), f1)
    u_m2 = jnp.where(pos >= 2, pltpu.roll(u, 2, 0), f2)
    cw = cw_ref[...]
    y = cw[0:1, :] * u_m2 + cw[1:2, :] * u_m1 + cw[2:3, :] * u
    ya = gb_ref[...].astype(F32) * y

    u_scr[...] = u
    for g in range(GROUPS_PER_TILE):
        for j in range(CONV_W - 1):
            src = (g + 1) * DEC_SEQ - (CONV_W - 1) + j
            tail_out[j, g:g + 1, :] = u_scr[src:src + 1, :]

    yb = jnp.where(is_s, ybs_ref[...], ybp_ref[...])
    yc = jnp.where(is_s, ycs_ref[...], ycp_ref[...])
    g = _sigmoid(zg_ref[...].astype(F32))
    merged = (g[:, 0:D_MODEL] * _dot(ya.astype(BF16), wpc_ref[...])
              + g[:, D_MODEL:2 * D_MODEL] * _dot(yb, wpa_ref[...])
              + g[:, 2 * D_MODEL:3 * D_MODEL] * _dot(yc, wpm_ref[...]))
    x = jnp.where(is_s, xs_ref[...], xp_ref[...])
    x_out[...] = x + _dot(merged.astype(BF16), wout_ref[...])


def _merge(z, zg, s1, s2, conv_w, yb_p, yb_s, yc_p, yc_s, x_p, x_s, x_s_block, wpc, wpa, wpm, wout, layer):
    halo_blocks = ROW_TILE // HALO

    def z_spec(col):
        return pl.BlockSpec((ROW_TILE, D_CONV), lambda i: (i, col))

    def halo_spec(col):
        return pl.BlockSpec((HALO, D_CONV), lambda i: (jnp.maximum(i * halo_blocks - 1, 0), col))

    def prompt_spec(width):
        return pl.BlockSpec((ROW_TILE, width), lambda i: (jnp.minimum(i, N_PROMPT_TILES - 1), 0))

    def weight_spec(shape):
        return _layer_spec(layer, shape, pipeline_mode=pl.Buffered(1))

    return pl.pallas_call(
        _merge_kernel,
        grid=(N_TILES,),
        in_specs=[
            z_spec(COL_XA), z_spec(COL_GB), z_spec(COL_GC), halo_spec(COL_XA), halo_spec(COL_GC),
            _layer_spec(layer, (ROW_TILE, D_CONV)), _layer_spec(layer, (ROW_TILE, D_CONV)),
            _layer_spec(layer, (CONV_W, D_CONV)),
            prompt_spec(D_ATT), _const_spec((ROW_TILE, D_ATT)),
            prompt_spec(D_MLSTM), _const_spec((ROW_TILE, D_MLSTM)),
            _row_spec(N_BRANCH * D_MODEL),
            prompt_spec(D_MODEL), pl.BlockSpec((ROW_TILE, D_MODEL), lambda i: (x_s_block, 0)),
            weight_spec((D_CONV, D_MODEL)), weight_spec((D_ATT, D_MODEL)), weight_spec((D_MLSTM, D_MODEL)),
            weight_spec((D_MODEL, D_MODEL)),
        ],
        out_specs=[
            _row_spec(D_MODEL),
            pl.BlockSpec((None, CONV_W - 1, GROUPS_PER_TILE, D_CONV), lambda i: (i, 0, 0, 0)),
        ],
        out_shape=[
            jax.ShapeDtypeStruct((N_ROWS, D_MODEL), F32),
            jax.ShapeDtypeStruct((N_TILES, CONV_W - 1, GROUPS_PER_TILE, D_CONV), F32),
        ],
        scratch_shapes=[pltpu.VMEM((ROW_TILE, D_CONV), F32)],
        compiler_params=_params(("parallel",)),
        name="merge",
    )(z, z, z, z, z, s1, s2, conv_w, yb_p, yb_s, yc_p, yc_s, zg, x_p, x_s, wpc, wpa, wpm, wout)


def _granule_copies(idx_ref, first, n, src_hbm, buf, sem, slot):
    return [
        pltpu.make_async_copy(
            src_hbm.at[pl.ds(pl.multiple_of(idx_ref[first + k] * GRAN, GRAN), GRAN), :],
            buf.at[slot, pl.ds(k * GRAN, GRAN), :],
            sem.at[slot])
        for k in range(n)
    ]


GATHER_AHEAD = 2
GATHER_SLOTS = GATHER_AHEAD + 1


def _prefetched_gather(idx_ref, n, src_hbm, buf, sem, first_group=0, used_ref=None):
    step = pl.program_id(0)
    n_steps = pl.num_programs(0)

    def used(s):
        return True if used_ref is None else used_ref[jnp.minimum(s, n_steps - 1)] == 1

    def request(ahead):
        slot = (step + ahead) % GATHER_SLOTS
        for cp in _granule_copies(idx_ref, (first_group + step + ahead) * n, n, src_hbm, buf, sem, slot):
            cp.start()

    for ahead in range(GATHER_AHEAD):
        pl.when((step == 0) & (ahead < n_steps) & used(ahead))(functools.partial(request, ahead))
    pl.when((step + GATHER_AHEAD < n_steps) & used(step + GATHER_AHEAD))(functools.partial(request, GATHER_AHEAD))

    slot = step % GATHER_SLOTS

    @pl.when(used(step))
    def _():
        for cp in _granule_copies(idx_ref, (first_group + step) * n, n, src_hbm, buf, sem, slot):
            cp.wait()

    return slot


def _expert_ffn_kernel(src_ref, expert_ref, first_ref, valid_ref, next_ref, xs_hbm, wg_hbm, wu_hbm, wd_hbm,
                       o_ref, buf, sem, wg_st, wu_st, wd_st, wsem, wg_bf, wu_bf, wd_bf, *, layer):
    c = pl.program_id(0)
    slot = _prefetched_gather(src_ref, CHUNK_GRANS, xs_hbm, buf, sem, used_ref=valid_ref)

    def weight_copies(e):
        return [pltpu.make_async_copy(wg_hbm.at[layer, e], wg_st, wsem.at[0]),
                pltpu.make_async_copy(wu_hbm.at[layer, e], wu_st, wsem.at[1]),
                pltpu.make_async_copy(wd_hbm.at[layer, e], wd_st, wsem.at[2])]

    @pl.when(c == 0)
    def _():
        for cp in weight_copies(expert_ref[0]):
            cp.start()

    @pl.when(first_ref[c] == 1)
    def _():
        for cp in weight_copies(expert_ref[c]):
            cp.wait()
        wg_bf[...] = wg_st[...].astype(BF16)
        wu_bf[...] = wu_st[...].astype(BF16)
        wd_bf[...] = wd_st[...].astype(BF16)

        @pl.when(next_ref[c] >= 0)
        def _():
            for cp in weight_copies(next_ref[c]):
                cp.start()

    @pl.when(valid_ref[c] == 1)
    def _():
        rows = buf[slot]
        x = rows[:, :D_MODEL]
        meta = rows[:, D_MODEL:].astype(F32)
        w_first = meta[:, 1:2] + meta[:, 2:3] + meta[:, 3:4]
        w_second = meta[:, 5:6] + meta[:, 6:7] + meta[:, 7:8]
        w = jnp.where(meta[:, 0:1] == expert_ref[c].astype(F32), w_first, w_second)
        gate = _dot(x, wg_bf[...])
        a = gate * _sigmoid(gate) * _dot(x, wu_bf[...]) * w
        o_ref[...] = _dot(a.astype(BF16), wd_bf[...]).astype(o_ref.dtype)

    @pl.when(valid_ref[c] == 0)
    def _():
        o_ref[...] = jnp.zeros_like(o_ref)


def _expert_ffn(src, chunk_expert, chunk_first, chunk_valid, chunk_next, xs, w_gate, w_up, w_down, layer):
    any_spec = pl.BlockSpec(memory_space=pl.ANY)
    grid_spec = pltpu.PrefetchScalarGridSpec(
        num_scalar_prefetch=5,
        grid=(MAX_CHUNKS,),
        in_specs=[any_spec, any_spec, any_spec, any_spec],
        out_specs=pl.BlockSpec((CHUNK_ROWS, D_MODEL), lambda c, *_: (c, 0)),
        scratch_shapes=[
            pltpu.VMEM((GATHER_SLOTS, CHUNK_ROWS, XS_WIDTH), BF16),
            pltpu.SemaphoreType.DMA((GATHER_SLOTS,)),
            pltpu.VMEM((D_MODEL, D_EXPERT), F32),
            pltpu.VMEM((D_MODEL, D_EXPERT), F32),
            pltpu.VMEM((D_EXPERT, D_MODEL), F32),
            pltpu.SemaphoreType.DMA((3,)),
            pltpu.VMEM((D_MODEL, D_EXPERT), BF16),
            pltpu.VMEM((D_MODEL, D_EXPERT), BF16),
            pltpu.VMEM((D_EXPERT, D_MODEL), BF16),
        ],
    )
    return pl.pallas_call(
        functools.partial(_expert_ffn_kernel, layer=layer),
        grid_spec=grid_spec,
        out_shape=jax.ShapeDtypeStruct((MAX_CHUNKS * CHUNK_ROWS, D_MODEL), BF16),
        compiler_params=_params(("arbitrary",)),
        name="expert_ffn",
    )(src, chunk_expert, chunk_first, chunk_valid, chunk_next, xs, w_gate, w_up, w_down)


def _combine(dst_ref, ys_hbm, lp_ref, x_ref, buf, sem, first_tile=0):
    slot = _prefetched_gather(dst_ref, TILE_GRANS, ys_hbm, buf, sem, first_tile)
    lp = lp_ref[...]
    p_idx = lax.broadcasted_iota(jnp.int32, (ROW_TILE, LOCAL_ROWS), 1).astype(F32)
    pick = jnp.where((p_idx == lp[:, 0:1]) | (p_idx == lp[:, 1:2]), 1.0, 0.0).astype(BF16)
    return x_ref[...] + _dot(pick, buf[slot])


def _combine_norm_kernel(dst_ref, ys_hbm, lp_ref, x_ref, g_ref, whi_ref, wlo_ref, b_ref,
                         x_out, h_out, s_out, buf, sem):
    x = _combine(dst_ref, ys_hbm, lp_ref, x_ref, buf, sem)
    x_out[...] = x
    h = _rms(x, g_ref[...])
    h_out[...] = h.astype(BF16)
    s_out[...] = _small_proj(h, whi_ref, wlo_ref, b_ref)


def _combine_final_kernel(dst_ref, ys_hbm, lp_ref, x_ref, g_ref, o_ref, buf, sem, *, first_tile):
    o_ref[...] = _rms(_combine(dst_ref, ys_hbm, lp_ref, x_ref, buf, sem, first_tile), g_ref[...])


_COMBINE_SCRATCH = [pltpu.VMEM((GATHER_SLOTS, LOCAL_ROWS, D_MODEL), BF16),
                    pltpu.SemaphoreType.DMA((GATHER_SLOTS,))]


def _combine_norm(dst, ys, lp, x, g, whi, wlo, b, layer):
    def row(width):
        return pl.BlockSpec((ROW_TILE, width), lambda i, dst: (i, 0))

    grid_spec = pltpu.PrefetchScalarGridSpec(
        num_scalar_prefetch=1,
        grid=(N_TILES,),
        in_specs=[pl.BlockSpec(memory_space=pl.ANY), row(LANES), row(D_MODEL)] + _norm_param_specs(layer),
        out_specs=[row(D_MODEL), row(D_MODEL), row(LANES)],
        scratch_shapes=_COMBINE_SCRATCH,
    )
    return pl.pallas_call(
        _combine_norm_kernel,
        grid_spec=grid_spec,
        out_shape=[
            jax.ShapeDtypeStruct((N_ROWS, D_MODEL), F32),
            jax.ShapeDtypeStruct((N_ROWS, D_MODEL), BF16),
            jax.ShapeDtypeStruct((N_ROWS, LANES), F32),
        ],
        compiler_params=_params(("arbitrary",)),
        name="combine_norm",
    )(dst, ys, lp, x, g, whi, wlo, b)


def _combine_final(dst, ys, lp, x, g, first_tile, n_tiles):
    def row(width):
        return pl.BlockSpec((ROW_TILE, width), lambda i, dst: (i + first_tile, 0))

    grid_spec = pltpu.PrefetchScalarGridSpec(
        num_scalar_prefetch=1,
        grid=(n_tiles,),
        in_specs=[pl.BlockSpec(memory_space=pl.ANY), row(LANES), row(D_MODEL),
                  pl.BlockSpec((1, D_MODEL), lambda i, dst: (0, 0))],
        out_specs=pl.BlockSpec((ROW_TILE, D_MODEL), lambda i, dst: (i, 0)),
        scratch_shapes=_COMBINE_SCRATCH,
    )
    return pl.pallas_call(
        functools.partial(_combine_final_kernel, first_tile=first_tile),
        grid_spec=grid_spec,
        out_shape=jax.ShapeDtypeStruct((n_tiles * ROW_TILE, D_MODEL), F32),
        compiler_params=_params(("arbitrary",)),
        name="combine_final",
    )(dst, ys, lp, x, g)


N_SRC = MAX_CHUNKS * CHUNK_GRANS
N_DST = N_TILES * TILE_GRANS


def _granule_kernel(gcnt_ref, src_ref, dst_ref, expert_ref, first_ref, valid_ref, next_ref, local_ref):
    def fill(ref, n, value):
        def body(k, carry):
            ref[k] = value
            return carry
        lax.fori_loop(0, n, body, 0, unroll=8)

    fill(src_ref, N_SRC, TILE_GRANS - 1)
    fill(dst_ref, N_DST, 0)
    fill(expert_ref, MAX_CHUNKS, N_EXPERTS - 1)
    fill(first_ref, MAX_CHUNKS, 0)
    fill(valid_ref, MAX_CHUNKS, 0)
    fill(next_ref, MAX_CHUNKS + 1, -1)
    fill(local_ref, N_TILES, 0)

    def per_expert(e, carry):
        chunk, prev_first = carry
        slot0 = chunk * CHUNK_GRANS

        def per_tile(t, pos):
            n = gcnt_ref[t * N_EXPERTS + e]
            local0 = t * TILE_GRANS + local_ref[t]

            def per_granule(k, c2):
                src_ref[pos + k] = local0 + k
                dst_ref[local0 + k] = pos + k
                return c2

            lax.fori_loop(0, n, per_granule, 0)
            local_ref[t] = local_ref[t] + n
            return pos + n

        end = lax.fori_loop(0, N_TILES, per_tile, slot0)
        n_chunks = (end - slot0 + CHUNK_GRANS - 1) // CHUNK_GRANS

        def per_chunk(k, c2):
            expert_ref[chunk + k] = e
            valid_ref[chunk + k] = 1
            return c2

        lax.fori_loop(0, n_chunks, per_chunk, 0)
        owns = n_chunks > 0

        @pl.when(owns)
        def _():
            first_ref[chunk] = 1
            next_ref[prev_first] = e

        return chunk + n_chunks, jnp.where(owns, chunk, prev_first)

    lax.fori_loop(0, N_EXPERTS, per_expert, (0, MAX_CHUNKS))


def _granule_lists(gcnt):
    smem = pl.BlockSpec(memory_space=pltpu.SMEM)
    sizes = (N_SRC, N_DST, MAX_CHUNKS, MAX_CHUNKS, MAX_CHUNKS, MAX_CHUNKS + 1)
    return pl.pallas_call(
        _granule_kernel,
        in_specs=[smem],
        out_specs=[smem] * len(sizes),
        out_shape=[jax.ShapeDtypeStruct((n,), jnp.int32) for n in sizes],
        scratch_shapes=[pltpu.SMEM((N_TILES,), jnp.int32)],
        name="granule_lists",
    )(gcnt.reshape(N_TILES * N_EXPERTS))


def _pad_lanes(w):
    return jnp.pad(w, [(0, 0)] * (w.ndim - 1) + [(0, LANES - w.shape[-1])])


def _toeplitz_bias(table, n_q, n_k, reach):
    period = 1
    while period < n_q + n_k:
        period *= 2
    j = jnp.arange(period)
    d = jnp.where(j < n_k, j, j - period)
    u = table.reshape(-1, table.shape[-1])[:, jnp.clip(reach - d, -REL_CLIP, REL_CLIP) + REL_CLIP]
    flat = jnp.tile(u, (1, n_q))[:, :n_q * (period - 1)]
    return flat.reshape(table.shape[:-1] + (n_q, period - 1))[..., :n_k]


def _prompt_bias(table):
    bias = _toeplitz_bias(table, ATT_Q, ATT_K, ATT_REACH)
    qc = jnp.arange(ATT_Q)[:, None] // CHUNK
    m = jnp.arange(ATT_K)[None, :]
    kc = m // CHUNK
    band = (kc >= qc) & (kc <= qc + BAND_CHUNKS)
    first_key = ATT_REACH - ATT_Q * jnp.arange(ATT_VARIANTS)
    ok = band[None] & (m[None] >= first_key[:, None, None])
    return jnp.where(ok[None, :, None], bias[:, None], NEG)


def _step_bias(table, past):
    bias = _toeplitz_bias(table, DEC_SEQ, past + DEC_SEQ, past)
    return bias[..., :past], bias[..., past:]


def kernel(x_prompt, x_sample, state_conv, cache_k, cache_v, state_C, state_n, state_m, norm_mix, norm_ffn, w_in, conv_w, rel_bias, gate_bias, mlstm_norm, w_proj_conv, w_proj_att, w_proj_mlstm, w_out, router_group, router_group_bias, router_expert, router_expert_bias, w_gate, w_up, w_down, norm_final):
    past = cache_k.shape[2]
    xp = x_prompt.reshape(SEQ, D_MODEL)
    xs = x_sample.reshape(N_SAMPLE, D_MODEL)
    state_n5 = state_n.reshape(DEPTH, DEC_BATCH, N_HEADS_MLSTM, 1, HEAD_DIM_MLSTM)
    state_m5 = jnp.broadcast_to(state_m[..., None, None], (DEPTH, DEC_BATCH, N_HEADS_MLSTM, 1, LANES))

    w_in_t = jnp.swapaxes(w_in, 1, 2)

    g_mix = norm_mix[:, None, :]
    g_ffn = norm_ffn[:, None, :]
    gnorm = mlstm_norm[:, None, :]
    gif_w = jnp.swapaxes(w_in_t[:, MAIN_WIDTH:MAIN_WIDTH + GIF_WIDTH, :], 1, 2)
    gif_hi, gif_lo = _split_hi_lo(_pad_lanes(gif_w))
    gif_b = _pad_lanes(gate_bias[:, None, :])
    r_hi, r_lo = _split_hi_lo(_pad_lanes(jnp.concatenate([router_group, router_expert], axis=2)))
    r_b = _pad_lanes(jnp.concatenate([router_group_bias, router_expert_bias], axis=1)[:, None, :])
    hist = jnp.zeros((DEPTH, DEC_BATCH, DEC_SEQ, D_CONV), F32)
    s1 = hist.at[:, :, 0].set(state_conv[:, :, 1]).reshape(DEPTH, N_SAMPLE, D_CONV)
    s2 = (hist.at[:, :, 0].set(state_conv[:, :, 0]).at[:, :, 1].set(state_conv[:, :, 1])
          .reshape(DEPTH, N_SAMPLE, D_CONV))
    merge_w = (w_proj_conv.astype(BF16), w_proj_att.astype(BF16), w_proj_mlstm.astype(BF16), w_out.astype(BF16))

    x_mid = moe = None
    outs = {k: [] for k in ("p_conv", "p_k", "p_v", "p_C", "p_n", "p_m", "s_conv", "s_k", "s_v", "s_C", "s_n", "s_m")}
    for l in range(DEPTH):
        if l == 0:
            h, gif = _norm_first(xp, xs, g_mix, gif_hi, gif_lo, gif_b, l)
            residual = (xp, xs, 0)
        else:
            x, h, gif = _combine_norm(*moe, x_mid, g_mix, gif_hi, gif_lo, gif_b, l)
            residual = (x, x, N_PROMPT_TILES)
        z = _project(h, w_in_t, l, 0, MAIN_WIDTH)
        zg = _project(h, w_in_t, l, MAIN_WIDTH + GIF_WIDTH, N_BRANCH * D_MODEL)

        yb_p, yc_p, p_c, p_n, p_m = _mixers_prompt(z, _prompt_bias(rel_bias[l:l + 1]), gif, gnorm, l)
        bias_past, bias_new = _step_bias(rel_bias[l:l + 1], past)
        yb_s = _attn_step(z, cache_k, cache_v, bias_past, bias_new, l)
        yc_s, s_c, s_n, s_m = _mlstm_step(z, gif, gnorm, state_C, state_n5, state_m5, l)

        x_mid, tails = _merge(z, zg, s1, s2, conv_w, yb_p, yb_s, yc_p, yc_s, *residual, *merge_w, l)

        xs_local, lp, gcnt = _route_dispatch(x_mid, g_ffn, r_hi, r_lo, r_b, l)
        gcnt = gcnt[:, 0, N_GROUPS:N_GROUPS + N_EXPERTS].astype(jnp.int32)
        src, dst, chunk_expert, chunk_first, chunk_valid, chunk_next = _granule_lists(gcnt)
        ys = _expert_ffn(src, chunk_expert, chunk_first, chunk_valid, chunk_next, xs_local, w_gate, w_up, w_down, l)
        moe = (dst, ys, lp)

        keep = min(ATT_REACH, SEQ)
        k_rows = z[SEQ - keep:, COL_K * D_ATT:(COL_K + 1) * D_ATT].astype(F32)
        v_rows = z[SEQ - keep:, COL_V * D_ATT:(COL_V + 1) * D_ATT].astype(F32)
        outs["p_conv"].append(tails[N_PROMPT_TILES - 1, :, GROUPS_PER_TILE - 1][None])
        outs["p_k"].append(k_rows[:keep].reshape(1, keep, N_HEADS_ATT, HEAD_DIM_ATT))
        outs["p_v"].append(v_rows[:keep].reshape(1, keep, N_HEADS_ATT, HEAD_DIM_ATT))
        outs["p_C"].append(p_c[None])
        outs["p_n"].append(p_n[:, 0][None])
        outs["p_m"].append(p_m[:, 0, 0][None])
        outs["s_conv"].append(jnp.swapaxes(tails[N_PROMPT_TILES], 0, 1))
        outs["s_k"].append(k_rows[keep:].reshape(DEC_BATCH, DEC_SEQ, N_HEADS_ATT, HEAD_DIM_ATT))
        outs["s_v"].append(v_rows[keep:].reshape(DEC_BATCH, DEC_SEQ, N_HEADS_ATT, HEAD_DIM_ATT))
        outs["s_C"].append(s_c)
        outs["s_n"].append(s_n[:, :, 0])
        outs["s_m"].append(s_m[:, :, 0, 0])

    g_fin = norm_final[None, :]
    y_prompt = _combine_final(*moe, x_mid, g_fin, 0, N_PROMPT_TILES).reshape(x_prompt.shape)
    y_sample = _combine_final(*moe, x_mid, g_fin, N_PROMPT_TILES, 1).reshape(x_sample.shape)
    st = {k: jnp.stack(v) for k, v in outs.items()}
    return (y_prompt, y_sample, st["p_conv"], st["p_k"], st["p_v"], st["p_C"], st["p_n"], st["p_m"],
            st["s_conv"], st["s_k"], st["s_v"], st["s_C"], st["s_n"], st["s_m"])
```

```python
import functools
import math

import jax
import jax.numpy as jnp
from jax import lax
from jax.experimental import pallas as pl
from jax.experimental.pallas import tpu as pltpu

D_MODEL = 2048
SEQ = 8192
DEPTH = 2
DEC_BATCH = 8
DEC_SEQ = 32
N_SAMPLE = DEC_BATCH * DEC_SEQ
N_ROWS = SEQ + N_SAMPLE

CHUNK = 64
D_CONV = 1024
CONV_W = 3
N_HEADS_ATT = 8
HEAD_DIM_ATT = 128
D_ATT = N_HEADS_ATT * HEAD_DIM_ATT
BAND_CHUNKS = 8
ATT_REACH = BAND_CHUNKS * CHUNK
REL_CLIP = 128
N_HEADS_MLSTM = 4
HEAD_DIM_MLSTM = 256
D_MLSTM = N_HEADS_MLSTM * HEAD_DIM_MLSTM
N_BRANCH = 3
MAIN_WIDTH = 3 * D_CONV + 3 * D_ATT + 4 * D_MLSTM
GIF_WIDTH = 2 * N_HEADS_MLSTM
N_GROUPS = 4
EXPERTS_PER_GROUP = 4
N_EXPERTS = N_GROUPS * EXPERTS_PER_GROUP
D_EXPERT = 512
EPS = 1e-6

LANES = 128
BF16_ROWS = 16
ROW_TILE = 256
N_TILES = N_ROWS // ROW_TILE
N_PROMPT_TILES = SEQ // ROW_TILE
MM_ROWS = 1056
MM_COLS = 1024
ATT_Q = 256
ATT_K = ATT_Q + ATT_REACH
ATT_VARIANTS = ATT_REACH // ATT_Q + 1
MLSTM_L = 256
NEG = -1e30
VMEM_LIMIT = 56 * 1024 * 1024

GRAN = BF16_ROWS
XS_WIDTH = D_MODEL + LANES
MAX_RUN_PAD = N_EXPERTS * (GRAN - 1)
TILE_GRANS = (2 * ROW_TILE + MAX_RUN_PAD) // GRAN + 1
LOCAL_ROWS = TILE_GRANS * GRAN
CHUNK_GRANS = 16
CHUNK_ROWS = CHUNK_GRANS * GRAN
MAX_CHUNKS = (N_TILES * (TILE_GRANS - 1)) // CHUNK_GRANS + N_EXPERTS

COL_XA, COL_GB, COL_GC, COL_Q, COL_K, COL_V, COL_QM, COL_KM, COL_VM, COL_OM = range(10)

F32 = jnp.float32
BF16 = jnp.bfloat16


def _params(sem):
    return pltpu.CompilerParams(dimension_semantics=sem, vmem_limit_bytes=VMEM_LIMIT)


def _dot(a, b):
    return jnp.dot(a, b, preferred_element_type=F32)


def _dot_nt(a, b):
    return lax.dot_general(a, b, (((1,), (1,)), ((), ())), preferred_element_type=F32)


def _dot_tn(a, b):
    return lax.dot_general(a, b, (((0,), (0,)), ((), ())), preferred_element_type=F32)


def _split_hi_lo(w):
    hi = w.astype(BF16)
    lo = (w - hi.astype(F32)).astype(BF16)
    return hi, lo


def _sigmoid(x):
    return 1.0 / (1.0 + jnp.exp(-x))


def _log_sigmoid(x):
    return jnp.minimum(x, 0.0) - jnp.log1p(jnp.exp(-jnp.abs(x)))


def _rms(x, g):
    ms = jnp.mean(x * x, axis=-1, keepdims=True)
    return x * lax.rsqrt(ms + EPS) * g


def _small_proj(h, whi_ref, wlo_ref, b_ref):
    h_hi = h.astype(BF16)
    h_lo = (h - h_hi.astype(F32)).astype(BF16)
    whi = whi_ref[...]
    return _dot(h_hi, whi) + _dot(h_lo, whi) + _dot(h_hi, wlo_ref[...]) + b_ref[...]


def _norm_first_kernel(xp_ref, xs_ref, g_ref, whi_ref, wlo_ref, b_ref, h_out, s_out):
    i = pl.program_id(0)
    x = jnp.where(i < N_PROMPT_TILES, xp_ref[...], xs_ref[...])
    h = _rms(x, g_ref[...])
    h_out[...] = h.astype(BF16)
    s_out[...] = _small_proj(h, whi_ref, wlo_ref, b_ref)


def _route(logits):
    lane = lax.broadcasted_iota(jnp.int32, logits.shape, 1)
    is_g = lane < N_GROUPS
    gl = jnp.where(is_g, logits, NEG)
    gmax = jnp.max(gl, axis=1, keepdims=True)
    g_sel = jnp.min(jnp.where(is_g & (gl == gmax), lane, LANES), axis=1, keepdims=True)
    p_g = 1.0 / jnp.sum(jnp.where(is_g, jnp.exp(gl - gmax), 0.0), axis=1, keepdims=True)
    e_lane = lane - N_GROUPS
    in_g = (e_lane >= 0) & (e_lane < N_EXPERTS) & ((e_lane // EXPERTS_PER_GROUP) == g_sel)
    e1 = jnp.max(jnp.where(in_g, logits, NEG), axis=1, keepdims=True)
    i1 = jnp.min(jnp.where(in_g & (logits == e1), lane, LANES), axis=1, keepdims=True)
    rest = in_g & (lane != i1)
    e2 = jnp.max(jnp.where(rest, logits, NEG), axis=1, keepdims=True)
    i2 = jnp.min(jnp.where(rest & (logits == e2), lane, LANES), axis=1, keepdims=True)
    r = jnp.exp(e2 - e1)
    w1 = p_g / (1.0 + r)
    w2 = w1 * r
    return i1, i2, w1, w2


def _bf16_pieces(w):
    a = w.astype(BF16).astype(F32)
    b = (w - a).astype(BF16).astype(F32)
    return a, b, w - a - b


def _route_dispatch_tiles_lockstep(xs, g_ref, whi_ref, wlo_ref, b_ref, outs):
    tiles = range(len(xs))
    lane = lax.broadcasted_iota(jnp.int32, (ROW_TILE, LANES), 1)
    t_idx = lax.broadcasted_iota(jnp.int32, (ROW_TILE, ROW_TILE), 0)
    s_idx = lax.broadcasted_iota(jnp.int32, (ROW_TILE, ROW_TILE), 1)
    earlier = jnp.where(s_idx < t_idx, 1.0, 0.0).astype(BF16)
    a_idx = lax.broadcasted_iota(jnp.int32, (LANES, LANES), 0)
    b_idx = lax.broadcasted_iota(jnp.int32, (LANES, LANES), 1)
    before = jnp.where(a_idx < b_idx, 1.0, 0.0).astype(BF16)
    p_idx = lax.broadcasted_iota(jnp.int32, (LOCAL_ROWS, ROW_TILE), 0).astype(F32)

    hs = [_rms(xs[t], g_ref[...]) for t in tiles]
    logits = [_small_proj(hs[t], whi_ref, wlo_ref, b_ref) for t in tiles]
    routes = [_route(logits[t]) for t in tiles]

    lps = []
    for t in tiles:
        i1, i2, _, _ = routes[t]
        o1 = lane == i1
        o2 = lane == i2
        onehot = jnp.where(o1 | o2, 1.0, 0.0)
        rank = _dot(earlier, onehot.astype(BF16))
        gcnt = jnp.floor((jnp.sum(onehot, axis=0, keepdims=True) + (GRAN - 1)) * (1.0 / GRAN))
        gcnt8 = jnp.broadcast_to(gcnt, (8, LANES))
        run_start = _dot(gcnt8.astype(BF16), before)[0:1, :] * GRAN
        pos = run_start + rank
        lpos1 = jnp.sum(jnp.where(o1, pos, 0.0), axis=1, keepdims=True)
        lpos2 = jnp.sum(jnp.where(o2, pos, 0.0), axis=1, keepdims=True)
        lp = jnp.where(lane == 0, lpos1, jnp.where(lane == 1, lpos2, -1.0))
        outs[t][1][...] = lp
        outs[t][2][...] = gcnt8
        lps.append(lp)

    rows = []
    for t in tiles:
        i1, i2, w1, w2 = routes[t]
        meta = jnp.zeros((ROW_TILE, LANES), F32)
        fields = (((i1 - N_GROUPS).astype(F32),) + _bf16_pieces(w1)
                  + ((i2 - N_GROUPS).astype(F32),) + _bf16_pieces(w2))
        for k, val in enumerate(fields):
            meta = jnp.where(lane == k, val, meta)
        rows.append(jnp.concatenate([hs[t].astype(BF16), meta.astype(BF16)], axis=1))

    perms = []
    for t in tiles:
        lp_t = lps[t].T
        perms.append(jnp.where((p_idx == lp_t[0:1, :]) | (p_idx == lp_t[1:2, :]), 1.0, 0.0).astype(BF16))
    for t in tiles:
        outs[t][0][...] = _dot(perms[t], rows[t]).astype(BF16)


def _row_spec(width):
    return pl.BlockSpec((ROW_TILE, width), lambda i: (i, 0))


def _const_spec(shape):
    return pl.BlockSpec(shape, lambda i: (0,) * len(shape))


def _layer_spec(layer, shape, **kwargs):
    return pl.BlockSpec((None,) + tuple(shape), lambda *_: (layer,) + (0,) * len(shape), **kwargs)


def _norm_param_specs(layer):
    return [_layer_spec(layer, (1, D_MODEL)), _layer_spec(layer, (D_MODEL, LANES)),
            _layer_spec(layer, (D_MODEL, LANES)), _layer_spec(layer, (1, LANES))]


def _norm_first(xp, xs, g, whi, wlo, b, layer):
    return pl.pallas_call(
        _norm_first_kernel,
        grid=(N_TILES,),
        in_specs=[
            pl.BlockSpec((ROW_TILE, D_MODEL), lambda i: (jnp.minimum(i, N_PROMPT_TILES - 1), 0)),
            _const_spec((ROW_TILE, D_MODEL)),
        ] + _norm_param_specs(layer),
        out_specs=[_row_spec(D_MODEL), _row_spec(LANES)],
        out_shape=[
            jax.ShapeDtypeStruct((N_ROWS, D_MODEL), BF16),
            jax.ShapeDtypeStruct((N_ROWS, LANES), F32),
        ],
        compiler_params=_params(("parallel",)),
        name="norm_first",
    )(xp, xs, g, whi, wlo, b)


DISPATCH_TILES = 3


def _route_dispatch_kernel(x_ref, g_ref, whi_ref, wlo_ref, b_ref, xs_out, lp_out, cnt_out):
    tiles = range(DISPATCH_TILES)
    _route_dispatch_tiles_lockstep(
        [x_ref[t * ROW_TILE:(t + 1) * ROW_TILE, :] for t in tiles], g_ref, whi_ref, wlo_ref, b_ref,
        [(xs_out.at[pl.ds(t * LOCAL_ROWS, LOCAL_ROWS), :], lp_out.at[pl.ds(t * ROW_TILE, ROW_TILE), :], cnt_out.at[t])
         for t in tiles])


def _route_dispatch(x, g, whi, wlo, b, layer):
    def spec(rows, width):
        return pl.BlockSpec((DISPATCH_TILES * rows, width), lambda i: (i, 0))

    return pl.pallas_call(
        _route_dispatch_kernel,
        grid=(N_TILES // DISPATCH_TILES,),
        in_specs=[spec(ROW_TILE, D_MODEL)] + _norm_param_specs(layer),
        out_specs=[
            spec(LOCAL_ROWS, XS_WIDTH),
            spec(ROW_TILE, LANES),
            pl.BlockSpec((DISPATCH_TILES, 8, LANES), lambda i: (i, 0, 0)),
        ],
        out_shape=[
            jax.ShapeDtypeStruct((N_TILES * LOCAL_ROWS, XS_WIDTH), BF16),
            jax.ShapeDtypeStruct((N_ROWS, LANES), F32),
            jax.ShapeDtypeStruct((N_TILES, 8, LANES), F32),
        ],
        compiler_params=_params(("parallel",)),
        name="route_dispatch",
    )(x, g, whi, wlo, b)


def _mm_kernel(h_ref, wt_ref, o_ref, wb_ref):
    @pl.when(pl.program_id(1) == 0)
    def _():
        wb_ref[...] = wt_ref[0].T.astype(BF16)

    o_ref[...] = _dot(h_ref[...], wb_ref[...]).astype(o_ref.dtype)


def _project(h, wt, layer, row0, n_cols):
    return pl.pallas_call(
        _mm_kernel,
        grid=(n_cols // MM_COLS, N_ROWS // MM_ROWS),
        in_specs=[
            pl.BlockSpec((MM_ROWS, D_MODEL), lambda j, i: (i, 0)),
            pl.BlockSpec((pl.Element(1), pl.Element(MM_COLS), pl.Element(D_MODEL)),
                         lambda j, i: (layer, pl.multiple_of(row0 + j * MM_COLS, 8), 0)),
        ],
        out_specs=pl.BlockSpec((MM_ROWS, MM_COLS), lambda j, i: (i, j)),
        out_shape=jax.ShapeDtypeStruct((N_ROWS, n_cols), BF16),
        scratch_shapes=[pltpu.VMEM((D_MODEL, MM_COLS), BF16)],
        compiler_params=_params(("parallel", "arbitrary")),
        name="project",
    )(h, wt)


def _attn_prompt_head(h, q_ref, k_refs, v_refs, bias_ref, o_ref):
    sl = slice(h * HEAD_DIM_ATT, (h + 1) * HEAD_DIM_ATT)
    q = q_ref[:, sl]
    kk = jnp.concatenate([r[:, sl] for r in k_refs], axis=0)
    vv = jnp.concatenate([r[:, sl] for r in v_refs], axis=0)
    s = _dot_nt(q, kk) * (HEAD_DIM_ATT ** -0.5) + bias_ref[h]
    mx = jnp.max(s, axis=1, keepdims=True)
    p = jnp.exp(s - mx).astype(BF16)
    od = _dot(p, jnp.concatenate([vv, jnp.ones((ATT_K, HEAD_DIM_ATT), BF16)], axis=1))
    o = od[:, :HEAD_DIM_ATT] * (1.0 / od[:, HEAD_DIM_ATT:HEAD_DIM_ATT + 1])
    o_ref[:, sl] = o.astype(o_ref.dtype)


def _attn_step_kernel(q_ref, k_ref, v_ref, ck_ref, cv_ref, bp_ref, bn_ref, o_ref):
    scale = HEAD_DIM_ATT ** -0.5
    ck = jnp.swapaxes(ck_ref[...], 0, 1).astype(BF16)
    cv = jnp.swapaxes(cv_ref[...], 0, 1).astype(BF16)
    for h in range(N_HEADS_ATT):
        sl = slice(h * HEAD_DIM_ATT, (h + 1) * HEAD_DIM_ATT)
        q = q_ref[:, sl]
        s_past = _dot_nt(q, ck[h]) * scale + bp_ref[h]
        s_new = _dot_nt(q, k_ref[:, sl]) * scale + bn_ref[h]
        mx = jnp.maximum(jnp.max(s_past, axis=1, keepdims=True), jnp.max(s_new, axis=1, keepdims=True))
        p_past = jnp.exp(s_past - mx)
        p_new = jnp.exp(s_new - mx)
        den = jnp.sum(p_past, axis=1, keepdims=True) + jnp.sum(p_new, axis=1, keepdims=True)
        o = _dot(p_past.astype(BF16), cv[h]) + _dot(p_new.astype(BF16), v_ref[:, sl])
        o_ref[:, sl] = (o / den).astype(o_ref.dtype)


def _attn_step(z, cache_k, cache_v, bias_past, bias_new, layer):
    first = SEQ // DEC_SEQ
    past = cache_k.shape[2]

    def z_spec(col):
        return pl.BlockSpec((DEC_SEQ, D_ATT), lambda b: (first + b, col))

    cache_spec = pl.BlockSpec((None, None, past, N_HEADS_ATT, HEAD_DIM_ATT), lambda b: (layer, b, 0, 0, 0))
    return pl.pallas_call(
        _attn_step_kernel,
        grid=(DEC_BATCH,),
        in_specs=[
            z_spec(COL_Q), z_spec(COL_K), z_spec(COL_V), cache_spec, cache_spec,
            _layer_spec(0, (N_HEADS_ATT, DEC_SEQ, past)),
            _layer_spec(0, (N_HEADS_ATT, DEC_SEQ, DEC_SEQ)),
        ],
        out_specs=pl.BlockSpec((DEC_SEQ, D_ATT), lambda b: (b, 0)),
        out_shape=jax.ShapeDtypeStruct((N_SAMPLE, D_ATT), BF16),
        compiler_params=_params(("parallel",)),
        name="attn_step",
    )(z, z, z, cache_k, cache_v, bias_past, bias_new)


def _mlstm_block(q, k, v, om, gates, gnorm, c0, n0, m0):
    li_col, lf_col, li_row, lf_row = gates
    L = q.shape[0]
    kscale = HEAD_DIM_MLSTM ** -0.5
    t_idx = lax.broadcasted_iota(jnp.int32, (L, L), 0)
    s_idx = lax.broadcasted_iota(jnp.int32, (L, L), 1)
    causal = s_idx <= t_idx
    b_col = jnp.sum(jnp.where(causal, lf_row, 0.0), axis=1, keepdims=True)
    b_row = jnp.sum(jnp.where(t_idx <= s_idx, lf_col, 0.0), axis=0, keepdims=True)
    d = jnp.where(causal, b_col - b_row + li_row, NEG)
    inter = b_col + m0
    m_col = jnp.maximum(inter, jnp.max(d, axis=1, keepdims=True))
    w = jnp.exp(d - (m_col - math.log(kscale)))
    sc = jnp.exp(inter - m_col)
    qk = _dot_nt(q, k) * w
    num = sc * _dot_nt(q, c0.astype(BF16)) + _dot(qk.astype(BF16), v)
    qn = _dot_nt(q, jnp.broadcast_to(n0, (BF16_ROWS, HEAD_DIM_MLSTM)).astype(BF16))[:, 0:1]
    den = sc * qn + jnp.sum(qk, axis=1, keepdims=True)
    hh = num * (1.0 / jnp.maximum(jnp.abs(den), jnp.exp(-m_col)))
    mu = jnp.mean(hh, axis=1, keepdims=True)
    cen = hh - mu
    var = jnp.mean(cen * cen, axis=1, keepdims=True)
    y = _sigmoid(om.astype(F32)) * (cen * lax.rsqrt(var + EPS) * gnorm)
    m_last = m_col[L - 1:L, :]
    b_last = b_col[L - 1:L, :]
    decay = jnp.exp(b_last + m0 - m_last)
    ws = jnp.exp(b_last - b_col + li_col - m_last) * kscale
    vs = (v.astype(F32) * ws).astype(BF16)
    c1 = decay * c0 + _dot_tn(vs, k)
    n1 = decay * n0 + jnp.sum(k.astype(F32) * ws, axis=0, keepdims=True)
    return y, c1, n1, m_last


def _log_gates(gif):
    lane = lax.broadcasted_iota(jnp.int32, gif.shape, 1)
    lg = jnp.where(lane < N_HEADS_MLSTM, gif, _log_sigmoid(gif))
    return lg, lg.T


def _gate_views(log_gates, head):
    lg, lg_t = log_gates
    f = N_HEADS_MLSTM + head
    return lg[:, head:head + 1], lg[:, f:f + 1], lg_t[head:head + 1, :], lg_t[f:f + 1, :]


def _mixers_prompt_kernel(q_ref, k0_ref, k1_ref, k2_ref, v0_ref, v1_ref, v2_ref, bias_ref,
                          qm_ref, km_ref, vm_ref, om_ref, gif_ref, gn_ref,
                          yb_ref, yc_ref, c_out, n_out, m_out, c_scr, n_scr, m_scr):
    step = pl.program_id(0)

    @pl.when(step == 0)
    def _():
        c_scr[...] = jnp.zeros_like(c_scr)
        n_scr[...] = jnp.zeros_like(n_scr)
        m_scr[...] = jnp.zeros_like(m_scr)

    log_gates = _log_gates(gif_ref[...])
    att_per_mlstm = N_HEADS_ATT // N_HEADS_MLSTM
    for h in range(N_HEADS_MLSTM):
        _attn_prompt_head(h * att_per_mlstm, q_ref, (k0_ref, k1_ref, k2_ref), (v0_ref, v1_ref, v2_ref),
                          bias_ref, yb_ref)
        sl = slice(h * HEAD_DIM_MLSTM, (h + 1) * HEAD_DIM_MLSTM)
        y, c1, n1, m1 = _mlstm_block(qm_ref[:, sl], km_ref[:, sl], vm_ref[:, sl], om_ref[:, sl],
                                     _gate_views(log_gates, h), gn_ref[:, sl],
                                     c_scr[h], n_scr[h], m_scr[h][:, :1])
        yc_ref[:, sl] = y.astype(yc_ref.dtype)
        c_scr[h] = c1
        n_scr[h] = n1
        m_scr[h] = jnp.broadcast_to(m1, (1, LANES))
        for ha in range(h * att_per_mlstm + 1, (h + 1) * att_per_mlstm):
            _attn_prompt_head(ha, q_ref, (k0_ref, k1_ref, k2_ref), (v0_ref, v1_ref, v2_ref), bias_ref, yb_ref)

    @pl.when(step == pl.num_programs(0) - 1)
    def _():
        c_out[...] = c_scr[...]
        n_out[...] = n_scr[...]
        m_out[...] = m_scr[...]


def _mixers_prompt(z, bias, gif, gnorm, layer):
    assert ATT_Q == MLSTM_L

    def z_spec(col):
        return pl.BlockSpec((ATT_Q, D_ATT), lambda j: (j, col))

    def kv_spec(col, back):
        return pl.BlockSpec((ATT_Q, D_ATT), lambda j: (jnp.maximum(j - back, 0), col))

    state_shapes = [
        jax.ShapeDtypeStruct((N_HEADS_MLSTM, HEAD_DIM_MLSTM, HEAD_DIM_MLSTM), F32),
        jax.ShapeDtypeStruct((N_HEADS_MLSTM, 1, HEAD_DIM_MLSTM), F32),
        jax.ShapeDtypeStruct((N_HEADS_MLSTM, 1, LANES), F32),
    ]
    return pl.pallas_call(
        _mixers_prompt_kernel,
        grid=(SEQ // ATT_Q,),
        in_specs=[
            z_spec(COL_Q),
            kv_spec(COL_K, 2), kv_spec(COL_K, 1), kv_spec(COL_K, 0),
            kv_spec(COL_V, 2), kv_spec(COL_V, 1), kv_spec(COL_V, 0),
            pl.BlockSpec((None, None, N_HEADS_ATT, ATT_Q, ATT_K),
                         lambda j: (0, jnp.minimum(j, ATT_VARIANTS - 1), 0, 0, 0)),
            z_spec(COL_QM), z_spec(COL_KM), z_spec(COL_VM), z_spec(COL_OM),
            pl.BlockSpec((MLSTM_L, LANES), lambda j: (j, 0)),
            _layer_spec(layer, (1, D_MLSTM)),
        ],
        out_specs=[pl.BlockSpec((ATT_Q, D_ATT), lambda j: (j, 0)), pl.BlockSpec((MLSTM_L, D_MLSTM), lambda j: (j, 0))]
        + [_const_spec(s.shape) for s in state_shapes],
        out_shape=[jax.ShapeDtypeStruct((SEQ, D_ATT), BF16), jax.ShapeDtypeStruct((SEQ, D_MLSTM), BF16)]
        + state_shapes,
        scratch_shapes=[pltpu.VMEM(s.shape, F32) for s in state_shapes],
        compiler_params=_params(("arbitrary",)),
        name="mixers_prompt",
    )(z, z, z, z, z, z, z, bias, z, z, z, z, gif, gnorm)


def _mlstm_step_kernel(q_ref, k_ref, v_ref, om_ref, gif_ref, gn_ref, c_ref, n_ref, m_ref,
                       y_ref, c_out, n_out, m_out):
    log_gates = _log_gates(gif_ref[...])
    for h in range(N_HEADS_MLSTM):
        sl = slice(h * HEAD_DIM_MLSTM, (h + 1) * HEAD_DIM_MLSTM)
        y, c1, n1, m1 = _mlstm_block(q_ref[:, sl], k_ref[:, sl], v_ref[:, sl], om_ref[:, sl],
                                     _gate_views(log_gates, h), gn_ref[:, sl],
                                     c_ref[h], n_ref[h], m_ref[h][:, :1])
        y_ref[:, sl] = y.astype(y_ref.dtype)
        c_out[h] = c1
        n_out[h] = n1
        m_out[h] = jnp.broadcast_to(m1, (1, LANES))


def _mlstm_step(z, gif, gnorm, state_c, state_n, state_m, layer):
    first = SEQ // DEC_SEQ

    def z_spec(col):
        return pl.BlockSpec((DEC_SEQ, D_MLSTM), lambda b: (first + b, col))

    def st_in(shape):
        return pl.BlockSpec((None, None) + shape, lambda b: (layer, b) + (0,) * len(shape))

    def st_out(shape):
        return pl.BlockSpec((None,) + shape, lambda b: (b,) + (0,) * len(shape))

    shapes = [(N_HEADS_MLSTM, HEAD_DIM_MLSTM, HEAD_DIM_MLSTM), (N_HEADS_MLSTM, 1, HEAD_DIM_MLSTM),
              (N_HEADS_MLSTM, 1, LANES)]
    return pl.pallas_call(
        _mlstm_step_kernel,
        grid=(DEC_BATCH,),
        in_specs=[
            z_spec(COL_QM), z_spec(COL_KM), z_spec(COL_VM), z_spec(COL_OM),
            pl.BlockSpec((DEC_SEQ, LANES), lambda b: (first + b, 0)),
            _layer_spec(layer, (1, D_MLSTM)),
        ] + [st_in(s) for s in shapes],
        out_specs=[pl.BlockSpec((DEC_SEQ, D_MLSTM), lambda b: (b, 0))] + [st_out(s) for s in shapes],
        out_shape=[jax.ShapeDtypeStruct((N_SAMPLE, D_MLSTM), BF16)]
        + [jax.ShapeDtypeStruct((DEC_BATCH,) + s, F32) for s in shapes],
        compiler_params=_params(("parallel",)),
        name="mlstm_step",
    )(z, z, z, z, gif, gnorm, state_c, state_n, state_m)


HALO = 16
GROUPS_PER_TILE = ROW_TILE // DEC_SEQ


def _merge_kernel(xa_ref, gb_ref, gc_ref, xah_ref, gch_ref, s1_ref, s2_ref, cw_ref,
                  ybp_ref, ybs_ref, ycp_ref, ycs_ref, zg_ref, xp_ref, xs_ref,
                  wpc_ref, wpa_ref, wpm_ref, wout_ref, x_out, tail_out, u_scr):
    i = pl.program_id(0)
    is_s = i >= N_PROMPT_TILES
    row = lax.broadcasted_iota(jnp.int32, (ROW_TILE, 1), 0)
    pos = jnp.where(is_s, row % DEC_SEQ, row)
    u = gc_ref[...].astype(F32) * xa_ref[...].astype(F32)
    u_halo = gch_ref[...].astype(F32) * xah_ref[...].astype(F32)
    keep = jnp.logical_and(i > 0, jnp.logical_not(is_s))
    h1 = jnp.where(keep, u_halo[HALO - 1:HALO, :], 0.0)
    h2 = jnp.where(keep, u_halo[HALO - 2:HALO - 1, :], 0.0)
    f1 = jnp.where(is_s, s1_ref[...], h1)
    f2 = jnp.where(is_s, s2_ref[...], jnp.where(row == 0, h2, h1))
    u_m1 = jnp.where(pos >= 1, pltpu.roll(u, 1, 0), f1)
    u_m2 = jnp.where(pos >= 2, pltpu.roll(u, 2, 0), f2)
    cw = cw_ref[...]
    y = cw[0:1, :] * u_m2 + cw[1:2, :] * u_m1 + cw[2:3, :] * u
    ya = gb_ref[...].astype(F32) * y

    u_scr[...] = u
    for g in range(GROUPS_PER_TILE):
        for j in range(CONV_W - 1):
            src = (g + 1) * DEC_SEQ - (CONV_W - 1) + j
            tail_out[j, g:g + 1, :] = u_scr[src:src + 1, :]

    yb = jnp.where(is_s, ybs_ref[...], ybp_ref[...])
    yc = jnp.where(is_s, ycs_ref[...], ycp_ref[...])
    g = _sigmoid(zg_ref[...].astype(F32))
    merged = (g[:, 0:D_MODEL] * _dot(ya.astype(BF16), wpc_ref[...])
              + g[:, D_MODEL:2 * D_MODEL] * _dot(yb, wpa_ref[...])
              + g[:, 2 * D_MODEL:3 * D_MODEL] * _dot(yc, wpm_ref[...]))
    x = jnp.where(is_s, xs_ref[...], xp_ref[...])
    x_out[...] = x + _dot(merged.astype(BF16), wout_ref[...])


def _merge(z, zg, s1, s2, conv_w, yb_p, yb_s, yc_p, yc_s, x_p, x_s, x_s_block, wpc, wpa, wpm, wout, layer):
    halo_blocks = ROW_TILE // HALO

    def z_spec(col):
        return pl.BlockSpec((ROW_TILE, D_CONV), lambda i: (i, col))

    def halo_spec(col):
        return pl.BlockSpec((HALO, D_CONV), lambda i: (jnp.maximum(i * halo_blocks - 1, 0), col))

    def prompt_spec(width):
        return pl.BlockSpec((ROW_TILE, width), lambda i: (jnp.minimum(i, N_PROMPT_TILES - 1), 0))

    def weight_spec(shape):
        return _layer_spec(layer, shape, pipeline_mode=pl.Buffered(1))

    return pl.pallas_call(
        _merge_kernel,
        grid=(N_TILES,),
        in_specs=[
            z_spec(COL_XA), z_spec(COL_GB), z_spec(COL_GC), halo_spec(COL_XA), halo_spec(COL_GC),
            _layer_spec(layer, (ROW_TILE, D_CONV)), _layer_spec(layer, (ROW_TILE, D_CONV)),
            _layer_spec(layer, (CONV_W, D_CONV)),
            prompt_spec(D_ATT), _const_spec((ROW_TILE, D_ATT)),
            prompt_spec(D_MLSTM), _const_spec((ROW_TILE, D_MLSTM)),
            _row_spec(N_BRANCH * D_MODEL),
            prompt_spec(D_MODEL), pl.BlockSpec((ROW_TILE, D_MODEL), lambda i: (x_s_block, 0)),
            weight_spec((D_CONV, D_MODEL)), weight_spec((D_ATT, D_MODEL)), weight_spec((D_MLSTM, D_MODEL)),
            weight_spec((D_MODEL, D_MODEL)),
        ],
        out_specs=[
            _row_spec(D_MODEL),
            pl.BlockSpec((None, CONV_W - 1, GROUPS_PER_TILE, D_CONV), lambda i: (i, 0, 0, 0)),
        ],
        out_shape=[
            jax.ShapeDtypeStruct((N_ROWS, D_MODEL), F32),
            jax.ShapeDtypeStruct((N_TILES, CONV_W - 1, GROUPS_PER_TILE, D_CONV), F32),
        ],
        scratch_shapes=[pltpu.VMEM((ROW_TILE, D_CONV), F32)],
        compiler_params=_params(("parallel",)),
        name="merge",
    )(z, z, z, z, z, s1, s2, conv_w, yb_p, yb_s, yc_p, yc_s, zg, x_p, x_s, wpc, wpa, wpm, wout)


def _granule_copies(idx_ref, first, n, src_hbm, buf, sem, slot):
    return [
        pltpu.make_async_copy(
            src_hbm.at[pl.ds(pl.multiple_of(idx_ref[first + k] * GRAN, GRAN), GRAN), :],
            buf.at[slot, pl.ds(k * GRAN, GRAN), :],
            sem.at[slot])
        for k in range(n)
    ]


GATHER_AHEAD = 2
GATHER_SLOTS = GATHER_AHEAD + 1


def _prefetched_gather(idx_ref, n, src_hbm, buf, sem, first_group=0, used_ref=None):
    step = pl.program_id(0)
    n_steps = pl.num_programs(0)

    def used(s):
        return True if used_ref is None else used_ref[jnp.minimum(s, n_steps - 1)] == 1

    def request(ahead):
        slot = (step + ahead) % GATHER_SLOTS
        for k, cp in enumerate(_granule_copies(idx_ref, (first_group + step + ahead) * n, n, src_hbm, buf, sem, slot)):
            cp.start(priority=k % 2)

    for ahead in range(GATHER_AHEAD):
        pl.when((step == 0) & (ahead < n_steps) & used(ahead))(functools.partial(request, ahead))
    pl.when((step + GATHER_AHEAD < n_steps) & used(step + GATHER_AHEAD))(functools.partial(request, GATHER_AHEAD))

    slot = step % GATHER_SLOTS

    @pl.when(used(step))
    def _():
        for cp in _granule_copies(idx_ref, (first_group + step) * n, n, src_hbm, buf, sem, slot):
            cp.wait()

    return slot


def _expert_ffn_kernel(src_ref, expert_ref, first_ref, valid_ref, next_ref, xs_hbm, wg_hbm, wu_hbm, wd_hbm,
                       o_ref, buf, sem, wg_st, wu_st, wd_st, wsem, wg_bf, wu_bf, wd_bf, *, layer):
    c = pl.program_id(0)
    slot = _prefetched_gather(src_ref, CHUNK_GRANS, xs_hbm, buf, sem, used_ref=valid_ref)

    def weight_copies(e):
        return [pltpu.make_async_copy(wg_hbm.at[layer, e], wg_st, wsem.at[0]),
                pltpu.make_async_copy(wu_hbm.at[layer, e], wu_st, wsem.at[1]),
                pltpu.make_async_copy(wd_hbm.at[layer, e], wd_st, wsem.at[2])]

    @pl.when(c == 0)
    def _():
        for cp in weight_copies(expert_ref[0]):
            cp.start()

    @pl.when(first_ref[c] == 1)
    def _():
        for cp in weight_copies(expert_ref[c]):
            cp.wait()
        wg_bf[...] = wg_st[...].astype(BF16)
        wu_bf[...] = wu_st[...].astype(BF16)
        wd_bf[...] = wd_st[...].astype(BF16)

        @pl.when(next_ref[c] >= 0)
        def _():
            for cp in weight_copies(next_ref[c]):
                cp.start()

    @pl.when(valid_ref[c] == 1)
    def _():
        rows = buf[slot]
        x = rows[:, :D_MODEL]
        meta = rows[:, D_MODEL:].astype(F32)
        w_first = meta[:, 1:2] + meta[:, 2:3] + meta[:, 3:4]
        w_second = meta[:, 5:6] + meta[:, 6:7] + meta[:, 7:8]
        w = jnp.where(meta[:, 0:1] == expert_ref[c].astype(F32), w_first, w_second)
        gate = _dot(x, wg_bf[...])
        a = gate * _sigmoid(gate) * _dot(x, wu_bf[...]) * w
        o_ref[...] = _dot(a.astype(BF16), wd_bf[...]).astype(o_ref.dtype)

    @pl.when(valid_ref[c] == 0)
    def _():
        o_ref[...] = jnp.zeros_like(o_ref)


def _expert_ffn(src, chunk_expert, chunk_first, chunk_valid, chunk_next, xs, w_gate, w_up, w_down, layer):
    any_spec = pl.BlockSpec(memory_space=pl.ANY)
    grid_spec = pltpu.PrefetchScalarGridSpec(
        num_scalar_prefetch=5,
        grid=(MAX_CHUNKS,),
        in_specs=[any_spec, any_spec, any_spec, any_spec],
        out_specs=pl.BlockSpec((CHUNK_ROWS, D_MODEL), lambda c, *_: (c, 0)),
        scratch_shapes=[
            pltpu.VMEM((GATHER_SLOTS, CHUNK_ROWS, XS_WIDTH), BF16),
            pltpu.SemaphoreType.DMA((GATHER_SLOTS,)),
            pltpu.VMEM((D_MODEL, D_EXPERT), F32),
            pltpu.VMEM((D_MODEL, D_EXPERT), F32),
            pltpu.VMEM((D_EXPERT, D_MODEL), F32),
            pltpu.SemaphoreType.DMA((3,)),
            pltpu.VMEM((D_MODEL, D_EXPERT), BF16),
            pltpu.VMEM((D_MODEL, D_EXPERT), BF16),
            pltpu.VMEM((D_EXPERT, D_MODEL), BF16),
        ],
    )
    return pl.pallas_call(
        functools.partial(_expert_ffn_kernel, layer=layer),
        grid_spec=grid_spec,
        out_shape=jax.ShapeDtypeStruct((MAX_CHUNKS * CHUNK_ROWS, D_MODEL), BF16),
        compiler_params=_params(("arbitrary",)),
        name="expert_ffn",
    )(src, chunk_expert, chunk_first, chunk_valid, chunk_next, xs, w_gate, w_up, w_down)


def _combine(dst_ref, ys_hbm, lp_ref, x_ref, buf, sem, first_tile=0):
    slot = _prefetched_gather(dst_ref, TILE_GRANS, ys_hbm, buf, sem, first_tile)
    lp = lp_ref[...]
    p_idx = lax.broadcasted_iota(jnp.int32, (ROW_TILE, LOCAL_ROWS), 1).astype(F32)
    pick = jnp.where((p_idx == lp[:, 0:1]) | (p_idx == lp[:, 1:2]), 1.0, 0.0).astype(BF16)
    return x_ref[...] + _dot(pick, buf[slot])


def _combine_norm_kernel(dst_ref, ys_hbm, lp_ref, x_ref, g_ref, whi_ref, wlo_ref, b_ref,
                         x_out, h_out, s_out, buf, sem):
    x = _combine(dst_ref, ys_hbm, lp_ref, x_ref, buf, sem)
    x_out[...] = x
    h = _rms(x, g_ref[...])
    h_out[...] = h.astype(BF16)
    s_out[...] = _small_proj(h, whi_ref, wlo_ref, b_ref)


def _combine_final_kernel(dst_ref, ys_hbm, lp_ref, x_ref, g_ref, o_ref, buf, sem, *, first_tile):
    o_ref[...] = _rms(_combine(dst_ref, ys_hbm, lp_ref, x_ref, buf, sem, first_tile), g_ref[...])


_COMBINE_SCRATCH = [pltpu.VMEM((GATHER_SLOTS, LOCAL_ROWS, D_MODEL), BF16),
                    pltpu.SemaphoreType.DMA((GATHER_SLOTS,))]


def _combine_norm(dst, ys, lp, x, g, whi, wlo, b, layer):
    def row(width):
        return pl.BlockSpec((ROW_TILE, width), lambda i, dst: (i, 0))

    grid_spec = pltpu.PrefetchScalarGridSpec(
        num_scalar_prefetch=1,
        grid=(N_TILES,),
        in_specs=[pl.BlockSpec(memory_space=pl.ANY), row(LANES), row(D_MODEL)] + _norm_param_specs(layer),
        out_specs=[row(D_MODEL), row(D_MODEL), row(LANES)],
        scratch_shapes=_COMBINE_SCRATCH,
    )
    return pl.pallas_call(
        _combine_norm_kernel,
        grid_spec=grid_spec,
        out_shape=[
            jax.ShapeDtypeStruct((N_ROWS, D_MODEL), F32),
            jax.ShapeDtypeStruct((N_ROWS, D_MODEL), BF16),
            jax.ShapeDtypeStruct((N_ROWS, LANES), F32),
        ],
        compiler_params=_params(("arbitrary",)),
        name="combine_norm",
    )(dst, ys, lp, x, g, whi, wlo, b)


def _combine_final(dst, ys, lp, x, g, first_tile, n_tiles):
    def row(width):
        return pl.BlockSpec((ROW_TILE, width), lambda i, dst: (i + first_tile, 0))

    grid_spec = pltpu.PrefetchScalarGridSpec(
        num_scalar_prefetch=1,
        grid=(n_tiles,),
        in_specs=[pl.BlockSpec(memory_space=pl.ANY), row(LANES), row(D_MODEL),
                  pl.BlockSpec((1, D_MODEL), lambda i, dst: (0, 0))],
        out_specs=pl.BlockSpec((ROW_TILE, D_MODEL), lambda i, dst: (i, 0)),
        scratch_shapes=_COMBINE_SCRATCH,
    )
    return pl.pallas_call(
        functools.partial(_combine_final_kernel, first_tile=first_tile),
        grid_spec=grid_spec,
        out_shape=jax.ShapeDtypeStruct((n_tiles * ROW_TILE, D_MODEL), F32),
        compiler_params=_params(("arbitrary",)),
        name="combine_final",
    )(dst, ys, lp, x, g)


N_SRC = MAX_CHUNKS * CHUNK_GRANS
N_DST = N_TILES * TILE_GRANS


def _granule_kernel(gcnt_ref, src_ref, dst_ref, expert_ref, first_ref, valid_ref, next_ref, local_ref):
    def fill(ref, n, value):
        def body(k, carry):
            ref[k] = value
            return carry
        lax.fori_loop(0, n, body, 0, unroll=8)

    fill(src_ref, N_SRC, TILE_GRANS - 1)
    fill(dst_ref, N_DST, 0)
    fill(expert_ref, MAX_CHUNKS, N_EXPERTS - 1)
    fill(first_ref, MAX_CHUNKS, 0)
    fill(valid_ref, MAX_CHUNKS, 0)
    fill(next_ref, MAX_CHUNKS + 1, -1)
    fill(local_ref, N_TILES, 0)

    def per_expert(e, carry):
        chunk, prev_first = carry
        slot0 = chunk * CHUNK_GRANS

        def per_tile(t, pos):
            n = gcnt_ref[t * N_EXPERTS + e]
            local0 = t * TILE_GRANS + local_ref[t]

            def per_granule(k, c2):
                src_ref[pos + k] = local0 + k
                dst_ref[local0 + k] = pos + k
                return c2

            lax.fori_loop(0, n, per_granule, 0)
            local_ref[t] = local_ref[t] + n
            return pos + n

        end = lax.fori_loop(0, N_TILES, per_tile, slot0)
        n_chunks = (end - slot0 + CHUNK_GRANS - 1) // CHUNK_GRANS

        def per_chunk(k, c2):
            expert_ref[chunk + k] = e
            valid_ref[chunk + k] = 1
            return c2

        lax.fori_loop(0, n_chunks, per_chunk, 0)
        owns = n_chunks > 0

        @pl.when(owns)
        def _():
            first_ref[chunk] = 1
            next_ref[prev_first] = e

        return chunk + n_chunks, jnp.where(owns, chunk, prev_first)

    lax.fori_loop(0, N_EXPERTS, per_expert, (0, MAX_CHUNKS))


def _granule_lists(gcnt):
    smem = pl.BlockSpec(memory_space=pltpu.SMEM)
    sizes = (N_SRC, N_DST, MAX_CHUNKS, MAX_CHUNKS, MAX_CHUNKS, MAX_CHUNKS + 1)
    return pl.pallas_call(
        _granule_kernel,
        in_specs=[smem],
        out_specs=[smem] * len(sizes),
        out_shape=[jax.ShapeDtypeStruct((n,), jnp.int32) for n in sizes],
        scratch_shapes=[pltpu.SMEM((N_TILES,), jnp.int32)],
        name="granule_lists",
    )(gcnt.reshape(N_TILES * N_EXPERTS))


def _pad_lanes(w):
    return jnp.pad(w, [(0, 0)] * (w.ndim - 1) + [(0, LANES - w.shape[-1])])


def _toeplitz_bias(table, n_q, n_k, reach):
    period = 1
    while period < n_q + n_k:
        period *= 2
    j = jnp.arange(period)
    d = jnp.where(j < n_k, j, j - period)
    u = table.reshape(-1, table.shape[-1])[:, jnp.clip(reach - d, -REL_CLIP, REL_CLIP) + REL_CLIP]
    flat = jnp.tile(u, (1, n_q))[:, :n_q * (period - 1)]
    return flat.reshape(table.shape[:-1] + (n_q, period - 1))[..., :n_k]


def _prompt_bias(table):
    bias = _toeplitz_bias(table, ATT_Q, ATT_K, ATT_REACH)
    qc = jnp.arange(ATT_Q)[:, None] // CHUNK
    m = jnp.arange(ATT_K)[None, :]
    kc = m // CHUNK
    band = (kc >= qc) & (kc <= qc + BAND_CHUNKS)
    first_key = ATT_REACH - ATT_Q * jnp.arange(ATT_VARIANTS)
    ok = band[None] & (m[None] >= first_key[:, None, None])
    return jnp.where(ok[None, :, None], bias[:, None], NEG)


def _step_bias(table, past):
    bias = _toeplitz_bias(table, DEC_SEQ, past + DEC_SEQ, past)
    return bias[..., :past], bias[..., past:]


def kernel(x_prompt, x_sample, state_conv, cache_k, cache_v, state_C, state_n, state_m, norm_mix, norm_ffn, w_in, conv_w, rel_bias, gate_bias, mlstm_norm, w_proj_conv, w_proj_att, w_proj_mlstm, w_out, router_group, router_group_bias, router_expert, router_expert_bias, w_gate, w_up, w_down, norm_final):
    past = cache_k.shape[2]
    xp = x_prompt.reshape(SEQ, D_MODEL)
    xs = x_sample.reshape(N_SAMPLE, D_MODEL)
    state_n5 = state_n.reshape(DEPTH, DEC_BATCH, N_HEADS_MLSTM, 1, HEAD_DIM_MLSTM)
    state_m5 = jnp.broadcast_to(state_m[..., None, None], (DEPTH, DEC_BATCH, N_HEADS_MLSTM, 1, LANES))

    w_in_t = jnp.swapaxes(w_in, 1, 2)

    g_mix = norm_mix[:, None, :]
    g_ffn = norm_ffn[:, None, :]
    gnorm = mlstm_norm[:, None, :]
    gif_w = jnp.swapaxes(w_in_t[:, MAIN_WIDTH:MAIN_WIDTH + GIF_WIDTH, :], 1, 2)
    gif_hi, gif_lo = _split_hi_lo(_pad_lanes(gif_w))
    gif_b = _pad_lanes(gate_bias[:, None, :])
    r_hi, r_lo = _split_hi_lo(_pad_lanes(jnp.concatenate([router_group, router_expert], axis=2)))
    r_b = _pad_lanes(jnp.concatenate([router_group_bias, router_expert_bias], axis=1)[:, None, :])
    hist = jnp.zeros((DEPTH, DEC_BATCH, DEC_SEQ, D_CONV), F32)
    s1 = hist.at[:, :, 0].set(state_conv[:, :, 1]).reshape(DEPTH, N_SAMPLE, D_CONV)
    s2 = (hist.at[:, :, 0].set(state_conv[:, :, 0]).at[:, :, 1].set(state_conv[:, :, 1])
          .reshape(DEPTH, N_SAMPLE, D_CONV))
    merge_w = (w_proj_conv.astype(BF16), w_proj_att.astype(BF16), w_proj_mlstm.astype(BF16), w_out.astype(BF16))

    x_mid = moe = None
    outs = {k: [] for k in ("p_conv", "p_k", "p_v", "p_C", "p_n", "p_m", "s_conv", "s_k", "s_v", "s_C", "s_n", "s_m")}
    for l in range(DEPTH):
        if l == 0:
            h, gif = _norm_first(xp, xs, g_mix, gif_hi, gif_lo, gif_b, l)
            residual = (xp, xs, 0)
        else:
            x, h, gif = _combine_norm(*moe, x_mid, g_mix, gif_hi, gif_lo, gif_b, l)
            residual = (x, x, N_PROMPT_TILES)
        z = _project(h, w_in_t, l, 0, MAIN_WIDTH)
        zg = _project(h, w_in_t, l, MAIN_WIDTH + GIF_WIDTH, N_BRANCH * D_MODEL)

        yb_p, yc_p, p_c, p_n, p_m = _mixers_prompt(z, _prompt_bias(rel_bias[l:l + 1]), gif, gnorm, l)
        bias_past, bias_new = _step_bias(rel_bias[l:l + 1], past)
        yb_s = _attn_step(z, cache_k, cache_v, bias_past, bias_new, l)
        yc_s, s_c, s_n, s_m = _mlstm_step(z, gif, gnorm, state_C, state_n5, state_m5, l)

        x_mid, tails = _merge(z, zg, s1, s2, conv_w, yb_p, yb_s, yc_p, yc_s, *residual, *merge_w, l)

        xs_local, lp, gcnt = _route_dispatch(x_mid, g_ffn, r_hi, r_lo, r_b, l)
        gcnt = gcnt[:, 0, N_GROUPS:N_GROUPS + N_EXPERTS].astype(jnp.int32)
        src, dst, chunk_expert, chunk_first, chunk_valid, chunk_next = _granule_lists(gcnt)
        ys = _expert_ffn(src, chunk_expert, chunk_first, chunk_valid, chunk_next, xs_local, w_gate, w_up, w_down, l)
        moe = (dst, ys, lp)

        keep = min(ATT_REACH, SEQ)
        k_rows = z[SEQ - keep:, COL_K * D_ATT:(COL_K + 1) * D_ATT].astype(F32)
        v_rows = z[SEQ - keep:, COL_V * D_ATT:(COL_V + 1) * D_ATT].astype(F32)
        outs["p_conv"].append(tails[N_PROMPT_TILES - 1, :, GROUPS_PER_TILE - 1][None])
        outs["p_k"].append(k_rows[:keep].reshape(1, keep, N_HEADS_ATT, HEAD_DIM_ATT))
        outs["p_v"].append(v_rows[:keep].reshape(1, keep, N_HEADS_ATT, HEAD_DIM_ATT))
        outs["p_C"].append(p_c[None])
        outs["p_n"].append(p_n[:, 0][None])
        outs["p_m"].append(p_m[:, 0, 0][None])
        outs["s_conv"].append(jnp.swapaxes(tails[N_PROMPT_TILES], 0, 1))
        outs["s_k"].append(k_rows[keep:].reshape(DEC_BATCH, DEC_SEQ, N_HEADS_ATT, HEAD_DIM_ATT))
        outs["s_v"].append(v_rows[keep:].reshape(DEC_BATCH, DEC_SEQ, N_HEADS_ATT, HEAD_DIM_ATT))
        outs["s_C"].append(s_c)
        outs["s_n"].append(s_n[:, :, 0])
        outs["s_m"].append(s_m[:, :, 0, 0])

    g_fin = norm_final[None, :]
    y_prompt = _combine_final(*moe, x_mid, g_fin, 0, N_PROMPT_TILES).reshape(x_prompt.shape)
    y_sample = _combine_final(*moe, x_mid, g_fin, N_PROMPT_TILES, 1).reshape(x_sample.shape)
    st = {k: jnp.stack(v) for k, v in outs.items()}
    return (y_prompt, y_sample, st["p_conv"], st["p_k"], st["p_v"], st["p_C"], st["p_n"], st["p_m"],
            st["s_conv"], st["s_k"], st["s_v"], st["s_C"], st["s_n"], st["s_m"])
```
